```python
import jax, jax.numpy as jnp
from jax import lax
import numpy as np

D_MODEL = 1024
BATCH = 8
SEQ = 4096
DEPTH = 2

GRID_W = 64
CTX_LEN = 256
CHUNK = 128
A_HEADS = 4
A_HEAD_DIM = 128
A_W = A_HEADS * A_HEAD_DIM
B_GROUPS = 4
B_GROUP_DIM = 128
B_W = B_GROUPS * B_GROUP_DIM
AB_IN = 3 * A_W + 2 * B_W
AB_OUT = A_W + B_W
C_HEADS = 16
C_KV_HEADS = 4
C_GROUP = C_HEADS // C_KV_HEADS
C_HEAD_DIM = 64
C_Q_W = C_HEADS * C_HEAD_DIM
C_KV_W = C_KV_HEADS * C_HEAD_DIM
C_IN = 2 * C_Q_W + 2 * C_KV_W
WINDOW = 128
Q_BLOCK = 128
ROPE_BASE = 10000.0
NORM_EPS = 1e-6
NEG_INF = -1e30
N_EVEN = (DEPTH + 1) // 2
N_ODD = DEPTH // 2

kernel_name = 'hybrid_gmlp_fnet_swa_prefix_trunk'


def rms_norm(x, g):
    xf = x.astype(jnp.float32)
    y = xf * lax.rsqrt(jnp.mean(xf * xf, axis=-1, keepdims=True) + NORM_EPS)
    return (y * g.astype(jnp.float32)).astype(x.dtype)


def layer_norm(x, g):
    xf = x.astype(jnp.float32)
    mu = jnp.mean(xf, axis=-1, keepdims=True)
    xc = xf - mu
    y = xc * lax.rsqrt(jnp.mean(xc * xc, axis=-1, keepdims=True) + NORM_EPS)
    return (y * g.astype(jnp.float32)).astype(x.dtype)


def adaln(cond, w, b):
    m = jax.nn.silu(cond) @ w + b
    return jnp.split(m, 3, axis=-1)


def rope_1d(x, pos):
    nf = x.shape[-1] // 2
    inv = ROPE_BASE ** (-jnp.arange(nf, dtype=jnp.float32) / nf)
    ang = pos.astype(jnp.float32)[:, None] * inv[None, :]
    cos = jnp.cos(ang)[None, :, None, :].astype(x.dtype)
    sin = jnp.sin(ang)[None, :, None, :].astype(x.dtype)
    x1, x2 = x[..., :nf], x[..., nf:]
    return jnp.concatenate([x1 * cos - x2 * sin, x2 * cos + x1 * sin], axis=-1)


def rope_2d(x, row, col):
    half = x.shape[-1] // 2
    return jnp.concatenate([rope_1d(x[..., :half], row), rope_1d(x[..., half:], col)], axis=-1)


def sink_softmax(scores, sink):
    s = jnp.broadcast_to(sink, scores.shape[:-1] + (1,))
    p = jax.nn.softmax(jnp.concatenate([s, scores], axis=-1), axis=-1)
    return p[..., 1:]


def mixer_ab(h, w_in, v_g, s_w, s_b, w_out):
    bsz, L, _ = h.shape
    z = h @ w_in
    u, v, g_a, x_b, g_b = jnp.split(z, [A_W, 2 * A_W, 3 * A_W, 3 * A_W + B_W], axis=-1)
    v = layer_norm(v, v_g).reshape(bsz, L // CHUNK, CHUNK, A_HEADS, A_HEAD_DIM)
    sv = jnp.einsum('bnphc,hqp->bnqhc', v, s_w) + s_b.T[:, :, None]
    y_a = u * sv.reshape(bsz, L, A_W) * jax.nn.silu(g_a)
    xb = x_b.reshape(bsz, L, B_GROUPS, B_GROUP_DIM).astype(jnp.float32)
    f = jnp.real(jnp.fft.fft2(xb, axes=(1, 3), norm='ortho')).astype(h.dtype).reshape(bsz, L, B_W)
    y_b = f * jax.nn.silu(g_b)
    return jnp.concatenate([y_a, y_b], axis=-1) @ w_out


def split_c(z):
    return jnp.split(z, [C_Q_W, C_Q_W + C_KV_W, C_Q_W + 2 * C_KV_W], axis=-1)


def window_attention(q, k, v, k_ctx, v_ctx, sink):
    bsz, S, _, hd = q.shape
    qg = q.reshape(bsz, S, C_KV_HEADS, C_GROUP, hd)
    pad = ((0, 0), (WINDOW, WINDOW), (0, 0), (0, 0))
    kp, vp = jnp.pad(k, pad), jnp.pad(v, pad)
    kb_len = Q_BLOCK + 2 * WINDOW
    scale = hd ** -0.5
    sink_b = sink.reshape(C_KV_HEADS, C_GROUP).astype(jnp.float32)[None, :, :, None, None]

    def one_block(n):
        start = n * Q_BLOCK
        qb = lax.dynamic_slice_in_dim(qg, start, Q_BLOCK, axis=1)
        kb = lax.dynamic_slice_in_dim(kp, start, kb_len, axis=1)
        vb = lax.dynamic_slice_in_dim(vp, start, kb_len, axis=1)
        qpos = start + jnp.arange(Q_BLOCK)
        kpos = start - WINDOW + jnp.arange(kb_len)
        valid = (jnp.abs(qpos[:, None] - kpos[None, :]) <= WINDOW) & (kpos[None, :] >= 0) & (kpos[None, :] < S)
        s_win = jnp.einsum('bqkgd,bjkd->bkgqj', qb, kb).astype(jnp.float32) * scale
        s_win = jnp.where(valid, s_win, NEG_INF)
        s_ctx = jnp.einsum('bqkgd,bjkd->bkgqj', qb, k_ctx).astype(jnp.float32) * scale
        p = sink_softmax(jnp.concatenate([s_ctx, s_win], axis=-1), sink_b)
        vals = jnp.concatenate([v_ctx, vb], axis=1)
        return jnp.einsum('bkgqj,bjkd->bqkgd', p.astype(v.dtype), vals)

    out = lax.map(one_block, jnp.arange(S // Q_BLOCK))
    return jnp.moveaxis(out, 0, 1).reshape(bsz, S, C_Q_W)


def context_attention(q, k, v, sink):
    bsz, L, _, hd = q.shape
    qg = q.reshape(bsz, L, C_KV_HEADS, C_GROUP, hd)
    s = jnp.einsum('bqkgd,bjkd->bkgqj', qg, k).astype(jnp.float32) * hd ** -0.5
    sink_b = sink.reshape(C_KV_HEADS, C_GROUP).astype(jnp.float32)[None, :, :, None, None]
    p = sink_softmax(s, sink_b)
    o = jnp.einsum('bkgqj,bjkd->bqkgd', p.astype(v.dtype), v)
    return o.reshape(bsz, L, C_Q_W)


def setup_inputs(seed: int = 0) -> dict:
    key = jax.random.key(seed)
    ks = jax.random.split(key, 16)

    def nrm(k, shape, s):
        return jax.random.normal(k, shape, jnp.float32) * s

    return {
        'x': nrm(ks[0], (BATCH, SEQ, D_MODEL), 1.0),
        'c': nrm(ks[1], (BATCH, D_MODEL), 1.0),
        'ctx': nrm(ks[2], (BATCH, CTX_LEN, D_MODEL), 1.0),
        'c_ctx': nrm(ks[3], (D_MODEL,), 1.0),
        'norm_g': 1.0 + nrm(ks[4], (DEPTH, D_MODEL), 0.02),
        'ada_w': nrm(ks[5], (DEPTH, D_MODEL, 3 * D_MODEL), 0.5 * D_MODEL ** -0.5),
        'ada_b': nrm(ks[6], (DEPTH, 3 * D_MODEL), 0.02),
        'w_in_ab': nrm(ks[7], (N_EVEN, D_MODEL, AB_IN), D_MODEL ** -0.5),
        'v_norm_g': 1.0 + nrm(ks[8], (N_EVEN, A_W), 0.02),
        'spatial_w': nrm(ks[9], (N_EVEN, A_HEADS, CHUNK, CHUNK), CHUNK ** -0.5),
        'spatial_b': 1.0 + nrm(ks[10], (N_EVEN, A_HEADS, CHUNK), 0.02),
        'w_out_ab': nrm(ks[11], (N_EVEN, AB_OUT, D_MODEL), AB_OUT ** -0.5),
        'w_in_c': nrm(ks[12], (N_ODD, D_MODEL, C_IN), D_MODEL ** -0.5),
        'sink_logit': nrm(ks[13], (N_ODD, C_HEADS), 0.5),
        'w_out_c': nrm(ks[14], (N_ODD, C_Q_W, D_MODEL), C_Q_W ** -0.5),
        'final_g': 1.0 + nrm(ks[15], (D_MODEL,), 0.02),
    }


def reference(x, c, ctx, c_ctx, norm_g, ada_w, ada_b, w_in_ab, v_norm_g, spatial_w, spatial_b,
              w_out_ab, w_in_c, sink_logit, w_out_c, final_g):
    bsz, S, _ = x.shape
    rows = S // GRID_W
    row = jnp.repeat(jnp.arange(rows), GRID_W)
    col = jnp.tile(jnp.arange(GRID_W), rows)

    for layer in range(DEPTH):
        need_ctx_out = layer < DEPTH - 1
        shift, scale, gate = adaln(c[:, None, :], ada_w[layer], ada_b[layer])
        shift_c, scale_c, gate_c = adaln(c_ctx, ada_w[layer], ada_b[layer])
        h = rms_norm(x, norm_g[layer]) * (1.0 + scale) + shift
        if layer % 2 == 0:
            i = layer // 2
            y = mixer_ab(h, w_in_ab[i], v_norm_g[i], spatial_w[i], spatial_b[i], w_out_ab[i])
            if need_ctx_out:
                hc = rms_norm(ctx, norm_g[layer]) * (1.0 + scale_c) + shift_c
                yc = mixer_ab(hc, w_in_ab[i], v_norm_g[i], spatial_w[i], spatial_b[i], w_out_ab[i])
                ctx = ctx + gate_c * yc
            x = x + gate * y
        else:
            i = layer // 2
            w_in = w_in_c[i]
            hc = rms_norm(ctx, norm_g[layer]) * (1.0 + scale_c) + shift_c
            q, k, v, g = split_c(h @ w_in)
            q = rope_2d(q.reshape(bsz, S, C_HEADS, C_HEAD_DIM), row, col)
            k = rope_2d(k.reshape(bsz, S, C_KV_HEADS, C_HEAD_DIM), row, col)
            v = v.reshape(bsz, S, C_KV_HEADS, C_HEAD_DIM)
            Lc = ctx.shape[1]
            if need_ctx_out:
                q_c, k_c, v_c, g_c = split_c(hc @ w_in)
            else:
                k_c, v_c = jnp.split(hc @ w_in[:, C_Q_W:C_Q_W + 2 * C_KV_W], 2, axis=-1)
            k_c = k_c.reshape(bsz, Lc, C_KV_HEADS, C_HEAD_DIM)
            v_c = v_c.reshape(bsz, Lc, C_KV_HEADS, C_HEAD_DIM)
            o = window_attention(q, k, v, k_c, v_c, sink_logit[i])
            y = (o * jax.nn.silu(g)) @ w_out_c[i]
            if need_ctx_out:
                o_c = context_attention(q_c.reshape(bsz, Lc, C_HEADS, C_HEAD_DIM), k_c, v_c, sink_logit[i])
                ctx = ctx + gate_c * ((o_c * jax.nn.silu(g_c)) @ w_out_c[i])
            x = x + gate * y

    return rms_norm(x, final_g)
```

```python
import functools
import math

import numpy as np
import jax
import jax.numpy as jnp
from jax import lax
from jax.experimental import pallas as pl
from jax.experimental.pallas import tpu as pltpu

F32 = jnp.float32
BF16 = jnp.bfloat16

D_MODEL = 1024
GRID_W = 64
CHUNK = 128
A_HEADS = 4
A_HEAD_DIM = 128
A_W = A_HEADS * A_HEAD_DIM
B_GROUPS = 4
B_GROUP_DIM = 128
B_W = B_GROUPS * B_GROUP_DIM
AB_IN = 3 * A_W + 2 * B_W
C_HEADS = 16
C_KV_HEADS = 4
C_GROUP = C_HEADS // C_KV_HEADS
C_HEAD_DIM = 64
C_Q_W = C_HEADS * C_HEAD_DIM
C_KV_W = C_KV_HEADS * C_HEAD_DIM
C_IN = 2 * C_Q_W + 2 * C_KV_W
WINDOW = 128
Q_BLOCK = 128
ROPE_BASE = 10000.0
NORM_EPS = 1e-6
NEG_INF = -1e30
LOG2E = math.log2(math.e)
RADIX = 4
MOD_ROWS = 16
V7X_VMEM_LIMIT = 56 * 1024 * 1024


def _silu(x):
    return x * (1.0 / (1.0 + jnp.exp(-x)))


def _cparams(sem):
    return pltpu.CompilerParams(dimension_semantics=sem, vmem_limit_bytes=V7X_VMEM_LIMIT)


def _channel_dft_matrix():
    n = np.arange(B_GROUP_DIM)
    ang = 2.0 * np.pi * np.outer(n, n) / B_GROUP_DIM
    return np.concatenate([np.cos(ang), np.sin(ang)], axis=1).astype(np.float32)


def _position_dft_matrix(seq_len):
    n2 = seq_len // RADIX
    idx = np.arange(n2)
    ang = 2.0 * np.pi * (np.outer(idx, idx) % n2) / n2
    norm = 1.0 / math.sqrt(seq_len * B_GROUP_DIM)
    return np.concatenate([np.cos(ang) * norm, -np.sin(ang) * norm], axis=1).astype(np.float32)


def _twiddle_tables(seq_len):
    n2 = seq_len // RADIX
    l2 = np.arange(n2)[None, :, None]
    k1 = np.arange(RADIX)[:, None, None]
    ang = 2.0 * np.pi * ((l2 * k1) % seq_len) / seq_len
    ang = np.broadcast_to(ang, (RADIX, n2, 128))
    return np.cos(ang).astype(np.float32), np.sin(ang).astype(np.float32)


def _rope_tables(seq_len):
    t = np.arange(seq_len)
    row = (t // GRID_W).astype(np.float64)
    col = (t % GRID_W).astype(np.float64)
    lane = np.arange(128)
    dd = lane % C_HEAD_DIM
    nf = C_HEAD_DIM // 4
    inv = ROPE_BASE ** (-(dd % nf).astype(np.float64) / nf)
    pos = np.where((dd < C_HEAD_DIM // 2)[None, :], row[:, None], col[:, None])
    ang = pos * inv[None, :]
    sign = np.where((dd % (2 * nf)) < nf, -1.0, 1.0)[None, :]
    return np.cos(ang).astype(np.float32), (np.sin(ang) * sign).astype(np.float32)


def _mod_kernel(c_ref, w_ref, b_ref, o_ref):
    s = _silu(c_ref[...]).astype(BF16)
    o_ref[0] = jnp.dot(s, w_ref[0].astype(BF16), preferred_element_type=F32) + b_ref[0]


def _mod_call(cc, ada_w, ada_b):
    depth = ada_w.shape[0]
    tn = 1024
    return pl.pallas_call(
        _mod_kernel,
        grid=(depth, 3 * D_MODEL // tn),
        in_specs=[
            pl.BlockSpec((MOD_ROWS, D_MODEL), lambda l, j: (0, 0)),
            pl.BlockSpec((1, D_MODEL, tn), lambda l, j: (l, 0, j)),
            pl.BlockSpec((1, 1, tn), lambda l, j: (l, 0, j)),
        ],
        out_specs=pl.BlockSpec((1, MOD_ROWS, tn), lambda l, j: (l, 0, j)),
        out_shape=jax.ShapeDtypeStruct((depth, MOD_ROWS, 3 * D_MODEL), F32),
        compiler_params=_cparams(("arbitrary", "arbitrary")),
        name="adaln_mod",
    )(cc, ada_w, ada_b.reshape(depth, 1, 3 * D_MODEL))


def _to_bf16_kernel(x_ref, o_ref):
    o_ref[...] = x_ref[...].astype(BF16)


def _to_bf16_call(a, rows_per_step):
    rows, cols = a.shape
    return pl.pallas_call(
        _to_bf16_kernel,
        grid=(rows // rows_per_step,),
        in_specs=[pl.BlockSpec((rows_per_step, cols), lambda i: (i, 0))],
        out_specs=pl.BlockSpec((rows_per_step, cols), lambda i: (i, 0)),
        out_shape=jax.ShapeDtypeStruct((rows, cols), BF16),
        compiler_params=_cparams(("parallel",)),
        name="to_bf16",
    )(a)


def _modulated_norm(x, mod_row, g):
    shift = mod_row[:, :D_MODEL]
    scale = mod_row[:, D_MODEL:2 * D_MODEL]
    ms = jnp.mean(x * x, axis=-1, keepdims=True)
    h = x * lax.rsqrt(ms + NORM_EPS) * g
    return h * (1.0 + scale) + shift


def _in_ab_kernel(x_ref, mod_ref, ng_ref, w_ref, vg_ref, sw_ref, sb_ref, cs_ref, twc_ref, tws_ref,
                  ya_ref, sgb_ref, uv_ref, *, tl2):
    rows = RADIX * tl2
    x = x_ref[0].reshape(rows, D_MODEL)
    h = _modulated_norm(x, mod_ref[0], ng_ref[...])
    z = jnp.dot(h.astype(BF16), w_ref[...], preferred_element_type=F32)

    v = z[:, A_W:2 * A_W]
    mu = jnp.mean(v, axis=-1, keepdims=True)
    vc = v - mu
    var = jnp.mean(vc * vc, axis=-1, keepdims=True)
    vn = (vc * lax.rsqrt(var + NORM_EPS) * vg_ref[...]).astype(BF16)

    piece = min(CHUNK, tl2)
    for ci in range(rows // CHUNK):
        r0 = ci * CHUNK
        for hd in range(A_HEADS):
            c0 = hd * A_HEAD_DIM
            sv = jnp.dot(sw_ref[hd], vn[r0:r0 + CHUNK, c0:c0 + A_HEAD_DIM],
                         preferred_element_type=F32) + sb_ref[hd]
            u = z[r0:r0 + CHUNK, c0:c0 + A_HEAD_DIM]
            ga = z[r0:r0 + CHUNK, 2 * A_W + c0:2 * A_W + c0 + A_HEAD_DIM]
            ya = (u * sv * _silu(ga)).astype(BF16)
            for p0 in range(0, CHUNK, piece):
                l1, off = divmod(r0 + p0, tl2)
                ya_ref[0, l1, off:off + piece, c0:c0 + A_HEAD_DIM] = ya[p0:p0 + piece]

    gb = z[:, 3 * A_W + B_W:]
    sgb_ref[0] = _silu(gb).astype(BF16).reshape(RADIX, tl2, B_W)

    xb = z[:, 3 * A_W:3 * A_W + B_W].astype(BF16)
    cs = cs_ref[...].astype(BF16)
    for g in range(B_GROUPS):
        c0 = g * B_GROUP_DIM
        ps, qs = [], []
        for j in range(RADIX):
            pq = jnp.dot(xb[j * tl2:(j + 1) * tl2, c0:c0 + B_GROUP_DIM], cs,
                         preferred_element_type=F32)
            ps.append(pq[:, :B_GROUP_DIM])
            qs.append(pq[:, B_GROUP_DIM:])
        p02, p13 = ps[0] - ps[2], ps[1] - ps[3]
        q02, q13 = qs[0] - qs[2], qs[1] - qs[3]
        pe, po = ps[0] + ps[2], ps[1] + ps[3]
        qe, qo = qs[0] + qs[2], qs[1] + qs[3]
        us = [pe + po, p02 - q13, pe - po, p02 + q13]
        vs = [qe + qo, q02 + p13, qe - qo, q02 - p13]
        for k1 in range(RADIX):
            if k1 == 0:
                ut, vt = us[0], vs[0]
            else:
                tc, ts = twc_ref[k1], tws_ref[k1]
                ut = us[k1] * tc - vs[k1] * ts
                vt = us[k1] * ts + vs[k1] * tc
            uv_ref[0, k1, 0, :, c0:c0 + B_GROUP_DIM] = ut.astype(BF16)
            uv_ref[0, k1, 1, :, c0:c0 + B_GROUP_DIM] = vt.astype(BF16)


def _in_ab_call(xs, mod, mod_row, ng, w_in, vg, sw, sb, cs, twc, tws, tl2):
    bsz, seq_len, _ = xs.shape
    n2 = seq_len // RADIX
    x4 = xs.reshape(bsz, RADIX, n2, D_MODEL)
    row_of = (lambda b: b) if mod_row is None else (lambda b: mod_row)
    const2 = lambda b, i: (0, 0)
    const3 = lambda b, i: (0, 0, 0)
    return pl.pallas_call(
        functools.partial(_in_ab_kernel, tl2=tl2),
        grid=(bsz, n2 // tl2),
        in_specs=[
            pl.BlockSpec((1, RADIX, tl2, D_MODEL), lambda b, i: (b, 0, i, 0)),
            pl.BlockSpec((1, 1, 3 * D_MODEL), lambda b, i: (row_of(b), 0, 0)),
            pl.BlockSpec((1, D_MODEL), const2),
            pl.BlockSpec((D_MODEL, AB_IN), const2),
            pl.BlockSpec((1, A_W), const2),
            pl.BlockSpec((A_HEADS, CHUNK, CHUNK), const3),
            pl.BlockSpec((A_HEADS, CHUNK, A_HEAD_DIM), const3),
            pl.BlockSpec((B_GROUP_DIM, 2 * B_GROUP_DIM), const2),
            pl.BlockSpec((RADIX, tl2, 128), lambda b, i: (0, i, 0)),
            pl.BlockSpec((RADIX, tl2, 128), lambda b, i: (0, i, 0)),
        ],
        out_specs=[
            pl.BlockSpec((1, RADIX, tl2, A_W), lambda b, i: (b, 0, i, 0)),
            pl.BlockSpec((1, RADIX, tl2, B_W), lambda b, i: (b, 0, i, 0)),
            pl.BlockSpec((1, RADIX, 2, tl2, B_W), lambda b, i: (b, 0, 0, i, 0)),
        ],
        out_shape=[
            jax.ShapeDtypeStruct((bsz, RADIX, n2, A_W), BF16),
            jax.ShapeDtypeStruct((bsz, RADIX, n2, B_W), BF16),
            jax.ShapeDtypeStruct((bsz, RADIX, 2, n2, B_W), BF16),
        ],
        compiler_params=_cparams(("parallel", "arbitrary")),
        name="in_ab",
    )(x4, mod, ng, w_in, vg, sw, sb, cs, twc, tws)


def _dft_out_kernel(g_ref, uv_ref, ya_ref, sgb_ref, x_ref, mod_ref, w_ref, o_ref):
    f = jnp.dot(g_ref[...], uv_ref[0, 0], preferred_element_type=F32)
    yb = (f * sgb_ref[0].astype(F32)).astype(BF16)
    y = jnp.dot(ya_ref[0], w_ref[:A_W], preferred_element_type=F32)
    y = y + jnp.dot(yb, w_ref[A_W:], preferred_element_type=F32)
    gate = mod_ref[0][:, 2 * D_MODEL:]
    o_ref[0] = x_ref[0] + gate * y


def _dft_out_call(gmat, uv, ya, sgb, xs, mod, mod_row, w_out):
    bsz, seq_len, _ = xs.shape
    n2 = seq_len // RADIX
    row_of = (lambda b: b) if mod_row is None else (lambda b: mod_row)
    out = pl.pallas_call(
        _dft_out_kernel,
        grid=(bsz, RADIX),
        in_specs=[
            pl.BlockSpec((n2, 2 * n2), lambda b, k: (0, 0)),
            pl.BlockSpec((1, 1, 2 * n2, B_W), lambda b, k: (b, k, 0, 0)),
            pl.BlockSpec((1, n2, A_W), lambda b, k: (b, 0, k)),
            pl.BlockSpec((1, n2, B_W), lambda b, k: (b, 0, k)),
            pl.BlockSpec((1, n2, D_MODEL), lambda b, k: (b, 0, k)),
            pl.BlockSpec((1, 1, 3 * D_MODEL), lambda b, k: (row_of(b), 0, 0)),
            pl.BlockSpec((A_W + B_W, D_MODEL), lambda b, k: (0, 0)),
        ],
        out_specs=pl.BlockSpec((1, n2, D_MODEL), lambda b, k: (b, 0, k)),
        out_shape=jax.ShapeDtypeStruct((bsz, n2, RADIX * D_MODEL), F32),
        compiler_params=_cparams(("parallel", "arbitrary")),
        name="dft_out",
    )(gmat, uv.reshape(bsz, RADIX, 2 * n2, B_W), ya.reshape(bsz, n2, RADIX * A_W),
      sgb.reshape(bsz, n2, RADIX * B_W), xs.reshape(bsz, n2, RADIX * D_MODEL), mod, w_out)
    return out.reshape(bsz, seq_len, D_MODEL)


def _mixer_ab_layer(xs, mod, mod_row, ng, w_in, vg, sw, sb, w_out, cs, tl2):
    seq_len = xs.shape[1]
    twc, tws = _twiddle_tables(seq_len)
    gmat = _position_dft_matrix(seq_len)
    gmat = _to_bf16_call(jnp.asarray(gmat), min(256, gmat.shape[0]))
    ya, sgb, uv = _in_ab_call(xs, mod, mod_row, ng, w_in, vg, sw, sb, cs,
                              jnp.asarray(twc), jnp.asarray(tws), tl2)
    return _dft_out_call(gmat, uv, ya, sgb, xs, mod, mod_row, w_out)


def _rope_block(t, cos, sin_signed, lane_lo):
    nf = C_HEAD_DIM // 4
    swapped = jnp.where(lane_lo, pltpu.roll(t, 128 - nf, axis=1), pltpu.roll(t, nf, axis=1))
    return t * cos + swapped * sin_signed


def _in_c_kernel(x_ref, mod_ref, ng_ref, w_ref, cos_ref, sin_ref, q_ref, k_ref, v_ref, sg_ref):
    h = _modulated_norm(x_ref[0], mod_ref[0], ng_ref[...])
    z = jnp.dot(h.astype(BF16), w_ref[...], preferred_element_type=F32)
    cos = cos_ref[...]
    sin = sin_ref[...]
    lane = lax.broadcasted_iota(jnp.int32, cos.shape, 1)
    lane_lo = (lane % (C_HEAD_DIM // 2)) < (C_HEAD_DIM // 4)
    qscale = C_HEAD_DIM ** -0.5 * LOG2E
    for c in range(C_Q_W // 128):
        t = _rope_block(z[:, c * 128:(c + 1) * 128], cos, sin, lane_lo)
        q_ref[0, :, c * 128:(c + 1) * 128] = (t * qscale).astype(BF16)
    for c in range(C_KV_W // 128):
        t = _rope_block(z[:, C_Q_W + c * 128:C_Q_W + (c + 1) * 128], cos, sin, lane_lo)
        k_ref[0, :, c * 128:(c + 1) * 128] = t.astype(BF16)
    v_ref[0] = z[:, C_Q_W + C_KV_W:C_Q_W + 2 * C_KV_W].astype(BF16)
    sg_ref[0] = _silu(z[:, C_Q_W + 2 * C_KV_W:]).astype(BF16)


def _in_c_call(xs, mod, ng, w_in, cos_t, sin_t, tm):
    bsz, seq_len, _ = xs.shape
    const2 = lambda b, i: (0, 0)
    row_blk = lambda w: pl.BlockSpec((1, tm, w), lambda b, i: (b, i, 0))
    return pl.pallas_call(
        _in_c_kernel,
        grid=(bsz, seq_len // tm),
        in_specs=[
            row_blk(D_MODEL),
            pl.BlockSpec((1, 1, 3 * D_MODEL), lambda b, i: (b, 0, 0)),
            pl.BlockSpec((1, D_MODEL), const2),
            pl.BlockSpec((D_MODEL, C_IN), const2),
            pl.BlockSpec((tm, 128), lambda b, i: (i, 0)),
            pl.BlockSpec((tm, 128), lambda b, i: (i, 0)),
        ],
        out_specs=[row_blk(C_Q_W), row_blk(C_KV_W), row_blk(C_KV_W), row_blk(C_Q_W)],
        out_shape=[
            jax.ShapeDtypeStruct((bsz, seq_len, C_Q_W), BF16),
            jax.ShapeDtypeStruct((bsz, seq_len, C_KV_W), BF16),
            jax.ShapeDtypeStruct((bsz, seq_len, C_KV_W), BF16),
            jax.ShapeDtypeStruct((bsz, seq_len, C_Q_W), BF16),
        ],
        compiler_params=_cparams(("parallel", "arbitrary")),
        name="in_c",
    )(xs, mod, ng, w_in, cos_t, sin_t)


def _ctx_kv_kernel(x_ref, mod_ref, ng_ref, w_ref, k_ref, v_ref):
    h = _modulated_norm(x_ref[0], mod_ref[0], ng_ref[...])
    z = jnp.dot(h.astype(BF16), w_ref[...], preferred_element_type=F32)
    k_ref[0] = z[:, :C_KV_W].astype(BF16)
    v_ref[0] = z[:, C_KV_W:].astype(BF16)


def _ctx_kv_call(ctx, mod, mod_row, ng, w_kv):
    bsz, lc, _ = ctx.shape
    return pl.pallas_call(
        _ctx_kv_kernel,
        grid=(bsz,),
        in_specs=[
            pl.BlockSpec((1, lc, D_MODEL), lambda b: (b, 0, 0)),
            pl.BlockSpec((1, 1, 3 * D_MODEL), lambda b: (mod_row, 0, 0)),
            pl.BlockSpec((1, D_MODEL), lambda b: (0, 0)),
            pl.BlockSpec((D_MODEL, 2 * C_KV_W), lambda b: (0, 0)),
        ],
        out_specs=[pl.BlockSpec((1, lc, C_KV_W), lambda b: (b, 0, 0))] * 2,
        out_shape=[jax.ShapeDtypeStruct((bsz, lc, C_KV_W), BF16)] * 2,
        compiler_params=_cparams(("parallel",)),
        name="ctx_kv",
    )(ctx, mod, ng, w_kv)


def _attn_out_kernel(sink_ref, q_ref, kp_ref, kc_ref, kn_ref, vp_ref, vc_ref, vn_ref, kx_ref, vx_ref,
                     sg_ref, x_ref, mod_ref, w_ref, fg_ref, o_ref):
    n = pl.program_id(1)
    last = pl.num_programs(1) - 1
    lc = kx_ref.shape[1]
    qi = lax.broadcasted_iota(jnp.int32, (Q_BLOCK, Q_BLOCK), 0)
    kj = lax.broadcasted_iota(jnp.int32, (Q_BLOCK, Q_BLOCK), 1)
    bias_prev = jnp.where((kj >= qi) & (n > 0), 0.0, NEG_INF).astype(F32)
    bias_next = jnp.where((kj <= qi) & (n < last), 0.0, NEG_INF).astype(F32)
    bias = jnp.concatenate(
        [jnp.zeros((Q_BLOCK, lc), F32), bias_prev, jnp.zeros((Q_BLOCK, Q_BLOCK), F32), bias_next], axis=1)

    heads = []
    for kh in range(C_KV_HEADS):
        ks = slice(kh * C_HEAD_DIM, (kh + 1) * C_HEAD_DIM)
        kk = jnp.concatenate([kx_ref[0, :, ks], kp_ref[0, :, ks], kc_ref[0, :, ks], kn_ref[0, :, ks]], axis=0)
        vv = jnp.concatenate([vx_ref[0, :, ks], vp_ref[0, :, ks], vc_ref[0, :, ks], vn_ref[0, :, ks]], axis=0)
        for j in range(C_GROUP):
            hd = kh * C_GROUP + j
            qh = q_ref[0, :, hd * C_HEAD_DIM:(hd + 1) * C_HEAD_DIM]
            s = lax.dot_general(qh, kk, (((1,), (1,)), ((), ())), preferred_element_type=F32) + bias
            sink2 = sink_ref[hd] * LOG2E
            m = jnp.maximum(jnp.max(s, axis=-1, keepdims=True), sink2)
            p = jnp.exp2(s - m)
            denom = jnp.sum(p, axis=-1, keepdims=True) + jnp.exp2(sink2 - m)
            o = jnp.dot(p.astype(BF16), vv, preferred_element_type=F32)
            heads.append(o / denom)
    o_all = jnp.concatenate(heads, axis=1)
    y = jnp.dot((o_all * sg_ref[0].astype(F32)).astype(BF16), w_ref[...], preferred_element_type=F32)
    gate = mod_ref[0][:, 2 * D_MODEL:]
    x2 = x_ref[0] + gate * y
    ms = jnp.mean(x2 * x2, axis=-1, keepdims=True)
    o_ref[0] = x2 * lax.rsqrt(ms + NORM_EPS) * fg_ref[...]


def _attn_out_call(sink, q, k, v, kx, vx, sg, xs, mod, w_out, fg):
    bsz, seq_len, _ = xs.shape
    nblk = seq_len // Q_BLOCK
    lc = kx.shape[1]
    cur = lambda w: pl.BlockSpec((1, Q_BLOCK, w), lambda b, n: (b, n, 0))
    prev = pl.BlockSpec((1, Q_BLOCK, C_KV_W), lambda b, n: (b, jnp.maximum(n - 1, 0), 0))
    nxt = pl.BlockSpec((1, Q_BLOCK, C_KV_W), lambda b, n: (b, jnp.minimum(n + 1, nblk - 1), 0))
    ctx_blk = pl.BlockSpec((1, lc, C_KV_W), lambda b, n: (b, 0, 0))
    return pl.pallas_call(
        _attn_out_kernel,
        grid=(bsz, nblk),
        in_specs=[
            pl.BlockSpec(memory_space=pltpu.SMEM),
            cur(C_Q_W), prev, cur(C_KV_W), nxt, prev, cur(C_KV_W), nxt, ctx_blk, ctx_blk,
            cur(C_Q_W), cur(D_MODEL),
            pl.BlockSpec((1, 1, 3 * D_MODEL), lambda b, n: (b, 0, 0)),
            pl.BlockSpec((C_Q_W, D_MODEL), lambda b, n: (0, 0)),
            pl.BlockSpec((1, D_MODEL), lambda b, n: (0, 0)),
        ],
        out_specs=cur(D_MODEL),
        out_shape=jax.ShapeDtypeStruct((bsz, seq_len, D_MODEL), F32),
        compiler_params=_cparams(("parallel", "arbitrary")),
        name="attn_out",
    )(sink, q, k, k, k, v, v, v, kx, vx, sg, xs, mod, w_out, fg)


def kernel(x, c, ctx, c_ctx, norm_g, ada_w, ada_b, w_in_ab, v_norm_g, spatial_w, spatial_b, w_out_ab,
           w_in_c, sink_logit, w_out_c, final_g):
    bsz, seq_len, _ = x.shape
    depth = ada_w.shape[0]
    assert depth == 2 and bsz + 1 <= MOD_ROWS
    ctx_row = bsz

    cc = jnp.concatenate([c, c_ctx[None, :], jnp.zeros((MOD_ROWS - bsz - 1, D_MODEL), F32)], axis=0)
    mod = _mod_call(cc, ada_w, ada_b)
    mod0 = mod[0].reshape(MOD_ROWS, 1, 3 * D_MODEL)
    mod1 = mod[1].reshape(MOD_ROWS, 1, 3 * D_MODEL)

    cs = jnp.asarray(_channel_dft_matrix())
    ng0 = norm_g[0].reshape(1, D_MODEL)
    ng1 = norm_g[1].reshape(1, D_MODEL)
    w_in0 = w_in_ab[0].astype(BF16)
    w_out0 = w_out_ab[0].astype(BF16)
    vg = v_norm_g[0].reshape(1, A_W)
    sw = spatial_w[0].astype(BF16)
    sb = jnp.broadcast_to(spatial_b[0][:, :, None], (A_HEADS, CHUNK, A_HEAD_DIM))

    x1 = _mixer_ab_layer(x, mod0, None, ng0, w_in0, vg, sw, sb, w_out0, cs, tl2=128)
    ctx1 = _mixer_ab_layer(ctx, mod0, ctx_row, ng0, w_in0, vg, sw, sb, w_out0, cs,
                           tl2=ctx.shape[1] // RADIX)

    w_in1 = w_in_c[0].astype(BF16)
    w_out1 = w_out_c[0].astype(BF16)
    cos_t, sin_t = _rope_tables(seq_len)
    q, k, v, sg = _in_c_call(x1, mod1, ng1, w_in1, jnp.asarray(cos_t), jnp.asarray(sin_t), tm=512)
    kx, vx = _ctx_kv_call(ctx1, mod1, ctx_row, ng1, w_in1[:, C_Q_W:C_Q_W + 2 * C_KV_W])
    return _attn_out_call(sink_logit[0], q, k, v, kx, vx, sg, x1, mod1, w_out1,
                          final_g.reshape(1, D_MODEL))
```

```python
import functools
import math

import numpy as np
import jax
import jax.numpy as jnp
from jax import lax
from jax.experimental import pallas as pl
from jax.experimental.pallas import tpu as pltpu

F32 = jnp.float32
BF16 = jnp.bfloat16

D_MODEL = 1024
GRID_W = 64
CHUNK = 128
A_HEADS = 4
A_HEAD_DIM = 128
A_W = A_HEADS * A_HEAD_DIM
B_GROUPS = 4
B_GROUP_DIM = 128
B_W = B_GROUPS * B_GROUP_DIM
AB_IN = 3 * A_W + 2 * B_W
C_HEADS = 16
C_KV_HEADS = 4
C_GROUP = C_HEADS // C_KV_HEADS
C_HEAD_DIM = 64
C_Q_W = C_HEADS * C_HEAD_DIM
C_KV_W = C_KV_HEADS * C_HEAD_DIM
C_IN = 2 * C_Q_W + 2 * C_KV_W
WINDOW = 128
Q_BLOCK = 128
ROPE_BASE = 10000.0
NORM_EPS = 1e-6
NEG_INF = -1e30
LOG2E = math.log2(math.e)
RADIX = 4
MOD_ROWS = 16
V7X_VMEM_LIMIT = 56 * 1024 * 1024


def _silu(x):
    return x * (1.0 / (1.0 + jnp.exp(-x)))


def _cparams(sem):
    return pltpu.CompilerParams(dimension_semantics=sem, vmem_limit_bytes=V7X_VMEM_LIMIT)


def _channel_dft_matrix():
    n = np.arange(B_GROUP_DIM)
    ang = 2.0 * np.pi * np.outer(n, n) / B_GROUP_DIM
    return np.concatenate([np.cos(ang), np.sin(ang)], axis=1).astype(np.float32)


def _position_dft_matrix(seq_len):
    n2 = seq_len // RADIX
    idx = np.arange(n2)
    ang = 2.0 * np.pi * (np.outer(idx, idx) % n2) / n2
    norm = 1.0 / math.sqrt(seq_len * B_GROUP_DIM)
    return np.concatenate([np.cos(ang) * norm, -np.sin(ang) * norm], axis=1).astype(np.float32)


def _twiddle_tables(seq_len):
    n2 = seq_len // RADIX
    l2 = np.arange(n2)[None, :, None]
    k1 = np.arange(RADIX)[:, None, None]
    ang = 2.0 * np.pi * ((l2 * k1) % seq_len) / seq_len
    ang = np.broadcast_to(ang, (RADIX, n2, 128))
    return np.cos(ang).astype(np.float32), np.sin(ang).astype(np.float32)


def _rope_tables(seq_len):
    t = np.arange(seq_len)
    row = (t // GRID_W).astype(np.float64)
    col = (t % GRID_W).astype(np.float64)
    lane = np.arange(128)
    dd = lane % C_HEAD_DIM
    nf = C_HEAD_DIM // 4
    inv = ROPE_BASE ** (-(dd % nf).astype(np.float64) / nf)
    pos = np.where((dd < C_HEAD_DIM // 2)[None, :], row[:, None], col[:, None])
    ang = pos * inv[None, :]
    sign = np.where((dd % (2 * nf)) < nf, -1.0, 1.0)[None, :]
    return np.cos(ang).astype(np.float32), (np.sin(ang) * sign).astype(np.float32)


def _mod_kernel(c_ref, w_ref, b_ref, o_ref):
    s = _silu(c_ref[...]).astype(BF16)
    o_ref[0] = jnp.dot(s, w_ref[0].astype(BF16), preferred_element_type=F32) + b_ref[0]


def _mod_call(cc, ada_w, ada_b):
    depth = ada_w.shape[0]
    tn = 1024
    return pl.pallas_call(
        _mod_kernel,
        grid=(depth, 3 * D_MODEL // tn),
        in_specs=[
            pl.BlockSpec((MOD_ROWS, D_MODEL), lambda l, j: (0, 0)),
            pl.BlockSpec((1, D_MODEL, tn), lambda l, j: (l, 0, j)),
            pl.BlockSpec((1, 1, tn), lambda l, j: (l, 0, j)),
        ],
        out_specs=pl.BlockSpec((1, MOD_ROWS, tn), lambda l, j: (l, 0, j)),
        out_shape=jax.ShapeDtypeStruct((depth, MOD_ROWS, 3 * D_MODEL), F32),
        compiler_params=_cparams(("arbitrary", "arbitrary")),
        name="adaln_mod",
    )(cc, ada_w, ada_b.reshape(depth, 1, 3 * D_MODEL))


def _to_bf16_kernel(x_ref, o_ref):
    o_ref[...] = x_ref[...].astype(BF16)


def _to_bf16_call(a, rows_per_step):
    rows, cols = a.shape
    return pl.pallas_call(
        _to_bf16_kernel,
        grid=(rows // rows_per_step,),
        in_specs=[pl.BlockSpec((rows_per_step, cols), lambda i: (i, 0))],
        out_specs=pl.BlockSpec((rows_per_step, cols), lambda i: (i, 0)),
        out_shape=jax.ShapeDtypeStruct((rows, cols), BF16),
        compiler_params=_cparams(("parallel",)),
        name="to_bf16",
    )(a)


def _modulated_norm(x, mod_row, g):
    shift = mod_row[:, :D_MODEL]
    scale = mod_row[:, D_MODEL:2 * D_MODEL]
    ms = jnp.mean(x * x, axis=-1, keepdims=True)
    h = x * lax.rsqrt(ms + NORM_EPS) * g
    return h * (1.0 + scale) + shift


def _in_ab_kernel(x_ref, mod_ref, ng_ref, w_ref, vg_ref, sw_ref, sb_ref, cs_ref, twc_ref, tws_ref,
                  ya_ref, sgb_ref, uv_ref, *, tl2):
    rows = RADIX * tl2
    x = x_ref[0].reshape(rows, D_MODEL)
    h = _modulated_norm(x, mod_ref[0], ng_ref[...])
    z = jnp.dot(h.astype(BF16), w_ref[...], preferred_element_type=F32)

    v = z[:, A_W:2 * A_W]
    mu = jnp.mean(v, axis=-1, keepdims=True)
    vc = v - mu
    var = jnp.mean(vc * vc, axis=-1, keepdims=True)
    vn = (vc * lax.rsqrt(var + NORM_EPS) * vg_ref[...]).astype(BF16)

    piece = min(CHUNK, tl2)
    for ci in range(rows // CHUNK):
        r0 = ci * CHUNK
        for hd in range(A_HEADS):
            c0 = hd * A_HEAD_DIM
            sv = jnp.dot(sw_ref[hd], vn[r0:r0 + CHUNK, c0:c0 + A_HEAD_DIM],
                         preferred_element_type=F32) + sb_ref[hd]
            u = z[r0:r0 + CHUNK, c0:c0 + A_HEAD_DIM]
            ga = z[r0:r0 + CHUNK, 2 * A_W + c0:2 * A_W + c0 + A_HEAD_DIM]
            ya = (u * sv * _silu(ga)).astype(BF16)
            for p0 in range(0, CHUNK, piece):
                l1, off = divmod(r0 + p0, tl2)
                ya_ref[0, l1, off:off + piece, c0:c0 + A_HEAD_DIM] = ya[p0:p0 + piece]

    gb = z[:, 3 * A_W + B_W:]
    sgb_ref[0] = _silu(gb).astype(BF16).reshape(RADIX, tl2, B_W)

    xb = z[:, 3 * A_W:3 * A_W + B_W].astype(BF16)
    cs = cs_ref[...].astype(BF16)
    for g in range(B_GROUPS):
        c0 = g * B_GROUP_DIM
        ps, qs = [], []
        for j in range(RADIX):
            pq = jnp.dot(xb[j * tl2:(j + 1) * tl2, c0:c0 + B_GROUP_DIM], cs,
                         preferred_element_type=F32)
            ps.append(pq[:, :B_GROUP_DIM])
            qs.append(pq[:, B_GROUP_DIM:])
        p02, p13 = ps[0] - ps[2], ps[1] - ps[3]
        q02, q13 = qs[0] - qs[2], qs[1] - qs[3]
        pe, po = ps[0] + ps[2], ps[1] + ps[3]
        qe, qo = qs[0] + qs[2], qs[1] + qs[3]
        us = [pe + po, p02 - q13, pe - po, p02 + q13]
        vs = [qe + qo, q02 + p13, qe - qo, q02 - p13]
        for k1 in range(RADIX):
            if k1 == 0:
                ut, vt = us[0], vs[0]
            else:
                tc, ts = twc_ref[k1], tws_ref[k1]
                ut = us[k1] * tc - vs[k1] * ts
                vt = us[k1] * ts + vs[k1] * tc
            uv_ref[0, k1, 0, :, c0:c0 + B_GROUP_DIM] = ut.astype(BF16)
            uv_ref[0, k1, 1, :, c0:c0 + B_GROUP_DIM] = vt.astype(BF16)


def _in_ab_call(xs, mod, mod_row, ng, w_in, vg, sw, sb, cs, twc, tws, tl2):
    bsz, seq_len, _ = xs.shape
    n2 = seq_len // RADIX
    x4 = xs.reshape(bsz, RADIX, n2, D_MODEL)
    row_of = (lambda b: b) if mod_row is None else (lambda b: mod_row)
    const2 = lambda b, i: (0, 0)
    const3 = lambda b, i: (0, 0, 0)
    return pl.pallas_call(
        functools.partial(_in_ab_kernel, tl2=tl2),
        grid=(bsz, n2 // tl2),
        in_specs=[
            pl.BlockSpec((1, RADIX, tl2, D_MODEL), lambda b, i: (b, 0, i, 0)),
            pl.BlockSpec((1, 1, 3 * D_MODEL), lambda b, i: (row_of(b), 0, 0)),
            pl.BlockSpec((1, D_MODEL), const2),
            pl.BlockSpec((D_MODEL, AB_IN), const2),
            pl.BlockSpec((1, A_W), const2),
            pl.BlockSpec((A_HEADS, CHUNK, CHUNK), const3),
            pl.BlockSpec((A_HEADS, CHUNK, A_HEAD_DIM), const3),
            pl.BlockSpec((B_GROUP_DIM, 2 * B_GROUP_DIM), const2),
            pl.BlockSpec((RADIX, tl2, 128), lambda b, i: (0, i, 0)),
            pl.BlockSpec((RADIX, tl2, 128), lambda b, i: (0, i, 0)),
        ],
        out_specs=[
            pl.BlockSpec((1, RADIX, tl2, A_W), lambda b, i: (b, 0, i, 0)),
            pl.BlockSpec((1, RADIX, tl2, B_W), lambda b, i: (b, 0, i, 0)),
            pl.BlockSpec((1, RADIX, 2, tl2, B_W), lambda b, i: (b, 0, 0, i, 0)),
        ],
        out_shape=[
            jax.ShapeDtypeStruct((bsz, RADIX, n2, A_W), BF16),
            jax.ShapeDtypeStruct((bsz, RADIX, n2, B_W), BF16),
            jax.ShapeDtypeStruct((bsz, RADIX, 2, n2, B_W), BF16),
        ],
        compiler_params=_cparams(("parallel", "arbitrary")),
        name="in_ab",
    )(x4, mod, ng, w_in, vg, sw, sb, cs, twc, tws)


def _dft_out_kernel(g_ref, uv_ref, ya_ref, sgb_ref, x_ref, mod_ref, w_ref, o_ref, f_ref, *, tk):
    for k1 in range(RADIX):
        f = jnp.dot(g_ref[...], uv_ref[0, k1], preferred_element_type=F32)
        for g in range(B_GROUPS):
            f_ref[g, pl.ds(k1, tk, stride=RADIX), :] = f[:, g * B_GROUP_DIM:(g + 1) * B_GROUP_DIM]
    fnat = jnp.concatenate([f_ref[g] for g in range(B_GROUPS)], axis=1)
    yb = (fnat * sgb_ref[0].astype(F32)).astype(BF16)
    y = jnp.dot(ya_ref[0], w_ref[:A_W], preferred_element_type=F32)
    y = y + jnp.dot(yb, w_ref[A_W:], preferred_element_type=F32)
    gate = mod_ref[0][:, 2 * D_MODEL:]
    o_ref[0] = x_ref[0] + gate * y


def _dft_out_call(gmat, uv, ya, sgb, xs, mod, mod_row, w_out, tk):
    bsz, seq_len, _ = xs.shape
    n2 = seq_len // RADIX
    rows = RADIX * tk
    row_of = (lambda b: b) if mod_row is None else (lambda b: mod_row)
    nat = lambda w: pl.BlockSpec((1, rows, w), lambda b, i: (b, i, 0))
    return pl.pallas_call(
        functools.partial(_dft_out_kernel, tk=tk),
        grid=(bsz, n2 // tk),
        in_specs=[
            pl.BlockSpec((tk, 2 * n2), lambda b, i: (i, 0)),
            pl.BlockSpec((1, RADIX, 2 * n2, B_W), lambda b, i: (b, 0, 0, 0)),
            nat(A_W), nat(B_W), nat(D_MODEL),
            pl.BlockSpec((1, 1, 3 * D_MODEL), lambda b, i: (row_of(b), 0, 0)),
            pl.BlockSpec((A_W + B_W, D_MODEL), lambda b, i: (0, 0)),
        ],
        out_specs=nat(D_MODEL),
        out_shape=jax.ShapeDtypeStruct((bsz, seq_len, D_MODEL), F32),
        scratch_shapes=[pltpu.VMEM((B_GROUPS, rows, B_GROUP_DIM), F32)],
        compiler_params=_cparams(("parallel", "arbitrary")),
        name="dft_out",
    )(gmat, uv.reshape(bsz, RADIX, 2 * n2, B_W), ya.reshape(bsz, seq_len, A_W),
      sgb.reshape(bsz, seq_len, B_W), xs, mod, w_out)


def _mixer_ab_layer(xs, mod, mod_row, ng, w_in, vg, sw, sb, w_out, cs, tl2):
    seq_len = xs.shape[1]
    twc, tws = _twiddle_tables(seq_len)
    gmat = _position_dft_matrix(seq_len)
    gmat = _to_bf16_call(jnp.asarray(gmat), min(256, gmat.shape[0]))
    ya, sgb, uv = _in_ab_call(xs, mod, mod_row, ng, w_in, vg, sw, sb, cs,
                              jnp.asarray(twc), jnp.asarray(tws), tl2)
    return _dft_out_call(gmat, uv, ya, sgb, xs, mod, mod_row, w_out, tk=tl2)


def _rope_block(t, cos, sin_signed, lane_lo):
    nf = C_HEAD_DIM // 4
    swapped = jnp.where(lane_lo, pltpu.roll(t, 128 - nf, axis=1), pltpu.roll(t, nf, axis=1))
    return t * cos + swapped * sin_signed


def _store_dup_heads(k_ref, c, t, lane):
    r = pltpu.roll(t, C_HEAD_DIM, axis=1)
    first = lane < C_HEAD_DIM
    k_ref[0, :, (2 * c) * 128:(2 * c + 1) * 128] = jnp.where(first, t, r).astype(BF16)
    k_ref[0, :, (2 * c + 1) * 128:(2 * c + 2) * 128] = jnp.where(first, r, t).astype(BF16)


def _in_c_kernel(x_ref, mod_ref, ng_ref, w_ref, cos_ref, sin_ref, q_ref, k_ref, v_ref, sg_ref):
    h = _modulated_norm(x_ref[0], mod_ref[0], ng_ref[...])
    z = jnp.dot(h.astype(BF16), w_ref[...], preferred_element_type=F32)
    cos = cos_ref[...]
    sin = sin_ref[...]
    lane = lax.broadcasted_iota(jnp.int32, cos.shape, 1)
    lane_lo = (lane % (C_HEAD_DIM // 2)) < (C_HEAD_DIM // 4)
    qscale = C_HEAD_DIM ** -0.5 * LOG2E
    for c in range(C_Q_W // 128):
        t = _rope_block(z[:, c * 128:(c + 1) * 128], cos, sin, lane_lo)
        q_ref[0, :, c * 128:(c + 1) * 128] = (t * qscale).astype(BF16)
    for c in range(C_KV_W // 128):
        t = _rope_block(z[:, C_Q_W + c * 128:C_Q_W + (c + 1) * 128], cos, sin, lane_lo)
        _store_dup_heads(k_ref, c, t, lane)
    v_ref[0] = z[:, C_Q_W + C_KV_W:C_Q_W + 2 * C_KV_W].astype(BF16)
    sg_ref[0] = _silu(z[:, C_Q_W + 2 * C_KV_W:]).astype(BF16)


def _in_c_call(xs, mod, ng, w_in, cos_t, sin_t, tm):
    bsz, seq_len, _ = xs.shape
    const2 = lambda b, i: (0, 0)
    row_blk = lambda w: pl.BlockSpec((1, tm, w), lambda b, i: (b, i, 0))
    return pl.pallas_call(
        _in_c_kernel,
        grid=(bsz, seq_len // tm),
        in_specs=[
            row_blk(D_MODEL),
            pl.BlockSpec((1, 1, 3 * D_MODEL), lambda b, i: (b, 0, 0)),
            pl.BlockSpec((1, D_MODEL), const2),
            pl.BlockSpec((D_MODEL, C_IN), const2),
            pl.BlockSpec((tm, 128), lambda b, i: (i, 0)),
            pl.BlockSpec((tm, 128), lambda b, i: (i, 0)),
        ],
        out_specs=[row_blk(C_Q_W), row_blk(2 * C_KV_W), row_blk(C_KV_W), row_blk(C_Q_W)],
        out_shape=[
            jax.ShapeDtypeStruct((bsz, seq_len, C_Q_W), BF16),
            jax.ShapeDtypeStruct((bsz, seq_len, 2 * C_KV_W), BF16),
            jax.ShapeDtypeStruct((bsz, seq_len, C_KV_W), BF16),
            jax.ShapeDtypeStruct((bsz, seq_len, C_Q_W), BF16),
        ],
        compiler_params=_cparams(("parallel", "arbitrary")),
        name="in_c",
    )(xs, mod, ng, w_in, cos_t, sin_t)


def _ctx_kv_kernel(x_ref, mod_ref, ng_ref, w_ref, k_ref, v_ref):
    h = _modulated_norm(x_ref[0], mod_ref[0], ng_ref[...])
    z = jnp.dot(h.astype(BF16), w_ref[...], preferred_element_type=F32)
    lane = lax.broadcasted_iota(jnp.int32, (z.shape[0], 128), 1)
    for c in range(C_KV_W // 128):
        _store_dup_heads(k_ref, c, z[:, c * 128:(c + 1) * 128], lane)
    v_ref[0] = z[:, C_KV_W:].astype(BF16)


def _ctx_kv_call(ctx, mod, mod_row, ng, w_kv):
    bsz, lc, _ = ctx.shape
    return pl.pallas_call(
        _ctx_kv_kernel,
        grid=(bsz,),
        in_specs=[
            pl.BlockSpec((1, lc, D_MODEL), lambda b: (b, 0, 0)),
            pl.BlockSpec((1, 1, 3 * D_MODEL), lambda b: (mod_row, 0, 0)),
            pl.BlockSpec((1, D_MODEL), lambda b: (0, 0)),
            pl.BlockSpec((D_MODEL, 2 * C_KV_W), lambda b: (0, 0)),
        ],
        out_specs=[pl.BlockSpec((1, lc, 2 * C_KV_W), lambda b: (b, 0, 0)),
                   pl.BlockSpec((1, lc, C_KV_W), lambda b: (b, 0, 0))],
        out_shape=[jax.ShapeDtypeStruct((bsz, lc, 2 * C_KV_W), BF16),
                   jax.ShapeDtypeStruct((bsz, lc, C_KV_W), BF16)],
        compiler_params=_cparams(("parallel",)),
        name="ctx_kv",
    )(ctx, mod, ng, w_kv)


def _attn_out_kernel(sink_ref, q_ref, kp_ref, kc_ref, kn_ref, vp_ref, vc_ref, vn_ref, kx_ref, vx_ref,
                     sg_ref, x_ref, mod_ref, w_ref, fg_ref, o_ref):
    n = pl.program_id(1)
    last = pl.num_programs(1) - 1
    gq = C_GROUP * Q_BLOCK
    kj = lax.broadcasted_iota(jnp.int32, (Q_BLOCK, Q_BLOCK), 0)
    qi = lax.broadcasted_iota(jnp.int32, (Q_BLOCK, Q_BLOCK), 1)
    bias_prev = jnp.where((kj >= qi) & (n > 0), 0.0, NEG_INF).astype(F32)
    bias_next = jnp.where((kj <= qi) & (n < last), 0.0, NEG_INF).astype(F32)
    bias_prev = jnp.concatenate([bias_prev] * C_GROUP, axis=1)
    bias_next = jnp.concatenate([bias_next] * C_GROUP, axis=1)
    lane = lax.broadcasted_iota(jnp.int32, (Q_BLOCK, 128), 1)
    first = lane < C_HEAD_DIM
    zero = jnp.zeros((Q_BLOCK, 128), BF16)

    o_cols = []
    for kh in range(C_KV_HEADS):
        kl = slice(kh * 128, (kh + 1) * 128)
        k2 = jnp.concatenate([kx_ref[0, :, kl], kp_ref[0, :, kl], kc_ref[0, :, kl], kn_ref[0, :, kl]], axis=0)
        vl = slice((kh // 2) * 128, (kh // 2 + 1) * 128)
        vpair = jnp.concatenate([vx_ref[0, :, vl], vp_ref[0, :, vl], vc_ref[0, :, vl], vn_ref[0, :, vl]], axis=0)
        q4 = []
        for c in range(2):
            qv = q_ref[0, :, kh * 256 + c * 128:kh * 256 + (c + 1) * 128]
            q4 += [jnp.where(first, qv, zero), jnp.where(first, zero, qv)]
        q4 = jnp.concatenate(q4, axis=0)
        st = lax.dot_general(k2, q4, (((1,), (1,)), ((), ())), preferred_element_type=F32)
        lc = kx_ref.shape[1]
        blocks = [st[:lc], st[lc:lc + Q_BLOCK] + bias_prev, st[lc + Q_BLOCK:lc + 2 * Q_BLOCK],
                  st[lc + 2 * Q_BLOCK:] + bias_next]
        sink2 = sink_ref[kh] * LOG2E
        m = sink2
        for blk in blocks:
            m = jnp.maximum(m, jnp.max(blk, axis=0, keepdims=True))
        ps = [jnp.exp2(blk - m) for blk in blocks]
        denom = jnp.exp2(sink2 - m)
        for p in ps:
            denom = denom + jnp.sum(p, axis=0, keepdims=True)
        pt = jnp.concatenate([p.astype(BF16) for p in ps], axis=0)
        ot = lax.dot_general(vpair, pt, (((0,), (0,)), ((), ())), preferred_element_type=F32)
        r0 = (kh % 2) * C_HEAD_DIM
        ot = ot[r0:r0 + C_HEAD_DIM] / denom
        for c in range(2):
            pair = jnp.concatenate([ot[:, (2 * c) * Q_BLOCK:(2 * c + 1) * Q_BLOCK],
                                    ot[:, (2 * c + 1) * Q_BLOCK:(2 * c + 2) * Q_BLOCK]], axis=0)
            o_cols.append(pair.T)
    o_all = jnp.concatenate(o_cols, axis=1)
    y = jnp.dot((o_all * sg_ref[0].astype(F32)).astype(BF16), w_ref[...], preferred_element_type=F32)
    gate = mod_ref[0][:, 2 * D_MODEL:]
    x2 = x_ref[0] + gate * y
    ms = jnp.mean(x2 * x2, axis=-1, keepdims=True)
    o_ref[0] = x2 * lax.rsqrt(ms + NORM_EPS) * fg_ref[...]


def _attn_out_call(sink, q, k, v, kx, vx, sg, xs, mod, w_out, fg):
    bsz, seq_len, _ = xs.shape
    nblk = seq_len // Q_BLOCK
    lc = kx.shape[1]
    cur = lambda w: pl.BlockSpec((1, Q_BLOCK, w), lambda b, n: (b, n, 0))
    prev = lambda w: pl.BlockSpec((1, Q_BLOCK, w), lambda b, n: (b, jnp.maximum(n - 1, 0), 0))
    nxt = lambda w: pl.BlockSpec((1, Q_BLOCK, w), lambda b, n: (b, jnp.minimum(n + 1, nblk - 1), 0))
    ctx_blk = lambda w: pl.BlockSpec((1, lc, w), lambda b, n: (b, 0, 0))
    kw, vw = 2 * C_KV_W, C_KV_W
    sink_t = jnp.repeat(sink.reshape(C_KV_HEADS, 1, C_GROUP), Q_BLOCK, axis=2)
    return pl.pallas_call(
        _attn_out_kernel,
        grid=(bsz, nblk),
        in_specs=[
            pl.BlockSpec((C_KV_HEADS, 1, C_GROUP * Q_BLOCK), lambda b, n: (0, 0, 0)),
            cur(C_Q_W), prev(kw), cur(kw), nxt(kw), prev(vw), cur(vw), nxt(vw), ctx_blk(kw), ctx_blk(vw),
            cur(C_Q_W), cur(D_MODEL),
            pl.BlockSpec((1, 1, 3 * D_MODEL), lambda b, n: (b, 0, 0)),
            pl.BlockSpec((C_Q_W, D_MODEL), lambda b, n: (0, 0)),
            pl.BlockSpec((1, D_MODEL), lambda b, n: (0, 0)),
        ],
        out_specs=cur(D_MODEL),
        out_shape=jax.ShapeDtypeStruct((bsz, seq_len, D_MODEL), F32),
        compiler_params=_cparams(("parallel", "arbitrary")),
        name="attn_out",
    )(sink_t, q, k, k, k, v, v, v, kx, vx, sg, xs, mod, w_out, fg)


def kernel(x, c, ctx, c_ctx, norm_g, ada_w, ada_b, w_in_ab, v_norm_g, spatial_w, spatial_b, w_out_ab,
           w_in_c, sink_logit, w_out_c, final_g):
    bsz, seq_len, _ = x.shape
    depth = ada_w.shape[0]
    assert depth == 2 and bsz + 1 <= MOD_ROWS
    ctx_row = bsz

    cc = jnp.concatenate([c, c_ctx[None, :], jnp.zeros((MOD_ROWS - bsz - 1, D_MODEL), F32)], axis=0)
    mod = _mod_call(cc, ada_w, ada_b)
    mod0 = mod[0].reshape(MOD_ROWS, 1, 3 * D_MODEL)
    mod1 = mod[1].reshape(MOD_ROWS, 1, 3 * D_MODEL)

    cs = jnp.asarray(_channel_dft_matrix())
    ng0 = norm_g[0].reshape(1, D_MODEL)
    ng1 = norm_g[1].reshape(1, D_MODEL)
    w_in0 = w_in_ab[0].astype(BF16)
    w_out0 = w_out_ab[0].astype(BF16)
    vg = v_norm_g[0].reshape(1, A_W)
    sw = spatial_w[0].astype(BF16)
    sb = jnp.broadcast_to(spatial_b[0][:, :, None], (A_HEADS, CHUNK, A_HEAD_DIM))

    x1 = _mixer_ab_layer(x, mod0, None, ng0, w_in0, vg, sw, sb, w_out0, cs, tl2=128)
    ctx1 = _mixer_ab_layer(ctx, mod0, ctx_row, ng0, w_in0, vg, sw, sb, w_out0, cs,
                           tl2=ctx.shape[1] // RADIX)

    w_in1 = w_in_c[0].astype(BF16)
    w_out1 = w_out_c[0].astype(BF16)
    cos_t, sin_t = _rope_tables(seq_len)
    q, k, v, sg = _in_c_call(x1, mod1, ng1, w_in1, jnp.asarray(cos_t), jnp.asarray(sin_t), tm=512)
    kx, vx = _ctx_kv_call(ctx1, mod1, ctx_row, ng1, w_in1[:, C_Q_W:C_Q_W + 2 * C_KV_W])
    return _attn_out_call(sink_logit[0], q, k, v, kx, vx, sg, x1, mod1, w_out1,
                          final_g.reshape(1, D_MODEL))
```

```python
import functools
import math

import numpy as np
import jax
import jax.numpy as jnp
from jax import lax
from jax.experimental import pallas as pl
from jax.experimental.pallas import tpu as pltpu

F32 = jnp.float32
BF16 = jnp.bfloat16

D_MODEL = 1024
GRID_W = 64
CHUNK = 128
A_HEADS = 4
A_HEAD_DIM = 128
A_W = A_HEADS * A_HEAD_DIM
B_GROUPS = 4
B_GROUP_DIM = 128
B_W = B_GROUPS * B_GROUP_DIM
AB_IN = 3 * A_W + 2 * B_W
C_HEADS = 16
C_KV_HEADS = 4
C_GROUP = C_HEADS // C_KV_HEADS
C_HEAD_DIM = 64
C_Q_W = C_HEADS * C_HEAD_DIM
C_KV_W = C_KV_HEADS * C_HEAD_DIM
C_IN = 2 * C_Q_W + 2 * C_KV_W
WINDOW = 128
Q_BLOCK = 128
ROPE_BASE = 10000.0
NORM_EPS = 1e-6
NEG_INF = -1e30
LOG2E = math.log2(math.e)
RADIX = 4
MOD_ROWS = 16
SOFTMAX_ROWS = 64
V7X_VMEM_LIMIT = 56 * 1024 * 1024


def _silu(x):
    return x * (1.0 / (1.0 + jnp.exp(-x)))


def _cparams(sem):
    return pltpu.CompilerParams(dimension_semantics=sem, vmem_limit_bytes=V7X_VMEM_LIMIT)


def _channel_dft_matrix():
    n = np.arange(B_GROUP_DIM)
    ang = 2.0 * np.pi * np.outer(n, n) / B_GROUP_DIM
    return np.concatenate([np.cos(ang), np.sin(ang)], axis=1).astype(np.float32)


def _position_dft_matrix(seq_len):
    n2 = seq_len // RADIX
    idx = np.arange(n2)
    ang = 2.0 * np.pi * (np.outer(idx, idx) % n2) / n2
    norm = 1.0 / math.sqrt(seq_len * B_GROUP_DIM)
    return np.concatenate([np.cos(ang) * norm, -np.sin(ang) * norm], axis=1).astype(np.float32)


def _twiddle_tables(seq_len):
    n2 = seq_len // RADIX
    l2 = np.arange(n2)[None, :, None]
    k1 = np.arange(RADIX)[:, None, None]
    ang = 2.0 * np.pi * ((l2 * k1) % seq_len) / seq_len
    ang = np.broadcast_to(ang, (RADIX, n2, 128))
    return np.cos(ang).astype(np.float32), np.sin(ang).astype(np.float32)


def _rope_tables(seq_len):
    t = np.arange(seq_len)
    row = (t // GRID_W).astype(np.float64)
    col = (t % GRID_W).astype(np.float64)
    lane = np.arange(128)
    dd = lane % C_HEAD_DIM
    nf = C_HEAD_DIM // 4
    inv = ROPE_BASE ** (-(dd % nf).astype(np.float64) / nf)
    pos = np.where((dd < C_HEAD_DIM // 2)[None, :], row[:, None], col[:, None])
    ang = pos * inv[None, :]
    sign = np.where((dd % (2 * nf)) < nf, -1.0, 1.0)[None, :]
    return np.cos(ang).astype(np.float32), (np.sin(ang) * sign).astype(np.float32)


def _mod_kernel(c_ref, w_ref, b_ref, o_ref):
    s = _silu(c_ref[...]).astype(BF16)
    o_ref[0] = jnp.dot(s, w_ref[0].astype(BF16), preferred_element_type=F32) + b_ref[0]


def _mod_call(cc, ada_w, ada_b):
    depth = ada_w.shape[0]
    tn = 1024
    return pl.pallas_call(
        _mod_kernel,
        grid=(depth, 3 * D_MODEL // tn),
        in_specs=[
            pl.BlockSpec((MOD_ROWS, D_MODEL), lambda l, j: (0, 0)),
            pl.BlockSpec((1, D_MODEL, tn), lambda l, j: (l, 0, j)),
            pl.BlockSpec((1, 1, tn), lambda l, j: (l, 0, j)),
        ],
        out_specs=pl.BlockSpec((1, MOD_ROWS, tn), lambda l, j: (l, 0, j)),
        out_shape=jax.ShapeDtypeStruct((depth, MOD_ROWS, 3 * D_MODEL), F32),
        compiler_params=_cparams(("arbitrary", "arbitrary")),
        name="adaln_mod",
    )(cc, ada_w, ada_b.reshape(depth, 1, 3 * D_MODEL))


def _to_bf16_kernel(x_ref, o_ref):
    o_ref[...] = x_ref[...].astype(BF16)


def _to_bf16_call(a, rows_per_step):
    rows, cols = a.shape
    return pl.pallas_call(
        _to_bf16_kernel,
        grid=(rows // rows_per_step,),
        in_specs=[pl.BlockSpec((rows_per_step, cols), lambda i: (i, 0))],
        out_specs=pl.BlockSpec((rows_per_step, cols), lambda i: (i, 0)),
        out_shape=jax.ShapeDtypeStruct((rows, cols), BF16),
        compiler_params=_cparams(("parallel",)),
        name="to_bf16",
    )(a)


def _modulated_norm(x, mod_row, g):
    shift = mod_row[:, :D_MODEL]
    scale = mod_row[:, D_MODEL:2 * D_MODEL]
    ms = jnp.mean(x * x, axis=-1, keepdims=True)
    h = x * lax.rsqrt(ms + NORM_EPS) * g
    return h * (1.0 + scale) + shift


def _in_ab_kernel(x_ref, mod_ref, ng_ref, w_ref, vg_ref, sw_ref, sb_ref, cs_ref, twc_ref, tws_ref,
                  ya_ref, sgb_ref, uv_ref, *, tl2):
    rows = RADIX * tl2
    x = x_ref[0].reshape(rows, D_MODEL)
    h = _modulated_norm(x, mod_ref[0], ng_ref[...])
    z = jnp.dot(h.astype(BF16), w_ref[...], preferred_element_type=F32)

    v = z[:, A_W:2 * A_W]
    mu = jnp.mean(v, axis=-1, keepdims=True)
    vc = v - mu
    var = jnp.mean(vc * vc, axis=-1, keepdims=True)
    vn = (vc * lax.rsqrt(var + NORM_EPS) * vg_ref[...]).astype(BF16)

    piece = min(CHUNK, tl2)
    for ci in range(rows // CHUNK):
        r0 = ci * CHUNK
        for hd in range(A_HEADS):
            c0 = hd * A_HEAD_DIM
            sv = jnp.dot(sw_ref[hd], vn[r0:r0 + CHUNK, c0:c0 + A_HEAD_DIM],
                         preferred_element_type=F32) + sb_ref[hd]
            u = z[r0:r0 + CHUNK, c0:c0 + A_HEAD_DIM]
            ga = z[r0:r0 + CHUNK, 2 * A_W + c0:2 * A_W + c0 + A_HEAD_DIM]
            ya = (u * sv * _silu(ga)).astype(BF16)
            for p0 in range(0, CHUNK, piece):
                l1, off = divmod(r0 + p0, tl2)
                ya_ref[0, l1, off:off + piece, c0:c0 + A_HEAD_DIM] = ya[p0:p0 + piece]

    gb = z[:, 3 * A_W + B_W:]
    sgb_ref[0] = _silu(gb).astype(BF16).reshape(RADIX, tl2, B_W)

    xb = z[:, 3 * A_W:3 * A_W + B_W].astype(BF16)
    cs = cs_ref[...].astype(BF16)
    for g in range(B_GROUPS):
        c0 = g * B_GROUP_DIM
        ps, qs = [], []
        for j in range(RADIX):
            pq = jnp.dot(xb[j * tl2:(j + 1) * tl2, c0:c0 + B_GROUP_DIM], cs,
                         preferred_element_type=F32)
            ps.append(pq[:, :B_GROUP_DIM])
            qs.append(pq[:, B_GROUP_DIM:])
        p02, p13 = ps[0] - ps[2], ps[1] - ps[3]
        q02, q13 = qs[0] - qs[2], qs[1] - qs[3]
        pe, po = ps[0] + ps[2], ps[1] + ps[3]
        qe, qo = qs[0] + qs[2], qs[1] + qs[3]
        us = [pe + po, p02 - q13, pe - po, p02 + q13]
        vs = [qe + qo, q02 + p13, qe - qo, q02 - p13]
        for k1 in range(RADIX):
            if k1 == 0:
                ut, vt = us[0], vs[0]
            else:
                tc, ts = twc_ref[k1], tws_ref[k1]
                ut = us[k1] * tc - vs[k1] * ts
                vt = us[k1] * ts + vs[k1] * tc
            uv_ref[0, k1, 0, :, c0:c0 + B_GROUP_DIM] = ut.astype(BF16)
            uv_ref[0, k1, 1, :, c0:c0 + B_GROUP_DIM] = vt.astype(BF16)


def _in_ab_call(xs, mod, mod_row, ng, w_in, vg, sw, sb, cs, twc, tws, tl2):
    bsz, seq_len, _ = xs.shape
    n2 = seq_len // RADIX
    x4 = xs.reshape(bsz, RADIX, n2, D_MODEL)
    row_of = (lambda b: b) if mod_row is None else (lambda b: mod_row)
    const2 = lambda b, i: (0, 0)
    const3 = lambda b, i: (0, 0, 0)
    return pl.pallas_call(
        functools.partial(_in_ab_kernel, tl2=tl2),
        grid=(bsz, n2 // tl2),
        in_specs=[
            pl.BlockSpec((1, RADIX, tl2, D_MODEL), lambda b, i: (b, 0, i, 0)),
            pl.BlockSpec((1, 1, 3 * D_MODEL), lambda b, i: (row_of(b), 0, 0)),
            pl.BlockSpec((1, D_MODEL), const2),
            pl.BlockSpec((D_MODEL, AB_IN), const2),
            pl.BlockSpec((1, A_W), const2),
            pl.BlockSpec((A_HEADS, CHUNK, CHUNK), const3),
            pl.BlockSpec((A_HEADS, CHUNK, A_HEAD_DIM), const3),
            pl.BlockSpec((B_GROUP_DIM, 2 * B_GROUP_DIM), const2),
            pl.BlockSpec((RADIX, tl2, 128), lambda b, i: (0, i, 0)),
            pl.BlockSpec((RADIX, tl2, 128), lambda b, i: (0, i, 0)),
        ],
        out_specs=[
            pl.BlockSpec((1, RADIX, tl2, A_W), lambda b, i: (b, 0, i, 0)),
            pl.BlockSpec((1, RADIX, tl2, B_W), lambda b, i: (b, 0, i, 0)),
            pl.BlockSpec((1, RADIX, 2, tl2, B_W), lambda b, i: (b, 0, 0, i, 0)),
        ],
        out_shape=[
            jax.ShapeDtypeStruct((bsz, RADIX, n2, A_W), BF16),
            jax.ShapeDtypeStruct((bsz, RADIX, n2, B_W), BF16),
            jax.ShapeDtypeStruct((bsz, RADIX, 2, n2, B_W), BF16),
        ],
        compiler_params=_cparams(("parallel", "arbitrary")),
        name="in_ab",
    )(x4, mod, ng, w_in, vg, sw, sb, cs, twc, tws)


def _dft_out_kernel(g_ref, uv_ref, ya_ref, sgb_ref, x_ref, mod_ref, w_ref, o_ref, f_ref, *, rows):
    i = pl.program_id(1)
    n2 = g_ref.shape[0]

    @pl.when(i == 0)
    def _():
        for k1 in range(RADIX):
            f = jnp.dot(g_ref[...], uv_ref[0, k1], preferred_element_type=F32)
            for g in range(B_GROUPS):
                f_ref[g, pl.ds(k1, n2, stride=RADIX), :] = f[:, g * B_GROUP_DIM:(g + 1) * B_GROUP_DIM]

    r0 = pl.multiple_of(i * rows, rows)
    fnat = jnp.concatenate([f_ref[g, pl.ds(r0, rows), :] for g in range(B_GROUPS)], axis=1)
    yb = (fnat * sgb_ref[0].astype(F32)).astype(BF16)
    y = jnp.dot(ya_ref[0], w_ref[:A_W], preferred_element_type=F32)
    y = y + jnp.dot(yb, w_ref[A_W:], preferred_element_type=F32)
    gate = mod_ref[0][:, 2 * D_MODEL:]
    o_ref[0] = x_ref[0] + gate * y


def _dft_out_call(gmat, uv, ya, sgb, xs, mod, mod_row, w_out, tk):
    bsz, seq_len, _ = xs.shape
    n2 = seq_len // RADIX
    rows = RADIX * tk
    row_of = (lambda b: b) if mod_row is None else (lambda b: mod_row)
    nat = lambda w: pl.BlockSpec((1, rows, w), lambda b, i: (b, i, 0))
    return pl.pallas_call(
        functools.partial(_dft_out_kernel, rows=rows),
        grid=(bsz, n2 // tk),
        in_specs=[
            pl.BlockSpec((n2, 2 * n2), lambda b, i: (0, 0), pipeline_mode=pl.Buffered(1)),
            pl.BlockSpec((1, RADIX, 2 * n2, B_W), lambda b, i: (b, 0, 0, 0)),
            nat(A_W), nat(B_W), nat(D_MODEL),
            pl.BlockSpec((1, 1, 3 * D_MODEL), lambda b, i: (row_of(b), 0, 0)),
            pl.BlockSpec((A_W + B_W, D_MODEL), lambda b, i: (0, 0), pipeline_mode=pl.Buffered(1)),
        ],
        out_specs=nat(D_MODEL),
        out_shape=jax.ShapeDtypeStruct((bsz, seq_len, D_MODEL), F32),
        scratch_shapes=[pltpu.VMEM((B_GROUPS, seq_len, B_GROUP_DIM), F32)],
        compiler_params=_cparams(("parallel", "arbitrary")),
        name="dft_out",
    )(gmat, uv.reshape(bsz, RADIX, 2 * n2, B_W), ya.reshape(bsz, seq_len, A_W),
      sgb.reshape(bsz, seq_len, B_W), xs, mod, w_out)


def _mixer_ab_layer(xs, mod, mod_row, ng, w_in, vg, sw, sb, w_out, cs, tl2):
    seq_len = xs.shape[1]
    twc, tws = _twiddle_tables(seq_len)
    gmat = _position_dft_matrix(seq_len)
    gmat = _to_bf16_call(jnp.asarray(gmat), min(256, gmat.shape[0]))
    ya, sgb, uv = _in_ab_call(xs, mod, mod_row, ng, w_in, vg, sw, sb, cs,
                              jnp.asarray(twc), jnp.asarray(tws), tl2)
    return _dft_out_call(gmat, uv, ya, sgb, xs, mod, mod_row, w_out, tk=tl2)


def _rope_block(t, cos, sin_signed, lane_lo):
    nf = C_HEAD_DIM // 4
    swapped = jnp.where(lane_lo, pltpu.roll(t, 128 - nf, axis=1), pltpu.roll(t, nf, axis=1))
    return t * cos + swapped * sin_signed


def _store_dup_heads(k_ref, c, t, lane):
    r = pltpu.roll(t, C_HEAD_DIM, axis=1)
    first = lane < C_HEAD_DIM
    k_ref[0, :, (2 * c) * 128:(2 * c + 1) * 128] = jnp.where(first, t, r).astype(BF16)
    k_ref[0, :, (2 * c + 1) * 128:(2 * c + 2) * 128] = jnp.where(first, r, t).astype(BF16)


def _in_c_kernel(x_ref, mod_ref, ng_ref, w_ref, cos_ref, sin_ref, q_ref, k_ref, v_ref, sg_ref):
    h = _modulated_norm(x_ref[0], mod_ref[0], ng_ref[...])
    z = jnp.dot(h.astype(BF16), w_ref[...], preferred_element_type=F32)
    cos = cos_ref[...]
    sin = sin_ref[...]
    lane = lax.broadcasted_iota(jnp.int32, cos.shape, 1)
    lane_lo = (lane % (C_HEAD_DIM // 2)) < (C_HEAD_DIM // 4)
    qscale = C_HEAD_DIM ** -0.5 * LOG2E
    for c in range(C_Q_W // 128):
        t = _rope_block(z[:, c * 128:(c + 1) * 128], cos, sin, lane_lo)
        q_ref[0, :, c * 128:(c + 1) * 128] = (t * qscale).astype(BF16)
    for c in range(C_KV_W // 128):
        t = _rope_block(z[:, C_Q_W + c * 128:C_Q_W + (c + 1) * 128], cos, sin, lane_lo)
        _store_dup_heads(k_ref, c, t, lane)
    v_ref[0] = z[:, C_Q_W + C_KV_W:C_Q_W + 2 * C_KV_W].astype(BF16)
    sg_ref[0] = _silu(z[:, C_Q_W + 2 * C_KV_W:]).astype(BF16)


def _in_c_call(xs, mod, ng, w_in, cos_t, sin_t, tm):
    bsz, seq_len, _ = xs.shape
    const2 = lambda b, i: (0, 0)
    row_blk = lambda w: pl.BlockSpec((1, tm, w), lambda b, i: (b, i, 0))
    return pl.pallas_call(
        _in_c_kernel,
        grid=(bsz, seq_len // tm),
        in_specs=[
            row_blk(D_MODEL),
            pl.BlockSpec((1, 1, 3 * D_MODEL), lambda b, i: (b, 0, 0)),
            pl.BlockSpec((1, D_MODEL), const2),
            pl.BlockSpec((D_MODEL, C_IN), const2),
            pl.BlockSpec((tm, 128), lambda b, i: (i, 0)),
            pl.BlockSpec((tm, 128), lambda b, i: (i, 0)),
        ],
        out_specs=[row_blk(C_Q_W), row_blk(2 * C_KV_W), row_blk(C_KV_W), row_blk(C_Q_W)],
        out_shape=[
            jax.ShapeDtypeStruct((bsz, seq_len, C_Q_W), BF16),
            jax.ShapeDtypeStruct((bsz, seq_len, 2 * C_KV_W), BF16),
            jax.ShapeDtypeStruct((bsz, seq_len, C_KV_W), BF16),
            jax.ShapeDtypeStruct((bsz, seq_len, C_Q_W), BF16),
        ],
        compiler_params=_cparams(("parallel", "arbitrary")),
        name="in_c",
    )(xs, mod, ng, w_in, cos_t, sin_t)


def _ctx_kv_kernel(x_ref, mod_ref, ng_ref, w_ref, k_ref, v_ref):
    h = _modulated_norm(x_ref[0], mod_ref[0], ng_ref[...])
    z = jnp.dot(h.astype(BF16), w_ref[...], preferred_element_type=F32)
    lane = lax.broadcasted_iota(jnp.int32, (z.shape[0], 128), 1)
    for c in range(C_KV_W // 128):
        _store_dup_heads(k_ref, c, z[:, c * 128:(c + 1) * 128], lane)
    v_ref[0] = z[:, C_KV_W:].astype(BF16)


def _ctx_kv_call(ctx, mod, mod_row, ng, w_kv):
    bsz, lc, _ = ctx.shape
    return pl.pallas_call(
        _ctx_kv_kernel,
        grid=(bsz,),
        in_specs=[
            pl.BlockSpec((1, lc, D_MODEL), lambda b: (b, 0, 0)),
            pl.BlockSpec((1, 1, 3 * D_MODEL), lambda b: (mod_row, 0, 0)),
            pl.BlockSpec((1, D_MODEL), lambda b: (0, 0)),
            pl.BlockSpec((D_MODEL, 2 * C_KV_W), lambda b: (0, 0)),
        ],
        out_specs=[pl.BlockSpec((1, lc, 2 * C_KV_W), lambda b: (b, 0, 0)),
                   pl.BlockSpec((1, lc, C_KV_W), lambda b: (b, 0, 0))],
        out_shape=[jax.ShapeDtypeStruct((bsz, lc, 2 * C_KV_W), BF16),
                   jax.ShapeDtypeStruct((bsz, lc, C_KV_W), BF16)],
        compiler_params=_cparams(("parallel",)),
        name="ctx_kv",
    )(ctx, mod, ng, w_kv)


def _attn_out_kernel(sink_ref, q_ref, kp_ref, kc_ref, kn_ref, vp_ref, vc_ref, vn_ref, kx_ref, vx_ref,
                     sg_ref, x_ref, mod_ref, w_ref, fg_ref, o_ref):
    n = pl.program_id(1)
    last = pl.num_programs(1) - 1
    gq = C_GROUP * Q_BLOCK
    kj = lax.broadcasted_iota(jnp.int32, (Q_BLOCK, Q_BLOCK), 0)
    qi = lax.broadcasted_iota(jnp.int32, (Q_BLOCK, Q_BLOCK), 1)
    tile = lambda bias: jnp.concatenate([bias.astype(F32)] * C_GROUP, axis=1)
    before = kj >= qi
    after = kj <= qi
    bias_before = tile(jnp.where(before, 0.0, NEG_INF))
    bias_after = tile(jnp.where(after, 0.0, NEG_INF))
    bias_first = tile(jnp.where(before & (n > 0), 0.0, NEG_INF))
    bias_last = tile(jnp.where(after & (n < last), 0.0, NEG_INF))
    lane = lax.broadcasted_iota(jnp.int32, (Q_BLOCK, 128), 1)
    first = lane < C_HEAD_DIM
    zero = jnp.zeros((Q_BLOCK, 128), BF16)
    hi = slice(Q_BLOCK, 2 * Q_BLOCK)
    lo = slice(0, Q_BLOCK)

    def scores(qb, kh):
        kl = slice(kh * 128, (kh + 1) * 128)
        if qb == 0:
            kblocks = [kx_ref[0, :, kl], kp_ref[0, :, kl], kc_ref[0, lo, kl], kc_ref[0, hi, kl]]
            biases = (None, bias_first, None, bias_after)
        else:
            kblocks = [kx_ref[0, :, kl], kc_ref[0, lo, kl], kc_ref[0, hi, kl], kn_ref[0, :, kl]]
            biases = (None, bias_before, None, bias_last)
        rows_q = slice(qb * Q_BLOCK, (qb + 1) * Q_BLOCK)
        q4 = []
        for c in range(2):
            qv = q_ref[0, rows_q, kh * 256 + c * 128:kh * 256 + (c + 1) * 128]
            q4 += [jnp.where(first, qv, zero), jnp.where(first, zero, qv)]
        q4 = jnp.concatenate(q4, axis=0)
        blocks, m8 = [], None
        for kb, bias in zip(kblocks, biases):
            sb = lax.dot_general(kb, q4, (((1,), (1,)), ((), ())), preferred_element_type=F32)
            if bias is not None:
                sb = sb + bias
            blocks.append(sb)
            mb = jnp.max(sb.reshape(sb.shape[0] // 8, 8, gq), axis=0)
            m8 = mb if m8 is None else jnp.maximum(m8, mb)
        return blocks, m8

    def softmax_pv(qb, kh, blocks, m8):
        sink2 = sink_ref[kh] * LOG2E
        m = jnp.maximum(jnp.max(m8, axis=0, keepdims=True), sink2)
        acc8 = jnp.zeros((8, gq), F32)
        pts = []
        for blk in blocks:
            for r in range(0, blk.shape[0], SOFTMAX_ROWS):
                p = jnp.exp2(blk[r:r + SOFTMAX_ROWS] - m)
                acc8 = acc8 + jnp.sum(p.reshape(SOFTMAX_ROWS // 8, 8, gq), axis=0)
                pts.append(p.astype(BF16))
        pt = jnp.concatenate(pts, axis=0)
        denom = jnp.sum(acc8, axis=0, keepdims=True) + jnp.exp2(sink2 - m)
        vl = slice((kh // 2) * 128, (kh // 2 + 1) * 128)
        if qb == 0:
            vblocks = [vx_ref[0, :, vl], vp_ref[0, :, vl], vc_ref[0, :, vl]]
        else:
            vblocks = [vx_ref[0, :, vl], vc_ref[0, :, vl], vn_ref[0, :, vl]]
        vpair = jnp.concatenate(vblocks, axis=0)
        ot = lax.dot_general(vpair, pt, (((0,), (0,)), ((), ())), preferred_element_type=F32)
        r0 = (kh % 2) * C_HEAD_DIM
        ot = ot[r0:r0 + C_HEAD_DIM] / denom
        cols = []
        for c in range(2):
            pair = jnp.concatenate([ot[:, (2 * c) * Q_BLOCK:(2 * c + 1) * Q_BLOCK],
                                    ot[:, (2 * c + 1) * Q_BLOCK:(2 * c + 2) * Q_BLOCK]], axis=0)
            cols.append(pair.T)
        return cols

    chains = [(qb, kh) for qb in range(2) for kh in range(C_KV_HEADS)]
    o_cols = {0: [], 1: []}
    pending = scores(*chains[0])
    for idx, (qb, kh) in enumerate(chains):
        nxt = scores(*chains[idx + 1]) if idx + 1 < len(chains) else None
        o_cols[qb] += softmax_pv(qb, kh, *pending)
        pending = nxt
    o_all = jnp.concatenate([jnp.concatenate(o_cols[0], axis=1),
                             jnp.concatenate(o_cols[1], axis=1)], axis=0)
    y = jnp.dot((o_all * sg_ref[0].astype(F32)).astype(BF16), w_ref[...], preferred_element_type=F32)
    gate = mod_ref[0][:, 2 * D_MODEL:]
    x2 = x_ref[0] + gate * y
    ms = jnp.mean(x2 * x2, axis=-1, keepdims=True)
    o_ref[0] = x2 * lax.rsqrt(ms + NORM_EPS) * fg_ref[...]


def _attn_out_call(sink, q, k, v, kx, vx, sg, xs, mod, w_out, fg):
    bsz, seq_len, _ = xs.shape
    nblk = seq_len // Q_BLOCK
    nstep = nblk // 2
    lc = kx.shape[1]
    assert lc % SOFTMAX_ROWS == 0 and Q_BLOCK % SOFTMAX_ROWS == 0
    cur = lambda w: pl.BlockSpec((1, 2 * Q_BLOCK, w), lambda b, n: (b, n, 0))
    prev = lambda w: pl.BlockSpec((1, Q_BLOCK, w), lambda b, n: (b, jnp.maximum(2 * n - 1, 0), 0))
    nxt = lambda w: pl.BlockSpec((1, Q_BLOCK, w), lambda b, n: (b, jnp.minimum(2 * n + 2, nblk - 1), 0))
    ctx_blk = lambda w: pl.BlockSpec((1, lc, w), lambda b, n: (b, 0, 0))
    kw, vw = 2 * C_KV_W, C_KV_W
    sink_t = jnp.repeat(sink.reshape(C_KV_HEADS, 1, C_GROUP), Q_BLOCK, axis=2)
    return pl.pallas_call(
        _attn_out_kernel,
        grid=(bsz, nstep),
        in_specs=[
            pl.BlockSpec((C_KV_HEADS, 1, C_GROUP * Q_BLOCK), lambda b, n: (0, 0, 0)),
            cur(C_Q_W), prev(kw), cur(kw), nxt(kw), prev(vw), cur(vw), nxt(vw), ctx_blk(kw), ctx_blk(vw),
            cur(C_Q_W), cur(D_MODEL),
            pl.BlockSpec((1, 1, 3 * D_MODEL), lambda b, n: (b, 0, 0)),
            pl.BlockSpec((C_Q_W, D_MODEL), lambda b, n: (0, 0)),
            pl.BlockSpec((1, D_MODEL), lambda b, n: (0, 0)),
        ],
        out_specs=cur(D_MODEL),
        out_shape=jax.ShapeDtypeStruct((bsz, seq_len, D_MODEL), F32),
        compiler_params=_cparams(("parallel", "arbitrary")),
        name="attn_out",
    )(sink_t, q, k, k, k, v, v, v, kx, vx, sg, xs, mod, w_out, fg)


def kernel(x, c, ctx, c_ctx, norm_g, ada_w, ada_b, w_in_ab, v_norm_g, spatial_w, spatial_b, w_out_ab,
           w_in_c, sink_logit, w_out_c, final_g):
    bsz, seq_len, _ = x.shape
    depth = ada_w.shape[0]
    assert depth == 2 and bsz + 1 <= MOD_ROWS
    ctx_row = bsz

    cc = jnp.concatenate([c, c_ctx[None, :], jnp.zeros((MOD_ROWS - bsz - 1, D_MODEL), F32)], axis=0)
    mod = _mod_call(cc, ada_w, ada_b)
    mod0 = mod[0].reshape(MOD_ROWS, 1, 3 * D_MODEL)
    mod1 = mod[1].reshape(MOD_ROWS, 1, 3 * D_MODEL)

    cs = jnp.asarray(_channel_dft_matrix())
    ng0 = norm_g[0].reshape(1, D_MODEL)
    ng1 = norm_g[1].reshape(1, D_MODEL)
    w_in0 = w_in_ab[0].astype(BF16)
    w_out0 = w_out_ab[0].astype(BF16)
    vg = v_norm_g[0].reshape(1, A_W)
    sw = spatial_w[0].astype(BF16)
    sb = jnp.broadcast_to(spatial_b[0][:, :, None], (A_HEADS, CHUNK, A_HEAD_DIM))

    x1 = _mixer_ab_layer(x, mod0, None, ng0, w_in0, vg, sw, sb, w_out0, cs, tl2=128)
    ctx1 = _mixer_ab_layer(ctx, mod0, ctx_row, ng0, w_in0, vg, sw, sb, w_out0, cs,
                           tl2=ctx.shape[1] // RADIX)

    w_in1 = w_in_c[0].astype(BF16)
    w_out1 = w_out_c[0].astype(BF16)
    cos_t, sin_t = _rope_tables(seq_len)
    q, k, v, sg = _in_c_call(x1, mod1, ng1, w_in1, jnp.asarray(cos_t), jnp.asarray(sin_t), tm=512)
    kx, vx = _ctx_kv_call(ctx1, mod1, ctx_row, ng1, w_in1[:, C_Q_W:C_Q_W + 2 * C_KV_W])
    return _attn_out_call(sink_logit[0], q, k, v, kx, vx, sg, x1, mod1, w_out1,
                          final_g.reshape(1, D_MODEL))
```

```python
import functools
import math

import numpy as np
import jax
import jax.numpy as jnp
from jax import lax
from jax.experimental import pallas as pl
from jax.experimental.pallas import tpu as pltpu

F32 = jnp.float32
BF16 = jnp.bfloat16

D_MODEL = 1024
GRID_W = 64
CHUNK = 128
A_HEADS = 4
A_HEAD_DIM = 128
A_W = A_HEADS * A_HEAD_DIM
B_GROUPS = 4
B_GROUP_DIM = 128
B_W = B_GROUPS * B_GROUP_DIM
AB_IN = 3 * A_W + 2 * B_W
C_HEADS = 16
C_KV_HEADS = 4
C_GROUP = C_HEADS // C_KV_HEADS
C_HEAD_DIM = 64
C_Q_W = C_HEADS * C_HEAD_DIM
C_KV_W = C_KV_HEADS * C_HEAD_DIM
C_IN = 2 * C_Q_W + 2 * C_KV_W
WINDOW = 128
Q_BLOCK = 128
ROPE_BASE = 10000.0
NORM_EPS = 1e-6
NEG_INF = -1e30
LOG2E = math.log2(math.e)
RADIX = 4
MOD_ROWS = 16
SOFTMAX_ROWS = 64
SCORE_LOOKAHEAD = 2
V7X_VMEM_LIMIT = 56 * 1024 * 1024


def _silu(x):
    return x * (1.0 / (1.0 + jnp.exp(-x)))


def _cparams(sem):
    return pltpu.CompilerParams(dimension_semantics=sem, vmem_limit_bytes=V7X_VMEM_LIMIT)


def _channel_dft_matrix():
    n = np.arange(B_GROUP_DIM)
    ang = 2.0 * np.pi * np.outer(n, n) / B_GROUP_DIM
    return np.concatenate([np.cos(ang), np.sin(ang)], axis=1).astype(np.float32)


def _position_dft_matrix(seq_len):
    n2 = seq_len // RADIX
    idx = np.arange(n2)
    ang = 2.0 * np.pi * (np.outer(idx, idx) % n2) / n2
    norm = 1.0 / math.sqrt(seq_len * B_GROUP_DIM)
    return np.concatenate([np.cos(ang) * norm, -np.sin(ang) * norm], axis=1).astype(np.float32)


def _twiddle_tables(seq_len):
    n2 = seq_len // RADIX
    l2 = np.arange(n2)[None, :, None]
    k1 = np.arange(RADIX)[:, None, None]
    ang = 2.0 * np.pi * ((l2 * k1) % seq_len) / seq_len
    ang = np.broadcast_to(ang, (RADIX, n2, 128))
    return np.cos(ang).astype(np.float32), np.sin(ang).astype(np.float32)


def _rope_tables(seq_len):
    t = np.arange(seq_len)
    row = (t // GRID_W).astype(np.float64)
    col = (t % GRID_W).astype(np.float64)
    lane = np.arange(128)
    dd = lane % C_HEAD_DIM
    nf = C_HEAD_DIM // 4
    inv = ROPE_BASE ** (-(dd % nf).astype(np.float64) / nf)
    pos = np.where((dd < C_HEAD_DIM // 2)[None, :], row[:, None], col[:, None])
    ang = pos * inv[None, :]
    sign = np.where((dd % (2 * nf)) < nf, -1.0, 1.0)[None, :]
    return np.cos(ang).astype(np.float32), (np.sin(ang) * sign).astype(np.float32)


def _mod_kernel(c_ref, w_ref, b_ref, o_ref):
    s = _silu(c_ref[...]).astype(BF16)
    o_ref[0] = jnp.dot(s, w_ref[0].astype(BF16), preferred_element_type=F32) + b_ref[0]


def _mod_call(cc, ada_w, ada_b):
    depth = ada_w.shape[0]
    tn = 1024
    return pl.pallas_call(
        _mod_kernel,
        grid=(depth, 3 * D_MODEL // tn),
        in_specs=[
            pl.BlockSpec((MOD_ROWS, D_MODEL), lambda l, j: (0, 0)),
            pl.BlockSpec((1, D_MODEL, tn), lambda l, j: (l, 0, j)),
            pl.BlockSpec((1, 1, tn), lambda l, j: (l, 0, j)),
        ],
        out_specs=pl.BlockSpec((1, MOD_ROWS, tn), lambda l, j: (l, 0, j)),
        out_shape=jax.ShapeDtypeStruct((depth, MOD_ROWS, 3 * D_MODEL), F32),
        compiler_params=_cparams(("arbitrary", "arbitrary")),
        name="adaln_mod",
    )(cc, ada_w, ada_b.reshape(depth, 1, 3 * D_MODEL))


def _to_bf16_kernel(x_ref, o_ref):
    o_ref[...] = x_ref[...].astype(BF16)


def _to_bf16_call(a, rows_per_step):
    rows, cols = a.shape
    return pl.pallas_call(
        _to_bf16_kernel,
        grid=(rows // rows_per_step,),
        in_specs=[pl.BlockSpec((rows_per_step, cols), lambda i: (i, 0))],
        out_specs=pl.BlockSpec((rows_per_step, cols), lambda i: (i, 0)),
        out_shape=jax.ShapeDtypeStruct((rows, cols), BF16),
        compiler_params=_cparams(("parallel",)),
        name="to_bf16",
    )(a)


def _modulated_norm(x, mod_row, g):
    shift = mod_row[:, :D_MODEL]
    scale = mod_row[:, D_MODEL:2 * D_MODEL]
    ms = jnp.mean(x * x, axis=-1, keepdims=True)
    h = x * lax.rsqrt(ms + NORM_EPS) * g
    return h * (1.0 + scale) + shift


def _in_ab_kernel(x_ref, mod_ref, ng_ref, w_ref, vg_ref, sw_ref, sb_ref, cs_ref, twc_ref, tws_ref,
                  ya_ref, sgb_ref, uv_ref, *, tl2):
    rows = RADIX * tl2
    x = x_ref[0].reshape(rows, D_MODEL)
    h = _modulated_norm(x, mod_ref[0], ng_ref[...])
    z = jnp.dot(h.astype(BF16), w_ref[...], preferred_element_type=F32)

    v = z[:, A_W:2 * A_W]
    mu = jnp.mean(v, axis=-1, keepdims=True)
    vc = v - mu
    var = jnp.mean(vc * vc, axis=-1, keepdims=True)
    vn = (vc * lax.rsqrt(var + NORM_EPS) * vg_ref[...]).astype(BF16)

    piece = min(CHUNK, tl2)
    for ci in range(rows // CHUNK):
        r0 = ci * CHUNK
        for hd in range(A_HEADS):
            c0 = hd * A_HEAD_DIM
            sv = jnp.dot(sw_ref[hd], vn[r0:r0 + CHUNK, c0:c0 + A_HEAD_DIM],
                         preferred_element_type=F32) + sb_ref[hd]
            u = z[r0:r0 + CHUNK, c0:c0 + A_HEAD_DIM]
            ga = z[r0:r0 + CHUNK, 2 * A_W + c0:2 * A_W + c0 + A_HEAD_DIM]
            ya = (u * sv * _silu(ga)).astype(BF16)
            for p0 in range(0, CHUNK, piece):
                l1, off = divmod(r0 + p0, tl2)
                ya_ref[0, l1, off:off + piece, c0:c0 + A_HEAD_DIM] = ya[p0:p0 + piece]

    gb = z[:, 3 * A_W + B_W:]
    sgb_ref[0] = _silu(gb).astype(BF16).reshape(RADIX, tl2, B_W)

    xb = z[:, 3 * A_W:3 * A_W + B_W].astype(BF16)
    cs = cs_ref[...].astype(BF16)
    for g in range(B_GROUPS):
        c0 = g * B_GROUP_DIM
        ps, qs = [], []
        for j in range(RADIX):
            pq = jnp.dot(xb[j * tl2:(j + 1) * tl2, c0:c0 + B_GROUP_DIM], cs,
                         preferred_element_type=F32)
            ps.append(pq[:, :B_GROUP_DIM])
            qs.append(pq[:, B_GROUP_DIM:])
        p02, p13 = ps[0] - ps[2], ps[1] - ps[3]
        q02, q13 = qs[0] - qs[2], qs[1] - qs[3]
        pe, po = ps[0] + ps[2], ps[1] + ps[3]
        qe, qo = qs[0] + qs[2], qs[1] + qs[3]
        us = [pe + po, p02 - q13, pe - po, p02 + q13]
        vs = [qe + qo, q02 + p13, qe - qo, q02 - p13]
        for k1 in range(RADIX):
            if k1 == 0:
                ut, vt = us[0], vs[0]
            else:
                tc, ts = twc_ref[k1], tws_ref[k1]
                ut = us[k1] * tc - vs[k1] * ts
                vt = us[k1] * ts + vs[k1] * tc
            uv_ref[0, k1, 0, :, c0:c0 + B_GROUP_DIM] = ut.astype(BF16)
            uv_ref[0, k1, 1, :, c0:c0 + B_GROUP_DIM] = vt.astype(BF16)


def _in_ab_call(xs, mod, mod_row, ng, w_in, vg, sw, sb, cs, twc, tws, tl2):
    bsz, seq_len, _ = xs.shape
    n2 = seq_len // RADIX
    x4 = xs.reshape(bsz, RADIX, n2, D_MODEL)
    row_of = (lambda b: b) if mod_row is None else (lambda b: mod_row)
    const2 = lambda b, i: (0, 0)
    const3 = lambda b, i: (0, 0, 0)
    return pl.pallas_call(
        functools.partial(_in_ab_kernel, tl2=tl2),
        grid=(bsz, n2 // tl2),
        in_specs=[
            pl.BlockSpec((1, RADIX, tl2, D_MODEL), lambda b, i: (b, 0, i, 0)),
            pl.BlockSpec((1, 1, 3 * D_MODEL), lambda b, i: (row_of(b), 0, 0)),
            pl.BlockSpec((1, D_MODEL), const2),
            pl.BlockSpec((D_MODEL, AB_IN), const2),
            pl.BlockSpec((1, A_W), const2),
            pl.BlockSpec((A_HEADS, CHUNK, CHUNK), const3),
            pl.BlockSpec((A_HEADS, CHUNK, A_HEAD_DIM), const3),
            pl.BlockSpec((B_GROUP_DIM, 2 * B_GROUP_DIM), const2),
            pl.BlockSpec((RADIX, tl2, 128), lambda b, i: (0, i, 0)),
            pl.BlockSpec((RADIX, tl2, 128), lambda b, i: (0, i, 0)),
        ],
        out_specs=[
            pl.BlockSpec((1, RADIX, tl2, A_W), lambda b, i: (b, 0, i, 0)),
            pl.BlockSpec((1, RADIX, tl2, B_W), lambda b, i: (b, 0, i, 0)),
            pl.BlockSpec((1, RADIX, 2, tl2, B_W), lambda b, i: (b, 0, 0, i, 0)),
        ],
        out_shape=[
            jax.ShapeDtypeStruct((bsz, RADIX, n2, A_W), BF16),
            jax.ShapeDtypeStruct((bsz, RADIX, n2, B_W), BF16),
            jax.ShapeDtypeStruct((bsz, RADIX, 2, n2, B_W), BF16),
        ],
        compiler_params=_cparams(("parallel", "arbitrary")),
        name="in_ab",
    )(x4, mod, ng, w_in, vg, sw, sb, cs, twc, tws)


def _dft_out_kernel(g_ref, uv_ref, ya_ref, sgb_ref, x_ref, mod_ref, w_ref, o_ref, f_ref, *, tk):
    for k1 in range(RADIX):
        f = jnp.dot(g_ref[...], uv_ref[0, k1], preferred_element_type=F32)
        for g in range(B_GROUPS):
            f_ref[g, pl.ds(k1, tk, stride=RADIX), :] = f[:, g * B_GROUP_DIM:(g + 1) * B_GROUP_DIM]
    fnat = jnp.concatenate([f_ref[g] for g in range(B_GROUPS)], axis=1)
    yb = (fnat * sgb_ref[0].astype(F32)).astype(BF16)
    y = jnp.dot(ya_ref[0], w_ref[:A_W], preferred_element_type=F32)
    y = y + jnp.dot(yb, w_ref[A_W:], preferred_element_type=F32)
    gate = mod_ref[0][:, 2 * D_MODEL:]
    o_ref[0] = x_ref[0] + gate * y


def _dft_out_call(gmat, uv, ya, sgb, xs, mod, mod_row, w_out, tk):
    bsz, seq_len, _ = xs.shape
    n2 = seq_len // RADIX
    rows = RADIX * tk
    row_of = (lambda b: b) if mod_row is None else (lambda b: mod_row)
    nat = lambda w: pl.BlockSpec((1, rows, w), lambda b, i: (b, i, 0))
    return pl.pallas_call(
        functools.partial(_dft_out_kernel, tk=tk),
        grid=(bsz, n2 // tk),
        in_specs=[
            pl.BlockSpec((tk, 2 * n2), lambda b, i: (i, 0)),
            pl.BlockSpec((1, RADIX, 2 * n2, B_W), lambda b, i: (b, 0, 0, 0)),
            nat(A_W), nat(B_W), nat(D_MODEL),
            pl.BlockSpec((1, 1, 3 * D_MODEL), lambda b, i: (row_of(b), 0, 0)),
            pl.BlockSpec((A_W + B_W, D_MODEL), lambda b, i: (0, 0), pipeline_mode=pl.Buffered(1)),
        ],
        out_specs=nat(D_MODEL),
        out_shape=jax.ShapeDtypeStruct((bsz, seq_len, D_MODEL), F32),
        scratch_shapes=[pltpu.VMEM((B_GROUPS, rows, B_GROUP_DIM), F32)],
        compiler_params=_cparams(("parallel", "arbitrary")),
        name="dft_out",
    )(gmat, uv.reshape(bsz, RADIX, 2 * n2, B_W), ya.reshape(bsz, seq_len, A_W),
      sgb.reshape(bsz, seq_len, B_W), xs, mod, w_out)


def _mixer_ab_layer(xs, mod, mod_row, ng, w_in, vg, sw, sb, w_out, cs, tl2, tk):
    seq_len = xs.shape[1]
    twc, tws = _twiddle_tables(seq_len)
    gmat = _position_dft_matrix(seq_len)
    gmat = _to_bf16_call(jnp.asarray(gmat), min(256, gmat.shape[0]))
    ya, sgb, uv = _in_ab_call(xs, mod, mod_row, ng, w_in, vg, sw, sb, cs,
                              jnp.asarray(twc), jnp.asarray(tws), tl2)
    return _dft_out_call(gmat, uv, ya, sgb, xs, mod, mod_row, w_out, tk=tk)


def _rope_block(t, cos, sin_signed, lane_lo):
    nf = C_HEAD_DIM // 4
    swapped = jnp.where(lane_lo, pltpu.roll(t, 128 - nf, axis=1), pltpu.roll(t, nf, axis=1))
    return t * cos + swapped * sin_signed


def _store_dup_heads(k_ref, c, t, lane):
    r = pltpu.roll(t, C_HEAD_DIM, axis=1)
    first = lane < C_HEAD_DIM
    k_ref[0, :, (2 * c) * 128:(2 * c + 1) * 128] = jnp.where(first, t, r).astype(BF16)
    k_ref[0, :, (2 * c + 1) * 128:(2 * c + 2) * 128] = jnp.where(first, r, t).astype(BF16)


def _in_c_kernel(x_ref, mod_ref, ng_ref, w_ref, cos_ref, sin_ref, q_ref, k_ref, v_ref, sg_ref):
    h = _modulated_norm(x_ref[0], mod_ref[0], ng_ref[...])
    z = jnp.dot(h.astype(BF16), w_ref[...], preferred_element_type=F32)
    cos = cos_ref[...]
    sin = sin_ref[...]
    lane = lax.broadcasted_iota(jnp.int32, cos.shape, 1)
    lane_lo = (lane % (C_HEAD_DIM // 2)) < (C_HEAD_DIM // 4)
    qscale = C_HEAD_DIM ** -0.5 * LOG2E
    for c in range(C_Q_W // 128):
        t = _rope_block(z[:, c * 128:(c + 1) * 128], cos, sin, lane_lo)
        q_ref[0, :, c * 128:(c + 1) * 128] = (t * qscale).astype(BF16)
    for c in range(C_KV_W // 128):
        t = _rope_block(z[:, C_Q_W + c * 128:C_Q_W + (c + 1) * 128], cos, sin, lane_lo)
        _store_dup_heads(k_ref, c, t, lane)
    v_ref[0] = z[:, C_Q_W + C_KV_W:C_Q_W + 2 * C_KV_W].astype(BF16)
    sg_ref[0] = _silu(z[:, C_Q_W + 2 * C_KV_W:]).astype(BF16)


def _in_c_call(xs, mod, ng, w_in, cos_t, sin_t, tm):
    bsz, seq_len, _ = xs.shape
    const2 = lambda b, i: (0, 0)
    row_blk = lambda w: pl.BlockSpec((1, tm, w), lambda b, i: (b, i, 0))
    return pl.pallas_call(
        _in_c_kernel,
        grid=(bsz, seq_len // tm),
        in_specs=[
            row_blk(D_MODEL),
            pl.BlockSpec((1, 1, 3 * D_MODEL), lambda b, i: (b, 0, 0)),
            pl.BlockSpec((1, D_MODEL), const2),
            pl.BlockSpec((D_MODEL, C_IN), const2),
            pl.BlockSpec((tm, 128), lambda b, i: (i, 0)),
            pl.BlockSpec((tm, 128), lambda b, i: (i, 0)),
        ],
        out_specs=[row_blk(C_Q_W), row_blk(2 * C_KV_W), row_blk(C_KV_W), row_blk(C_Q_W)],
        out_shape=[
            jax.ShapeDtypeStruct((bsz, seq_len, C_Q_W), BF16),
            jax.ShapeDtypeStruct((bsz, seq_len, 2 * C_KV_W), BF16),
            jax.ShapeDtypeStruct((bsz, seq_len, C_KV_W), BF16),
            jax.ShapeDtypeStruct((bsz, seq_len, C_Q_W), BF16),
        ],
        compiler_params=_cparams(("parallel", "arbitrary")),
        name="in_c",
    )(xs, mod, ng, w_in, cos_t, sin_t)


def _ctx_kv_kernel(x_ref, mod_ref, ng_ref, w_ref, k_ref, v_ref):
    h = _modulated_norm(x_ref[0], mod_ref[0], ng_ref[...])
    z = jnp.dot(h.astype(BF16), w_ref[...], preferred_element_type=F32)
    lane = lax.broadcasted_iota(jnp.int32, (z.shape[0], 128), 1)
    for c in range(C_KV_W // 128):
        _store_dup_heads(k_ref, c, z[:, c * 128:(c + 1) * 128], lane)
    v_ref[0] = z[:, C_KV_W:].astype(BF16)


def _ctx_kv_call(ctx, mod, mod_row, ng, w_kv):
    bsz, lc, _ = ctx.shape
    return pl.pallas_call(
        _ctx_kv_kernel,
        grid=(bsz,),
        in_specs=[
            pl.BlockSpec((1, lc, D_MODEL), lambda b: (b, 0, 0)),
            pl.BlockSpec((1, 1, 3 * D_MODEL), lambda b: (mod_row, 0, 0)),
            pl.BlockSpec((1, D_MODEL), lambda b: (0, 0)),
            pl.BlockSpec((D_MODEL, 2 * C_KV_W), lambda b: (0, 0)),
        ],
        out_specs=[pl.BlockSpec((1, lc, 2 * C_KV_W), lambda b: (b, 0, 0)),
                   pl.BlockSpec((1, lc, C_KV_W), lambda b: (b, 0, 0))],
        out_shape=[jax.ShapeDtypeStruct((bsz, lc, 2 * C_KV_W), BF16),
                   jax.ShapeDtypeStruct((bsz, lc, C_KV_W), BF16)],
        compiler_params=_cparams(("parallel",)),
        name="ctx_kv",
    )(ctx, mod, ng, w_kv)


def _attn_out_kernel(sink_ref, q_ref, kp_ref, kc_ref, kn_ref, vp_ref, vc_ref, vn_ref, kx_ref, vx_ref,
                     sg_ref, x_ref, mod_ref, w_ref, fg_ref, o_ref):
    n = pl.program_id(1)
    last = pl.num_programs(1) - 1
    gq = C_GROUP * Q_BLOCK
    kj = lax.broadcasted_iota(jnp.int32, (Q_BLOCK, Q_BLOCK), 0)
    qi = lax.broadcasted_iota(jnp.int32, (Q_BLOCK, Q_BLOCK), 1)
    bias_before = jnp.where(kj >= qi, 0.0, NEG_INF).astype(F32)
    bias_after = jnp.where(kj <= qi, 0.0, NEG_INF).astype(F32)
    edge_first = jnp.where(n > 0, 0.0, NEG_INF).astype(F32)
    edge_last = jnp.where(n < last, 0.0, NEG_INF).astype(F32)
    lane = lax.broadcasted_iota(jnp.int32, (Q_BLOCK, 128), 1)
    first = lane < C_HEAD_DIM
    zero = jnp.zeros((Q_BLOCK, 128), BF16)
    hi = slice(Q_BLOCK, 2 * Q_BLOCK)
    lo = slice(0, Q_BLOCK)

    def scores(qb, kh):
        kl = slice(kh * 128, (kh + 1) * 128)
        if qb == 0:
            kblocks = [kx_ref[0, :, kl], kp_ref[0, :, kl], kc_ref[0, lo, kl], kc_ref[0, hi, kl]]
            biases = (None, (bias_before, edge_first), None, (bias_after, None))
        else:
            kblocks = [kx_ref[0, :, kl], kc_ref[0, lo, kl], kc_ref[0, hi, kl], kn_ref[0, :, kl]]
            biases = (None, (bias_before, None), None, (bias_after, edge_last))
        rows_q = slice(qb * Q_BLOCK, (qb + 1) * Q_BLOCK)
        q4 = []
        for c in range(2):
            qv = q_ref[0, rows_q, kh * 256 + c * 128:kh * 256 + (c + 1) * 128]
            q4 += [jnp.where(first, qv, zero), jnp.where(first, zero, qv)]
        q4 = jnp.concatenate(q4, axis=0)
        k2 = jnp.concatenate(kblocks, axis=0)
        st = lax.dot_general(k2, q4, (((1,), (1,)), ((), ())), preferred_element_type=F32)
        blocks, m8, r = [], None, 0
        for kb, bias in zip(kblocks, biases):
            sb = st[r:r + kb.shape[0]]
            r += kb.shape[0]
            if bias is not None:
                mask_bias, edge = bias
                cols = [sb[:, j * Q_BLOCK:(j + 1) * Q_BLOCK] + mask_bias for j in range(C_GROUP)]
                sb = jnp.concatenate(cols, axis=1)
                if edge is not None:
                    sb = sb + edge
            blocks.append(sb)
            mb = jnp.max(sb.reshape(sb.shape[0] // 8, 8, gq), axis=0)
            m8 = mb if m8 is None else jnp.maximum(m8, mb)
        return blocks, m8

    def softmax_pv(qb, kh, blocks, m8):
        sink2 = sink_ref[kh] * LOG2E
        m = jnp.maximum(jnp.max(m8, axis=0, keepdims=True), sink2)
        acc8 = jnp.zeros((8, gq), F32)
        pts = []
        for blk in blocks:
            for r in range(0, blk.shape[0], SOFTMAX_ROWS):
                p = jnp.exp2(blk[r:r + SOFTMAX_ROWS] - m)
                acc8 = acc8 + jnp.sum(p.reshape(SOFTMAX_ROWS // 8, 8, gq), axis=0)
                pts.append(p.astype(BF16))
        pt = jnp.concatenate(pts, axis=0)
        denom = jnp.sum(acc8, axis=0, keepdims=True) + jnp.exp2(sink2 - m)
        vl = slice((kh // 2) * 128, (kh // 2 + 1) * 128)
        if qb == 0:
            vblocks = [vx_ref[0, :, vl], vp_ref[0, :, vl], vc_ref[0, :, vl]]
        else:
            vblocks = [vx_ref[0, :, vl], vc_ref[0, :, vl], vn_ref[0, :, vl]]
        vpair = jnp.concatenate(vblocks, axis=0)
        ot = lax.dot_general(vpair, pt, (((0,), (0,)), ((), ())), preferred_element_type=F32)
        r0 = (kh % 2) * C_HEAD_DIM
        ot = ot[r0:r0 + C_HEAD_DIM] / denom
        cols = []
        for c in range(2):
            pair = jnp.concatenate([ot[:, (2 * c) * Q_BLOCK:(2 * c + 1) * Q_BLOCK],
                                    ot[:, (2 * c + 1) * Q_BLOCK:(2 * c + 2) * Q_BLOCK]], axis=0)
            cols.append(pair.T)
        return cols

    chains = [(qb, kh) for qb in range(2) for kh in range(C_KV_HEADS)]
    o_cols = {0: [], 1: []}
    pending = [scores(*ch) for ch in chains[:SCORE_LOOKAHEAD]]
    for idx, (qb, kh) in enumerate(chains):
        if idx + SCORE_LOOKAHEAD < len(chains):
            pending.append(scores(*chains[idx + SCORE_LOOKAHEAD]))
        o_cols[qb] += softmax_pv(qb, kh, *pending.pop(0))
    o_all = jnp.concatenate([jnp.concatenate(o_cols[0], axis=1),
                             jnp.concatenate(o_cols[1], axis=1)], axis=0)
    y = jnp.dot((o_all * sg_ref[0].astype(F32)).astype(BF16), w_ref[...], preferred_element_type=F32)
    gate = mod_ref[0][:, 2 * D_MODEL:]
    x2 = x_ref[0] + gate * y
    ms = jnp.mean(x2 * x2, axis=-1, keepdims=True)
    o_ref[0] = x2 * lax.rsqrt(ms + NORM_EPS) * fg_ref[...]


def _attn_out_call(sink, q, k, v, kx, vx, sg, xs, mod, w_out, fg):
    bsz, seq_len, _ = xs.shape
    nblk = seq_len // Q_BLOCK
    nstep = nblk // 2
    lc = kx.shape[1]
    assert lc % SOFTMAX_ROWS == 0 and Q_BLOCK % SOFTMAX_ROWS == 0
    cur = lambda w: pl.BlockSpec((1, 2 * Q_BLOCK, w), lambda b, n: (b, n, 0))
    prev = lambda w: pl.BlockSpec((1, Q_BLOCK, w), lambda b, n: (b, jnp.maximum(2 * n - 1, 0), 0))
    nxt = lambda w: pl.BlockSpec((1, Q_BLOCK, w), lambda b, n: (b, jnp.minimum(2 * n + 2, nblk - 1), 0))
    ctx_blk = lambda w: pl.BlockSpec((1, lc, w), lambda b, n: (b, 0, 0))
    kw, vw = 2 * C_KV_W, C_KV_W
    sink_t = jnp.repeat(sink.reshape(C_KV_HEADS, 1, C_GROUP), Q_BLOCK, axis=2)
    return pl.pallas_call(
        _attn_out_kernel,
        grid=(bsz, nstep),
        in_specs=[
            pl.BlockSpec((C_KV_HEADS, 1, C_GROUP * Q_BLOCK), lambda b, n: (0, 0, 0)),
            cur(C_Q_W), prev(kw), cur(kw), nxt(kw), prev(vw), cur(vw), nxt(vw), ctx_blk(kw), ctx_blk(vw),
            cur(C_Q_W), cur(D_MODEL),
            pl.BlockSpec((1, 1, 3 * D_MODEL), lambda b, n: (b, 0, 0)),
            pl.BlockSpec((C_Q_W, D_MODEL), lambda b, n: (0, 0)),
            pl.BlockSpec((1, D_MODEL), lambda b, n: (0, 0)),
        ],
        out_specs=cur(D_MODEL),
        out_shape=jax.ShapeDtypeStruct((bsz, seq_len, D_MODEL), F32),
        compiler_params=_cparams(("parallel", "arbitrary")),
        name="attn_out",
    )(sink_t, q, k, k, k, v, v, v, kx, vx, sg, xs, mod, w_out, fg)


def kernel(x, c, ctx, c_ctx, norm_g, ada_w, ada_b, w_in_ab, v_norm_g, spatial_w, spatial_b, w_out_ab,
           w_in_c, sink_logit, w_out_c, final_g):
    bsz, seq_len, _ = x.shape
    depth = ada_w.shape[0]
    assert depth == 2 and bsz + 1 <= MOD_ROWS
    ctx_row = bsz

    cc = jnp.concatenate([c, c_ctx[None, :], jnp.zeros((MOD_ROWS - bsz - 1, D_MODEL), F32)], axis=0)
    mod = _mod_call(cc, ada_w, ada_b)
    mod0 = mod[0].reshape(MOD_ROWS, 1, 3 * D_MODEL)
    mod1 = mod[1].reshape(MOD_ROWS, 1, 3 * D_MODEL)

    cs = jnp.asarray(_channel_dft_matrix())
    ng0 = norm_g[0].reshape(1, D_MODEL)
    ng1 = norm_g[1].reshape(1, D_MODEL)
    w_in0 = w_in_ab[0].astype(BF16)
    w_out0 = w_out_ab[0].astype(BF16)
    vg = v_norm_g[0].reshape(1, A_W)
    sw = spatial_w[0].astype(BF16)
    sb = jnp.broadcast_to(spatial_b[0][:, :, None], (A_HEADS, CHUNK, A_HEAD_DIM))

    x1 = _mixer_ab_layer(x, mod0, None, ng0, w_in0, vg, sw, sb, w_out0, cs, tl2=128, tk=256)
    ctx1 = _mixer_ab_layer(ctx, mod0, ctx_row, ng0, w_in0, vg, sw, sb, w_out0, cs,
                           tl2=ctx.shape[1] // RADIX, tk=ctx.shape[1] // RADIX)

    w_in1 = w_in_c[0].astype(BF16)
    w_out1 = w_out_c[0].astype(BF16)
    cos_t, sin_t = _rope_tables(seq_len)
    q, k, v, sg = _in_c_call(x1, mod1, ng1, w_in1, jnp.asarray(cos_t), jnp.asarray(sin_t), tm=512)
    kx, vx = _ctx_kv_call(ctx1, mod1, ctx_row, ng1, w_in1[:, C_Q_W:C_Q_W + 2 * C_KV_W])
    return _attn_out_call(sink_logit[0], q, k, v, kx, vx, sg, x1, mod1, w_out1,
                          final_g.reshape(1, D_MODEL))
```

```python
import functools
import math

import numpy as np
import jax
import jax.numpy as jnp
from jax import lax
from jax.experimental import pallas as pl
from jax.experimental.pallas import tpu as pltpu

F32 = jnp.float32
BF16 = jnp.bfloat16

D_MODEL = 1024
GRID_W = 64
CHUNK = 128
A_HEADS = 4
A_HEAD_DIM = 128
A_W = A_HEADS * A_HEAD_DIM
B_GROUPS = 4
B_GROUP_DIM = 128
B_W = B_GROUPS * B_GROUP_DIM
AB_IN = 3 * A_W + 2 * B_W
C_HEADS = 16
C_KV_HEADS = 4
C_GROUP = C_HEADS // C_KV_HEADS
C_HEAD_DIM = 64
C_Q_W = C_HEADS * C_HEAD_DIM
C_KV_W = C_KV_HEADS * C_HEAD_DIM
C_IN = 2 * C_Q_W + 2 * C_KV_W
WINDOW = 128
Q_BLOCK = 128
ROPE_BASE = 10000.0
NORM_EPS = 1e-6
NEG_INF = -1e30
LOG2E = math.log2(math.e)
RADIX = 4
MOD_ROWS = 16
SOFTMAX_ROWS = 64
SCORE_LOOKAHEAD = 2
V7X_VMEM_LIMIT = 56 * 1024 * 1024


def _silu(x):
    return x * (1.0 / (1.0 + jnp.exp(-x)))


def _cparams(sem):
    return pltpu.CompilerParams(dimension_semantics=sem, vmem_limit_bytes=V7X_VMEM_LIMIT)


def _channel_dft_matrix():
    n = np.arange(B_GROUP_DIM)
    ang = 2.0 * np.pi * np.outer(n, n) / B_GROUP_DIM
    return np.concatenate([np.cos(ang), np.sin(ang)], axis=1).astype(np.float32)


def _position_dft_matrix(seq_len):
    n2 = seq_len // RADIX
    idx = np.arange(n2)
    ang = 2.0 * np.pi * (np.outer(idx, idx) % n2) / n2
    norm = 1.0 / math.sqrt(seq_len * B_GROUP_DIM)
    return np.concatenate([np.cos(ang) * norm, -np.sin(ang) * norm], axis=1).astype(np.float32)


def _twiddle_tables(seq_len):
    n2 = seq_len // RADIX
    l2 = np.arange(n2)[None, :, None]
    k1 = np.arange(RADIX)[:, None, None]
    ang = 2.0 * np.pi * ((l2 * k1) % seq_len) / seq_len
    ang = np.broadcast_to(ang, (RADIX, n2, 128))
    return np.cos(ang).astype(np.float32), np.sin(ang).astype(np.float32)


def _rope_tables(seq_len):
    t = np.arange(seq_len)
    row = (t // GRID_W).astype(np.float64)
    col = (t % GRID_W).astype(np.float64)
    lane = np.arange(128)
    dd = lane % C_HEAD_DIM
    nf = C_HEAD_DIM // 4
    inv = ROPE_BASE ** (-(dd % nf).astype(np.float64) / nf)
    pos = np.where((dd < C_HEAD_DIM // 2)[None, :], row[:, None], col[:, None])
    ang = pos * inv[None, :]
    sign = np.where((dd % (2 * nf)) < nf, -1.0, 1.0)[None, :]
    return np.cos(ang).astype(np.float32), (np.sin(ang) * sign).astype(np.float32)


def _mod_kernel(c_ref, w_ref, b_ref, o_ref):
    s = _silu(c_ref[...]).astype(BF16)
    o_ref[0] = jnp.dot(s, w_ref[0].astype(BF16), preferred_element_type=F32) + b_ref[0]


def _mod_call(cc, ada_w, ada_b):
    depth = ada_w.shape[0]
    tn = 1024
    return pl.pallas_call(
        _mod_kernel,
        grid=(depth, 3 * D_MODEL // tn),
        in_specs=[
            pl.BlockSpec((MOD_ROWS, D_MODEL), lambda l, j: (0, 0)),
            pl.BlockSpec((1, D_MODEL, tn), lambda l, j: (l, 0, j)),
            pl.BlockSpec((1, 1, tn), lambda l, j: (l, 0, j)),
        ],
        out_specs=pl.BlockSpec((1, MOD_ROWS, tn), lambda l, j: (l, 0, j)),
        out_shape=jax.ShapeDtypeStruct((depth, MOD_ROWS, 3 * D_MODEL), F32),
        compiler_params=_cparams(("arbitrary", "arbitrary")),
        name="adaln_mod",
    )(cc, ada_w, ada_b.reshape(depth, 1, 3 * D_MODEL))


def _to_bf16_kernel(x_ref, o_ref):
    o_ref[...] = x_ref[...].astype(BF16)


def _to_bf16_call(a, rows_per_step):
    rows, cols = a.shape
    return pl.pallas_call(
        _to_bf16_kernel,
        grid=(rows // rows_per_step,),
        in_specs=[pl.BlockSpec((rows_per_step, cols), lambda i: (i, 0))],
        out_specs=pl.BlockSpec((rows_per_step, cols), lambda i: (i, 0)),
        out_shape=jax.ShapeDtypeStruct((rows, cols), BF16),
        compiler_params=_cparams(("parallel",)),
        name="to_bf16",
    )(a)


def _modulated_norm(x, mod_row, g):
    shift = mod_row[:, :D_MODEL]
    scale = mod_row[:, D_MODEL:2 * D_MODEL]
    ms = jnp.mean(x * x, axis=-1, keepdims=True)
    h = x * lax.rsqrt(ms + NORM_EPS) * g
    return h * (1.0 + scale) + shift


def _in_ab_kernel(x_ref, mod_ref, ng_ref, w_ref, vg_ref, sw_ref, sb_ref, cs_ref, twc_ref, tws_ref,
                  ya_ref, sgb_ref, uv_ref, *, tl2):
    rows = RADIX * tl2
    x = x_ref[0].reshape(rows, D_MODEL)
    h = _modulated_norm(x, mod_ref[0], ng_ref[...])
    z = jnp.dot(h.astype(BF16), w_ref[...], preferred_element_type=F32)

    v = z[:, A_W:2 * A_W]
    mu = jnp.mean(v, axis=-1, keepdims=True)
    vc = v - mu
    var = jnp.mean(vc * vc, axis=-1, keepdims=True)
    vn = (vc * lax.rsqrt(var + NORM_EPS) * vg_ref[...]).astype(BF16)

    piece = min(CHUNK, tl2)
    nchunk = rows // CHUNK
    assert nchunk % 2 == 0
    for hd in range(A_HEADS):
        c0 = hd * A_HEAD_DIM
        for cp in range(nchunk // 2):
            ra, rb = 2 * cp * CHUNK, (2 * cp + 1) * CHUNK
            vpair = jnp.concatenate([vn[ra:ra + CHUNK, c0:c0 + A_HEAD_DIM],
                                     vn[rb:rb + CHUNK, c0:c0 + A_HEAD_DIM]], axis=1)
            sv2 = jnp.dot(sw_ref[hd], vpair, preferred_element_type=F32)
            for half, r0 in enumerate((ra, rb)):
                sv = sv2[:, half * A_HEAD_DIM:(half + 1) * A_HEAD_DIM] + sb_ref[hd]
                u = z[r0:r0 + CHUNK, c0:c0 + A_HEAD_DIM]
                ga = z[r0:r0 + CHUNK, 2 * A_W + c0:2 * A_W + c0 + A_HEAD_DIM]
                ya = (u * sv * _silu(ga)).astype(BF16)
                for p0 in range(0, CHUNK, piece):
                    l1, off = divmod(r0 + p0, tl2)
                    ya_ref[0, l1, off:off + piece, c0:c0 + A_HEAD_DIM] = ya[p0:p0 + piece]

    gb = z[:, 3 * A_W + B_W:]
    sgb_ref[0] = _silu(gb).astype(BF16).reshape(RADIX, tl2, B_W)

    xb = z[:, 3 * A_W:3 * A_W + B_W].astype(BF16)
    cs = cs_ref[...].astype(BF16)
    for g in range(B_GROUPS):
        c0 = g * B_GROUP_DIM
        ps, qs = [], []
        for j in range(RADIX):
            pq = jnp.dot(xb[j * tl2:(j + 1) * tl2, c0:c0 + B_GROUP_DIM], cs,
                         preferred_element_type=F32)
            ps.append(pq[:, :B_GROUP_DIM])
            qs.append(pq[:, B_GROUP_DIM:])
        p02, p13 = ps[0] - ps[2], ps[1] - ps[3]
        q02, q13 = qs[0] - qs[2], qs[1] - qs[3]
        pe, po = ps[0] + ps[2], ps[1] + ps[3]
        qe, qo = qs[0] + qs[2], qs[1] + qs[3]
        us = [pe + po, p02 - q13, pe - po, p02 + q13]
        vs = [qe + qo, q02 + p13, qe - qo, q02 - p13]
        for k1 in range(RADIX):
            if k1 == 0:
                ut, vt = us[0], vs[0]
            else:
                tc, ts = twc_ref[k1], tws_ref[k1]
                ut = us[k1] * tc - vs[k1] * ts
                vt = us[k1] * ts + vs[k1] * tc
            uv_ref[0, k1, 0, :, c0:c0 + B_GROUP_DIM] = ut.astype(BF16)
            uv_ref[0, k1, 1, :, c0:c0 + B_GROUP_DIM] = vt.astype(BF16)


def _in_ab_call(xs, mod, mod_row, ng, w_in, vg, sw, sb, cs, twc, tws, tl2):
    bsz, seq_len, _ = xs.shape
    n2 = seq_len // RADIX
    x4 = xs.reshape(bsz, RADIX, n2, D_MODEL)
    row_of = (lambda b: b) if mod_row is None else (lambda b: mod_row)
    const2 = lambda b, i: (0, 0)
    const3 = lambda b, i: (0, 0, 0)
    return pl.pallas_call(
        functools.partial(_in_ab_kernel, tl2=tl2),
        grid=(bsz, n2 // tl2),
        in_specs=[
            pl.BlockSpec((1, RADIX, tl2, D_MODEL), lambda b, i: (b, 0, i, 0)),
            pl.BlockSpec((1, 1, 3 * D_MODEL), lambda b, i: (row_of(b), 0, 0)),
            pl.BlockSpec((1, D_MODEL), const2),
            pl.BlockSpec((D_MODEL, AB_IN), const2),
            pl.BlockSpec((1, A_W), const2),
            pl.BlockSpec((A_HEADS, CHUNK, CHUNK), const3),
            pl.BlockSpec((A_HEADS, CHUNK, A_HEAD_DIM), const3),
            pl.BlockSpec((B_GROUP_DIM, 2 * B_GROUP_DIM), const2),
            pl.BlockSpec((RADIX, tl2, 128), lambda b, i: (0, i, 0)),
            pl.BlockSpec((RADIX, tl2, 128), lambda b, i: (0, i, 0)),
        ],
        out_specs=[
            pl.BlockSpec((1, RADIX, tl2, A_W), lambda b, i: (b, 0, i, 0)),
            pl.BlockSpec((1, RADIX, tl2, B_W), lambda b, i: (b, 0, i, 0)),
            pl.BlockSpec((1, RADIX, 2, tl2, B_W), lambda b, i: (b, 0, 0, i, 0)),
        ],
        out_shape=[
            jax.ShapeDtypeStruct((bsz, RADIX, n2, A_W), BF16),
            jax.ShapeDtypeStruct((bsz, RADIX, n2, B_W), BF16),
            jax.ShapeDtypeStruct((bsz, RADIX, 2, n2, B_W), BF16),
        ],
        compiler_params=_cparams(("parallel", "arbitrary")),
        name="in_ab",
    )(x4, mod, ng, w_in, vg, sw, sb, cs, twc, tws)


def _dft_out_kernel(g_ref, uv_ref, ya_ref, sgb_ref, x_ref, mod_ref, w_ref, o_ref, f_ref, *, tk):
    for k1 in range(RADIX):
        f = jnp.dot(g_ref[...], uv_ref[0, k1], preferred_element_type=F32)
        for g in range(B_GROUPS):
            f_ref[g, pl.ds(k1, tk, stride=RADIX), :] = f[:, g * B_GROUP_DIM:(g + 1) * B_GROUP_DIM]
    fnat = jnp.concatenate([f_ref[g] for g in range(B_GROUPS)], axis=1)
    yb = (fnat * sgb_ref[0].astype(F32)).astype(BF16)
    y = jnp.dot(ya_ref[0], w_ref[:A_W], preferred_element_type=F32)
    y = y + jnp.dot(yb, w_ref[A_W:], preferred_element_type=F32)
    gate = mod_ref[0][:, 2 * D_MODEL:]
    o_ref[0] = x_ref[0] + gate * y


def _dft_out_call(gmat, uv, ya, sgb, xs, mod, mod_row, w_out, tk):
    bsz, seq_len, _ = xs.shape
    n2 = seq_len // RADIX
    rows = RADIX * tk
    row_of = (lambda b: b) if mod_row is None else (lambda b: mod_row)
    nat = lambda w: pl.BlockSpec((1, rows, w), lambda b, i: (b, i, 0))
    return pl.pallas_call(
        functools.partial(_dft_out_kernel, tk=tk),
        grid=(bsz, n2 // tk),
        in_specs=[
            pl.BlockSpec((tk, 2 * n2), lambda b, i: (i, 0)),
            pl.BlockSpec((1, RADIX, 2 * n2, B_W), lambda b, i: (b, 0, 0, 0)),
            nat(A_W), nat(B_W), nat(D_MODEL),
            pl.BlockSpec((1, 1, 3 * D_MODEL), lambda b, i: (row_of(b), 0, 0)),
            pl.BlockSpec((A_W + B_W, D_MODEL), lambda b, i: (0, 0), pipeline_mode=pl.Buffered(1)),
        ],
        out_specs=nat(D_MODEL),
        out_shape=jax.ShapeDtypeStruct((bsz, seq_len, D_MODEL), F32),
        scratch_shapes=[pltpu.VMEM((B_GROUPS, rows, B_GROUP_DIM), F32)],
        compiler_params=_cparams(("parallel", "arbitrary")),
        name="dft_out",
    )(gmat, uv.reshape(bsz, RADIX, 2 * n2, B_W), ya.reshape(bsz, seq_len, A_W),
      sgb.reshape(bsz, seq_len, B_W), xs, mod, w_out)


def _mixer_ab_layer(xs, mod, mod_row, ng, w_in, vg, sw, sb, w_out, cs, tl2, tk):
    seq_len = xs.shape[1]
    twc, tws = _twiddle_tables(seq_len)
    gmat = _position_dft_matrix(seq_len)
    gmat = _to_bf16_call(jnp.asarray(gmat), min(256, gmat.shape[0]))
    ya, sgb, uv = _in_ab_call(xs, mod, mod_row, ng, w_in, vg, sw, sb, cs,
                              jnp.asarray(twc), jnp.asarray(tws), tl2)
    return _dft_out_call(gmat, uv, ya, sgb, xs, mod, mod_row, w_out, tk=tk)


def _rope_block(t, cos, sin_signed, lane_lo):
    nf = C_HEAD_DIM // 4
    swapped = jnp.where(lane_lo, pltpu.roll(t, 128 - nf, axis=1), pltpu.roll(t, nf, axis=1))
    return t * cos + swapped * sin_signed


def _store_dup_heads(k_ref, c, t, lane):
    r = pltpu.roll(t, C_HEAD_DIM, axis=1)
    first = lane < C_HEAD_DIM
    k_ref[0, :, (2 * c) * 128:(2 * c + 1) * 128] = jnp.where(first, t, r).astype(BF16)
    k_ref[0, :, (2 * c + 1) * 128:(2 * c + 2) * 128] = jnp.where(first, r, t).astype(BF16)


def _in_c_kernel(x_ref, mod_ref, ng_ref, w_ref, cos_ref, sin_ref, q_ref, k_ref, v_ref, sg_ref):
    h = _modulated_norm(x_ref[0], mod_ref[0], ng_ref[...])
    z = jnp.dot(h.astype(BF16), w_ref[...], preferred_element_type=F32)
    cos = cos_ref[...]
    sin = sin_ref[...]
    lane = lax.broadcasted_iota(jnp.int32, cos.shape, 1)
    lane_lo = (lane % (C_HEAD_DIM // 2)) < (C_HEAD_DIM // 4)
    qscale = C_HEAD_DIM ** -0.5 * LOG2E
    for c in range(C_Q_W // 128):
        t = _rope_block(z[:, c * 128:(c + 1) * 128], cos, sin, lane_lo)
        q_ref[0, :, c * 128:(c + 1) * 128] = (t * qscale).astype(BF16)
    for c in range(C_KV_W // 128):
        t = _rope_block(z[:, C_Q_W + c * 128:C_Q_W + (c + 1) * 128], cos, sin, lane_lo)
        _store_dup_heads(k_ref, c, t, lane)
    v_ref[0] = z[:, C_Q_W + C_KV_W:C_Q_W + 2 * C_KV_W].astype(BF16)
    sg_ref[0] = _silu(z[:, C_Q_W + 2 * C_KV_W:]).astype(BF16)


def _in_c_call(xs, mod, ng, w_in, cos_t, sin_t, tm):
    bsz, seq_len, _ = xs.shape
    const2 = lambda b, i: (0, 0)
    row_blk = lambda w: pl.BlockSpec((1, tm, w), lambda b, i: (b, i, 0))
    return pl.pallas_call(
        _in_c_kernel,
        grid=(bsz, seq_len // tm),
        in_specs=[
            row_blk(D_MODEL),
            pl.BlockSpec((1, 1, 3 * D_MODEL), lambda b, i: (b, 0, 0)),
            pl.BlockSpec((1, D_MODEL), const2),
            pl.BlockSpec((D_MODEL, C_IN), const2),
            pl.BlockSpec((tm, 128), lambda b, i: (i, 0)),
            pl.BlockSpec((tm, 128), lambda b, i: (i, 0)),
        ],
        out_specs=[row_blk(C_Q_W), row_blk(2 * C_KV_W), row_blk(C_KV_W), row_blk(C_Q_W)],
        out_shape=[
            jax.ShapeDtypeStruct((bsz, seq_len, C_Q_W), BF16),
            jax.ShapeDtypeStruct((bsz, seq_len, 2 * C_KV_W), BF16),
            jax.ShapeDtypeStruct((bsz, seq_len, C_KV_W), BF16),
            jax.ShapeDtypeStruct((bsz, seq_len, C_Q_W), BF16),
        ],
        compiler_params=_cparams(("parallel", "arbitrary")),
        name="in_c",
    )(xs, mod, ng, w_in, cos_t, sin_t)


def _ctx_kv_kernel(x_ref, mod_ref, ng_ref, w_ref, k_ref, v_ref):
    h = _modulated_norm(x_ref[0], mod_ref[0], ng_ref[...])
    z = jnp.dot(h.astype(BF16), w_ref[...], preferred_element_type=F32)
    lane = lax.broadcasted_iota(jnp.int32, (z.shape[0], 128), 1)
    for c in range(C_KV_W // 128):
        _store_dup_heads(k_ref, c, z[:, c * 128:(c + 1) * 128], lane)
    v_ref[0] = z[:, C_KV_W:].astype(BF16)


def _ctx_kv_call(ctx, mod, mod_row, ng, w_kv):
    bsz, lc, _ = ctx.shape
    return pl.pallas_call(
        _ctx_kv_kernel,
        grid=(bsz,),
        in_specs=[
            pl.BlockSpec((1, lc, D_MODEL), lambda b: (b, 0, 0)),
            pl.BlockSpec((1, 1, 3 * D_MODEL), lambda b: (mod_row, 0, 0)),
            pl.BlockSpec((1, D_MODEL), lambda b: (0, 0)),
            pl.BlockSpec((D_MODEL, 2 * C_KV_W), lambda b: (0, 0)),
        ],
        out_specs=[pl.BlockSpec((1, lc, 2 * C_KV_W), lambda b: (b, 0, 0)),
                   pl.BlockSpec((1, lc, C_KV_W), lambda b: (b, 0, 0))],
        out_shape=[jax.ShapeDtypeStruct((bsz, lc, 2 * C_KV_W), BF16),
                   jax.ShapeDtypeStruct((bsz, lc, C_KV_W), BF16)],
        compiler_params=_cparams(("parallel",)),
        name="ctx_kv",
    )(ctx, mod, ng, w_kv)


def _attn_out_kernel(sink_ref, q_ref, kp_ref, kc_ref, kn_ref, vp_ref, vc_ref, vn_ref, kx_ref, vx_ref,
                     sg_ref, x_ref, mod_ref, w_ref, fg_ref, o_ref):
    n = pl.program_id(1)
    last = pl.num_programs(1) - 1
    gq = C_GROUP * Q_BLOCK
    kj = lax.broadcasted_iota(jnp.int32, (Q_BLOCK, Q_BLOCK), 0)
    qi = lax.broadcasted_iota(jnp.int32, (Q_BLOCK, Q_BLOCK), 1)
    bias_before = jnp.where(kj >= qi, 0.0, NEG_INF).astype(F32)
    bias_after = jnp.where(kj <= qi, 0.0, NEG_INF).astype(F32)
    edge_first = jnp.where(n > 0, 0.0, NEG_INF).astype(F32)
    edge_last = jnp.where(n < last, 0.0, NEG_INF).astype(F32)
    lane = lax.broadcasted_iota(jnp.int32, (Q_BLOCK, 128), 1)
    first = lane < C_HEAD_DIM
    zero = jnp.zeros((Q_BLOCK, 128), BF16)
    hi = slice(Q_BLOCK, 2 * Q_BLOCK)
    lo = slice(0, Q_BLOCK)
    lane_k = lax.broadcasted_iota(jnp.int32, (kx_ref.shape[1] + 3 * Q_BLOCK, 128), 1)

    def scores(qb, kh):
        kl = slice(kh * 128, (kh + 1) * 128)
        if qb == 0:
            kblocks = [kx_ref[0, :, kl], kp_ref[0, :, kl], kc_ref[0, lo, kl], kc_ref[0, hi, kl]]
            biases = (None, (bias_before, edge_first), None, (bias_after, None))
        else:
            kblocks = [kx_ref[0, :, kl], kc_ref[0, lo, kl], kc_ref[0, hi, kl], kn_ref[0, :, kl]]
            biases = (None, (bias_before, None), None, (bias_after, edge_last))
        rows_q = slice(qb * Q_BLOCK, (qb + 1) * Q_BLOCK)
        q4 = []
        for c in range(2):
            qv = q_ref[0, rows_q, kh * 256 + c * 128:kh * 256 + (c + 1) * 128]
            q4 += [jnp.where(first, qv, zero), jnp.where(first, zero, qv)]
        q4 = jnp.concatenate(q4, axis=0)
        k2 = jnp.concatenate(kblocks, axis=0)
        st = lax.dot_general(k2, q4, (((1,), (1,)), ((), ())), preferred_element_type=F32)
        blocks, m8, r = [], None, 0
        for kb, bias in zip(kblocks, biases):
            sb = st[r:r + kb.shape[0]]
            r += kb.shape[0]
            if bias is not None:
                mask_bias, edge = bias
                cols = [sb[:, j * Q_BLOCK:(j + 1) * Q_BLOCK] + mask_bias for j in range(C_GROUP)]
                sb = jnp.concatenate(cols, axis=1)
                if edge is not None:
                    sb = sb + edge
            blocks.append(sb)
            mb = jnp.max(sb.reshape(sb.shape[0] // 8, 8, gq), axis=0)
            m8 = mb if m8 is None else jnp.maximum(m8, mb)
        return blocks, m8

    def softmax_pv(qb, kh, blocks, m8):
        sink2 = sink_ref[kh] * LOG2E
        m = jnp.maximum(jnp.max(m8, axis=0, keepdims=True), sink2)
        pts = []
        for blk in blocks:
            for r in range(0, blk.shape[0], SOFTMAX_ROWS):
                pts.append(jnp.exp2((blk[r:r + SOFTMAX_ROWS] - m).astype(BF16)))
        pt = jnp.concatenate(pts, axis=0)
        vl = slice((kh // 2) * 128, (kh // 2 + 1) * 128)
        if qb == 0:
            vblocks = [vx_ref[0, :, vl], vp_ref[0, :, vl], vc_ref[0, :, vl]]
        else:
            vblocks = [vx_ref[0, :, vl], vc_ref[0, :, vl], vn_ref[0, :, vl]]
        vpair = jnp.concatenate(vblocks, axis=0)
        own = (lane_k < C_HEAD_DIM) if kh % 2 == 0 else (lane_k >= C_HEAD_DIM)
        vsum = jnp.where(own, vpair, jnp.ones_like(vpair))
        ot = lax.dot_general(vsum, pt, (((0,), (0,)), ((), ())), preferred_element_type=F32)
        r0 = (kh % 2) * C_HEAD_DIM
        r1 = C_HEAD_DIM - r0
        denom = ot[r1:r1 + C_HEAD_DIM] + jnp.exp2(sink2 - m)
        ot = ot[r0:r0 + C_HEAD_DIM] / denom
        cols = []
        for c in range(2):
            pair = jnp.concatenate([ot[:, (2 * c) * Q_BLOCK:(2 * c + 1) * Q_BLOCK],
                                    ot[:, (2 * c + 1) * Q_BLOCK:(2 * c + 2) * Q_BLOCK]], axis=0)
            cols.append(pair.T)
        return cols

    chains = [(qb, kh) for qb in range(2) for kh in range(C_KV_HEADS)]
    o_cols = {0: [], 1: []}
    pending = [scores(*ch) for ch in chains[:SCORE_LOOKAHEAD]]
    for idx, (qb, kh) in enumerate(chains):
        if idx + SCORE_LOOKAHEAD < len(chains):
            pending.append(scores(*chains[idx + SCORE_LOOKAHEAD]))
        o_cols[qb] += softmax_pv(qb, kh, *pending.pop(0))
    o_all = jnp.concatenate([jnp.concatenate(o_cols[0], axis=1),
                             jnp.concatenate(o_cols[1], axis=1)], axis=0)
    y = jnp.dot((o_all * sg_ref[0].astype(F32)).astype(BF16), w_ref[...], preferred_element_type=F32)
    gate = mod_ref[0][:, 2 * D_MODEL:]
    x2 = x_ref[0] + gate * y
    ms = jnp.mean(x2 * x2, axis=-1, keepdims=True)
    o_ref[0] = x2 * lax.rsqrt(ms + NORM_EPS) * fg_ref[...]


def _attn_out_call(sink, q, k, v, kx, vx, sg, xs, mod, w_out, fg):
    bsz, seq_len, _ = xs.shape
    nblk = seq_len // Q_BLOCK
    nstep = nblk // 2
    lc = kx.shape[1]
    assert lc % SOFTMAX_ROWS == 0 and Q_BLOCK % SOFTMAX_ROWS == 0
    cur = lambda w: pl.BlockSpec((1, 2 * Q_BLOCK, w), lambda b, n: (b, n, 0))
    prev = lambda w: pl.BlockSpec((1, Q_BLOCK, w), lambda b, n: (b, jnp.maximum(2 * n - 1, 0), 0))
    nxt = lambda w: pl.BlockSpec((1, Q_BLOCK, w), lambda b, n: (b, jnp.minimum(2 * n + 2, nblk - 1), 0))
    ctx_blk = lambda w: pl.BlockSpec((1, lc, w), lambda b, n: (b, 0, 0))
    kw, vw = 2 * C_KV_W, C_KV_W
    sink_t = jnp.repeat(sink.reshape(C_KV_HEADS, 1, C_GROUP), Q_BLOCK, axis=2)
    return pl.pallas_call(
        _attn_out_kernel,
        grid=(bsz, nstep),
        in_specs=[
            pl.BlockSpec((C_KV_HEADS, 1, C_GROUP * Q_BLOCK), lambda b, n: (0, 0, 0)),
            cur(C_Q_W), prev(kw), cur(kw), nxt(kw), prev(vw), cur(vw), nxt(vw), ctx_blk(kw), ctx_blk(vw),
            cur(C_Q_W), cur(D_MODEL),
            pl.BlockSpec((1, 1, 3 * D_MODEL), lambda b, n: (b, 0, 0)),
            pl.BlockSpec((C_Q_W, D_MODEL), lambda b, n: (0, 0)),
            pl.BlockSpec((1, D_MODEL), lambda b, n: (0, 0)),
        ],
        out_specs=cur(D_MODEL),
        out_shape=jax.ShapeDtypeStruct((bsz, seq_len, D_MODEL), F32),
        compiler_params=_cparams(("parallel", "arbitrary")),
        name="attn_out",
    )(sink_t, q, k, k, k, v, v, v, kx, vx, sg, xs, mod, w_out, fg)


def kernel(x, c, ctx, c_ctx, norm_g, ada_w, ada_b, w_in_ab, v_norm_g, spatial_w, spatial_b, w_out_ab,
           w_in_c, sink_logit, w_out_c, final_g):
    bsz, seq_len, _ = x.shape
    depth = ada_w.shape[0]
    assert depth == 2 and bsz + 1 <= MOD_ROWS
    ctx_row = bsz

    cc = jnp.concatenate([c, c_ctx[None, :], jnp.zeros((MOD_ROWS - bsz - 1, D_MODEL), F32)], axis=0)
    mod = _mod_call(cc, ada_w, ada_b)
    mod0 = mod[0].reshape(MOD_ROWS, 1, 3 * D_MODEL)
    mod1 = mod[1].reshape(MOD_ROWS, 1, 3 * D_MODEL)

    cs = jnp.asarray(_channel_dft_matrix())
    ng0 = norm_g[0].reshape(1, D_MODEL)
    ng1 = norm_g[1].reshape(1, D_MODEL)
    w_in0 = w_in_ab[0].astype(BF16)
    w_out0 = w_out_ab[0].astype(BF16)
    vg = v_norm_g[0].reshape(1, A_W)
    sw = spatial_w[0].astype(BF16)
    sb = jnp.broadcast_to(spatial_b[0][:, :, None], (A_HEADS, CHUNK, A_HEAD_DIM))

    x1 = _mixer_ab_layer(x, mod0, None, ng0, w_in0, vg, sw, sb, w_out0, cs, tl2=256, tk=256)
    ctx1 = _mixer_ab_layer(ctx, mod0, ctx_row, ng0, w_in0, vg, sw, sb, w_out0, cs,
                           tl2=ctx.shape[1] // RADIX, tk=ctx.shape[1] // RADIX)

    w_in1 = w_in_c[0].astype(BF16)
    w_out1 = w_out_c[0].astype(BF16)
    cos_t, sin_t = _rope_tables(seq_len)
    q, k, v, sg = _in_c_call(x1, mod1, ng1, w_in1, jnp.asarray(cos_t), jnp.asarray(sin_t), tm=1024)
    kx, vx = _ctx_kv_call(ctx1, mod1, ctx_row, ng1, w_in1[:, C_Q_W:C_Q_W + 2 * C_KV_W])
    return _attn_out_call(sink_logit[0], q, k, v, kx, vx, sg, x1, mod1, w_out1,
                          final_g.reshape(1, D_MODEL))
```

```python
import functools
import math

import numpy as np
import jax
import jax.numpy as jnp
from jax import lax
from jax.experimental import pallas as pl
from jax.experimental.pallas import tpu as pltpu

F32 = jnp.float32
BF16 = jnp.bfloat16

D_MODEL = 1024
GRID_W = 64
CHUNK = 128
A_HEADS = 4
A_HEAD_DIM = 128
A_W = A_HEADS * A_HEAD_DIM
B_GROUPS = 4
B_GROUP_DIM = 128
B_W = B_GROUPS * B_GROUP_DIM
AB_IN = 3 * A_W + 2 * B_W
C_HEADS = 16
C_KV_HEADS = 4
C_GROUP = C_HEADS // C_KV_HEADS
C_HEAD_DIM = 64
C_Q_W = C_HEADS * C_HEAD_DIM
C_KV_W = C_KV_HEADS * C_HEAD_DIM
C_IN = 2 * C_Q_W + 2 * C_KV_W
WINDOW = 128
Q_BLOCK = 128
ROPE_BASE = 10000.0
NORM_EPS = 1e-6
NEG_INF = -1e30
LOG2E = math.log2(math.e)
RADIX = 4
MOD_ROWS = 16
SOFTMAX_ROWS = 64
SCORE_LOOKAHEAD = 2
ATTN_Q_BLOCKS = 4
OUT_PROJ_BLOCKS = 2
V7X_VMEM_LIMIT = 56 * 1024 * 1024


def _silu(x):
    return x * (1.0 / (1.0 + jnp.exp(-x)))


def _cparams(sem):
    return pltpu.CompilerParams(dimension_semantics=sem, vmem_limit_bytes=V7X_VMEM_LIMIT)


def _channel_dft_matrix():
    n = np.arange(B_GROUP_DIM)
    ang = 2.0 * np.pi * np.outer(n, n) / B_GROUP_DIM
    return np.concatenate([np.cos(ang), np.sin(ang)], axis=1).astype(np.float32)


def _position_dft_matrix(seq_len):
    n2 = seq_len // RADIX
    idx = np.arange(n2)
    ang = 2.0 * np.pi * (np.outer(idx, idx) % n2) / n2
    norm = 1.0 / math.sqrt(seq_len * B_GROUP_DIM)
    return np.concatenate([np.cos(ang) * norm, -np.sin(ang) * norm], axis=1).astype(np.float32)


def _twiddle_tables(seq_len):
    n2 = seq_len // RADIX
    l2 = np.arange(n2)[None, :, None]
    k1 = np.arange(RADIX)[:, None, None]
    ang = 2.0 * np.pi * ((l2 * k1) % seq_len) / seq_len
    ang = np.broadcast_to(ang, (RADIX, n2, 128))
    return np.cos(ang).astype(np.float32), np.sin(ang).astype(np.float32)


def _rope_tables(seq_len):
    t = np.arange(seq_len)
    row = (t // GRID_W).astype(np.float64)
    col = (t % GRID_W).astype(np.float64)
    lane = np.arange(128)
    dd = lane % C_HEAD_DIM
    nf = C_HEAD_DIM // 4
    inv = ROPE_BASE ** (-(dd % nf).astype(np.float64) / nf)
    pos = np.where((dd < C_HEAD_DIM // 2)[None, :], row[:, None], col[:, None])
    ang = pos * inv[None, :]
    sign = np.where((dd % (2 * nf)) < nf, -1.0, 1.0)[None, :]
    return np.cos(ang).astype(np.float32), (np.sin(ang) * sign).astype(np.float32)


def _mod_kernel(c_ref, w_ref, b_ref, o_ref):
    s = _silu(c_ref[...]).astype(BF16)
    o_ref[0] = jnp.dot(s, w_ref[0].astype(BF16), preferred_element_type=F32) + b_ref[0]


def _mod_call(cc, ada_w, ada_b):
    depth = ada_w.shape[0]
    tn = 1024
    return pl.pallas_call(
        _mod_kernel,
        grid=(depth, 3 * D_MODEL // tn),
        in_specs=[
            pl.BlockSpec((MOD_ROWS, D_MODEL), lambda l, j: (0, 0)),
            pl.BlockSpec((1, D_MODEL, tn), lambda l, j: (l, 0, j)),
            pl.BlockSpec((1, 1, tn), lambda l, j: (l, 0, j)),
        ],
        out_specs=pl.BlockSpec((1, MOD_ROWS, tn), lambda l, j: (l, 0, j)),
        out_shape=jax.ShapeDtypeStruct((depth, MOD_ROWS, 3 * D_MODEL), F32),
        compiler_params=_cparams(("arbitrary", "arbitrary")),
        name="adaln_mod",
    )(cc, ada_w, ada_b.reshape(depth, 1, 3 * D_MODEL))


def _modulated_norm(x, mod_row, g):
    shift = mod_row[:, :D_MODEL]
    scale = mod_row[:, D_MODEL:2 * D_MODEL]
    ms = jnp.mean(x * x, axis=-1, keepdims=True)
    h = x * lax.rsqrt(ms + NORM_EPS) * g
    return h * (1.0 + scale) + shift


def _in_ab_kernel(x_ref, mod_ref, ng_ref, w_ref, vg_ref, sw_ref, sb_ref, cs_ref, twc_ref, tws_ref,
                  ya_ref, sgb_ref, uv_ref, *, tl2):
    rows = RADIX * tl2
    x = x_ref[0].reshape(rows, D_MODEL)
    h = _modulated_norm(x, mod_ref[0], ng_ref[...])
    z = jnp.dot(h.astype(BF16), w_ref[...], preferred_element_type=F32)

    v = z[:, A_W:2 * A_W]
    mu = jnp.mean(v, axis=-1, keepdims=True)
    vc = v - mu
    var = jnp.mean(vc * vc, axis=-1, keepdims=True)
    vn = (vc * lax.rsqrt(var + NORM_EPS) * vg_ref[...]).astype(BF16)

    piece = min(CHUNK, tl2)
    nchunk = rows // CHUNK
    assert nchunk % 2 == 0
    for hd in range(A_HEADS):
        c0 = hd * A_HEAD_DIM
        for cp in range(nchunk // 2):
            ra, rb = 2 * cp * CHUNK, (2 * cp + 1) * CHUNK
            vpair = jnp.concatenate([vn[ra:ra + CHUNK, c0:c0 + A_HEAD_DIM],
                                     vn[rb:rb + CHUNK, c0:c0 + A_HEAD_DIM]], axis=1)
            sv2 = jnp.dot(sw_ref[hd], vpair, preferred_element_type=F32)
            for half, r0 in enumerate((ra, rb)):
                sv = sv2[:, half * A_HEAD_DIM:(half + 1) * A_HEAD_DIM] + sb_ref[hd]
                u = z[r0:r0 + CHUNK, c0:c0 + A_HEAD_DIM]
                ga = z[r0:r0 + CHUNK, 2 * A_W + c0:2 * A_W + c0 + A_HEAD_DIM]
                ya = (u * sv * _silu(ga)).astype(BF16)
                for p0 in range(0, CHUNK, piece):
                    l1, off = divmod(r0 + p0, tl2)
                    ya_ref[0, l1, off:off + piece, c0:c0 + A_HEAD_DIM] = ya[p0:p0 + piece]

    gb = z[:, 3 * A_W + B_W:]
    sgb_ref[0] = _silu(gb).astype(BF16).reshape(RADIX, tl2, B_W)

    xb = z[:, 3 * A_W:3 * A_W + B_W].astype(BF16)
    cs = cs_ref[...].astype(BF16)
    for g in range(B_GROUPS):
        c0 = g * B_GROUP_DIM
        ps, qs = [], []
        for j in range(RADIX):
            pq = jnp.dot(xb[j * tl2:(j + 1) * tl2, c0:c0 + B_GROUP_DIM], cs,
                         preferred_element_type=F32)
            ps.append(pq[:, :B_GROUP_DIM])
            qs.append(pq[:, B_GROUP_DIM:])
        p02, p13 = ps[0] - ps[2], ps[1] - ps[3]
        q02, q13 = qs[0] - qs[2], qs[1] - qs[3]
        pe, po = ps[0] + ps[2], ps[1] + ps[3]
        qe, qo = qs[0] + qs[2], qs[1] + qs[3]
        us = [pe + po, p02 - q13, pe - po, p02 + q13]
        vs = [qe + qo, q02 + p13, qe - qo, q02 - p13]
        for k1 in range(RADIX):
            if k1 == 0:
                ut, vt = us[0], vs[0]
            else:
                tc, ts = twc_ref[k1], tws_ref[k1]
                ut = us[k1] * tc - vs[k1] * ts
                vt = us[k1] * ts + vs[k1] * tc
            uv_ref[0, k1, 0, :, c0:c0 + B_GROUP_DIM] = ut.astype(BF16)
            uv_ref[0, k1, 1, :, c0:c0 + B_GROUP_DIM] = vt.astype(BF16)


def _in_ab_call(xs, mod, mod_row, ng, w_in, vg, sw, sb, cs, twc, tws, tl2):
    bsz, seq_len, _ = xs.shape
    n2 = seq_len // RADIX
    x4 = xs.reshape(bsz, RADIX, n2, D_MODEL)
    row_of = (lambda b: b) if mod_row is None else (lambda b: mod_row)
    const2 = lambda b, i: (0, 0)
    const3 = lambda b, i: (0, 0, 0)
    return pl.pallas_call(
        functools.partial(_in_ab_kernel, tl2=tl2),
        grid=(bsz, n2 // tl2),
        in_specs=[
            pl.BlockSpec((1, RADIX, tl2, D_MODEL), lambda b, i: (b, 0, i, 0)),
            pl.BlockSpec((1, 1, 3 * D_MODEL), lambda b, i: (row_of(b), 0, 0)),
            pl.BlockSpec((1, D_MODEL), const2),
            pl.BlockSpec((D_MODEL, AB_IN), const2),
            pl.BlockSpec((1, A_W), const2),
            pl.BlockSpec((A_HEADS, CHUNK, CHUNK), const3),
            pl.BlockSpec((A_HEADS, CHUNK, A_HEAD_DIM), const3),
            pl.BlockSpec((B_GROUP_DIM, 2 * B_GROUP_DIM), const2),
            pl.BlockSpec((RADIX, tl2, 128), lambda b, i: (0, i, 0)),
            pl.BlockSpec((RADIX, tl2, 128), lambda b, i: (0, i, 0)),
        ],
        out_specs=[
            pl.BlockSpec((1, RADIX, tl2, A_W), lambda b, i: (b, 0, i, 0)),
            pl.BlockSpec((1, RADIX, tl2, B_W), lambda b, i: (b, 0, i, 0)),
            pl.BlockSpec((1, RADIX, 2, tl2, B_W), lambda b, i: (b, 0, 0, i, 0)),
        ],
        out_shape=[
            jax.ShapeDtypeStruct((bsz, RADIX, n2, A_W), BF16),
            jax.ShapeDtypeStruct((bsz, RADIX, n2, B_W), BF16),
            jax.ShapeDtypeStruct((bsz, RADIX, 2, n2, B_W), BF16),
        ],
        compiler_params=_cparams(("parallel", "arbitrary")),
        name="in_ab",
    )(x4, mod, ng, w_in, vg, sw, sb, cs, twc, tws)


def _dft_out_kernel(g_ref, uv_ref, ya_ref, sgb_ref, x_ref, mod_ref, w_ref, o_ref, f_ref, *, tk):
    gmat = g_ref[...].astype(BF16)
    for k1 in range(RADIX):
        f = jnp.dot(gmat, uv_ref[0, k1], preferred_element_type=F32)
        for g in range(B_GROUPS):
            f_ref[g, pl.ds(k1, tk, stride=RADIX), :] = f[:, g * B_GROUP_DIM:(g + 1) * B_GROUP_DIM]
    fnat = jnp.concatenate([f_ref[g] for g in range(B_GROUPS)], axis=1)
    yb = (fnat * sgb_ref[0].astype(F32)).astype(BF16)
    y = jnp.dot(ya_ref[0], w_ref[:A_W], preferred_element_type=F32)
    y = y + jnp.dot(yb, w_ref[A_W:], preferred_element_type=F32)
    gate = mod_ref[0][:, 2 * D_MODEL:]
    o_ref[0] = x_ref[0] + gate * y


def _dft_out_call(gmat, uv, ya, sgb, xs, mod, mod_row, w_out, tk):
    bsz, seq_len, _ = xs.shape
    n2 = seq_len // RADIX
    rows = RADIX * tk
    row_of = (lambda b: b) if mod_row is None else (lambda b: mod_row)
    nat = lambda w: pl.BlockSpec((1, rows, w), lambda b, i: (b, i, 0))
    return pl.pallas_call(
        functools.partial(_dft_out_kernel, tk=tk),
        grid=(bsz, n2 // tk),
        in_specs=[
            pl.BlockSpec((tk, 2 * n2), lambda b, i: (i, 0)),
            pl.BlockSpec((1, RADIX, 2 * n2, B_W), lambda b, i: (b, 0, 0, 0)),
            nat(A_W), nat(B_W), nat(D_MODEL),
            pl.BlockSpec((1, 1, 3 * D_MODEL), lambda b, i: (row_of(b), 0, 0)),
            pl.BlockSpec((A_W + B_W, D_MODEL), lambda b, i: (0, 0), pipeline_mode=pl.Buffered(1)),
        ],
        out_specs=nat(D_MODEL),
        out_shape=jax.ShapeDtypeStruct((bsz, seq_len, D_MODEL), F32),
        scratch_shapes=[pltpu.VMEM((B_GROUPS, rows, B_GROUP_DIM), F32)],
        compiler_params=_cparams(("parallel", "arbitrary")),
        name="dft_out",
    )(gmat, uv.reshape(bsz, RADIX, 2 * n2, B_W), ya.reshape(bsz, seq_len, A_W),
      sgb.reshape(bsz, seq_len, B_W), xs, mod, w_out)


def _mixer_ab_layer(xs, mod, mod_row, ng, w_in, vg, sw, sb, w_out, cs, tl2, tk):
    seq_len = xs.shape[1]
    twc, tws = _twiddle_tables(seq_len)
    gmat = jnp.asarray(_position_dft_matrix(seq_len))
    ya, sgb, uv = _in_ab_call(xs, mod, mod_row, ng, w_in, vg, sw, sb, cs,
                              jnp.asarray(twc), jnp.asarray(tws), tl2)
    return _dft_out_call(gmat, uv, ya, sgb, xs, mod, mod_row, w_out, tk=tk)


def _rope_block(t, cos, sin_signed, lane_lo):
    nf = C_HEAD_DIM // 4
    swapped = jnp.where(lane_lo, pltpu.roll(t, 128 - nf, axis=1), pltpu.roll(t, nf, axis=1))
    return t * cos + swapped * sin_signed


def _store_dup_heads(k_ref, c, t, lane):
    r = pltpu.roll(t, C_HEAD_DIM, axis=1)
    first = lane < C_HEAD_DIM
    k_ref[0, :, (2 * c) * 128:(2 * c + 1) * 128] = jnp.where(first, t, r).astype(BF16)
    k_ref[0, :, (2 * c + 1) * 128:(2 * c + 2) * 128] = jnp.where(first, r, t).astype(BF16)


def _in_c_kernel(x_ref, mod_ref, ng_ref, w_ref, cos_ref, sin_ref, q_ref, k_ref, v_ref, sg_ref):
    h = _modulated_norm(x_ref[0], mod_ref[0], ng_ref[...])
    z = jnp.dot(h.astype(BF16), w_ref[...], preferred_element_type=F32)
    cos = cos_ref[...]
    sin = sin_ref[...]
    lane = lax.broadcasted_iota(jnp.int32, cos.shape, 1)
    lane_lo = (lane % (C_HEAD_DIM // 2)) < (C_HEAD_DIM // 4)
    qscale = C_HEAD_DIM ** -0.5 * LOG2E
    for c in range(C_Q_W // 128):
        t = _rope_block(z[:, c * 128:(c + 1) * 128], cos, sin, lane_lo)
        q_ref[0, :, c * 128:(c + 1) * 128] = (t * qscale).astype(BF16)
    for c in range(C_KV_W // 128):
        t = _rope_block(z[:, C_Q_W + c * 128:C_Q_W + (c + 1) * 128], cos, sin, lane_lo)
        _store_dup_heads(k_ref, c, t, lane)
    v_ref[0] = z[:, C_Q_W + C_KV_W:C_Q_W + 2 * C_KV_W].astype(BF16)
    sg_ref[0] = _silu(z[:, C_Q_W + 2 * C_KV_W:]).astype(BF16)


def _in_c_call(xs, mod, ng, w_in, cos_t, sin_t, tm):
    bsz, seq_len, _ = xs.shape
    const2 = lambda b, i: (0, 0)
    row_blk = lambda w: pl.BlockSpec((1, tm, w), lambda b, i: (b, i, 0))
    return pl.pallas_call(
        _in_c_kernel,
        grid=(bsz, seq_len // tm),
        in_specs=[
            row_blk(D_MODEL),
            pl.BlockSpec((1, 1, 3 * D_MODEL), lambda b, i: (b, 0, 0)),
            pl.BlockSpec((1, D_MODEL), const2),
            pl.BlockSpec((D_MODEL, C_IN), const2),
            pl.BlockSpec((tm, 128), lambda b, i: (i, 0)),
            pl.BlockSpec((tm, 128), lambda b, i: (i, 0)),
        ],
        out_specs=[row_blk(C_Q_W), row_blk(2 * C_KV_W), row_blk(C_KV_W), row_blk(C_Q_W)],
        out_shape=[
            jax.ShapeDtypeStruct((bsz, seq_len, C_Q_W), BF16),
            jax.ShapeDtypeStruct((bsz, seq_len, 2 * C_KV_W), BF16),
            jax.ShapeDtypeStruct((bsz, seq_len, C_KV_W), BF16),
            jax.ShapeDtypeStruct((bsz, seq_len, C_Q_W), BF16),
        ],
        compiler_params=_cparams(("parallel", "arbitrary")),
        name="in_c",
    )(xs, mod, ng, w_in, cos_t, sin_t)


def _ctx_kv_kernel(x_ref, mod_ref, ng_ref, w_ref, k_ref, v_ref):
    h = _modulated_norm(x_ref[0], mod_ref[0], ng_ref[...])
    z = jnp.dot(h.astype(BF16), w_ref[...], preferred_element_type=F32)
    lane = lax.broadcasted_iota(jnp.int32, (z.shape[0], 128), 1)
    for c in range(C_KV_W // 128):
        _store_dup_heads(k_ref, c, z[:, c * 128:(c + 1) * 128], lane)
    v_ref[0] = z[:, C_KV_W:].astype(BF16)


def _ctx_kv_call(ctx, mod, mod_row, ng, w_kv):
    bsz, lc, _ = ctx.shape
    return pl.pallas_call(
        _ctx_kv_kernel,
        grid=(bsz,),
        in_specs=[
            pl.BlockSpec((1, lc, D_MODEL), lambda b: (b, 0, 0)),
            pl.BlockSpec((1, 1, 3 * D_MODEL), lambda b: (mod_row, 0, 0)),
            pl.BlockSpec((1, D_MODEL), lambda b: (0, 0)),
            pl.BlockSpec((D_MODEL, 2 * C_KV_W), lambda b: (0, 0)),
        ],
        out_specs=[pl.BlockSpec((1, lc, 2 * C_KV_W), lambda b: (b, 0, 0)),
                   pl.BlockSpec((1, lc, C_KV_W), lambda b: (b, 0, 0))],
        out_shape=[jax.ShapeDtypeStruct((bsz, lc, 2 * C_KV_W), BF16),
                   jax.ShapeDtypeStruct((bsz, lc, C_KV_W), BF16)],
        compiler_params=_cparams(("parallel",)),
        name="ctx_kv",
    )(ctx, mod, ng, w_kv)


def _attn_out_kernel(sink_ref, q_ref, kp_ref, kc_ref, kn_ref, vp_ref, vc_ref, vn_ref, kx_ref, vx_ref,
                     sg_ref, x_ref, mod_ref, w_ref, fg_ref, o_ref):
    n = pl.program_id(1)
    last = pl.num_programs(1) - 1
    gq = C_GROUP * Q_BLOCK
    kj = lax.broadcasted_iota(jnp.int32, (Q_BLOCK, Q_BLOCK), 0)
    qi = lax.broadcasted_iota(jnp.int32, (Q_BLOCK, Q_BLOCK), 1)
    bias_before = jnp.where(kj >= qi, 0.0, NEG_INF).astype(F32)
    bias_after = jnp.where(kj <= qi, 0.0, NEG_INF).astype(F32)
    edge_first = jnp.where(n > 0, 0.0, NEG_INF).astype(F32)
    edge_last = jnp.where(n < last, 0.0, NEG_INF).astype(F32)
    lane = lax.broadcasted_iota(jnp.int32, (Q_BLOCK, 128), 1)
    first = lane < C_HEAD_DIM
    zero = jnp.zeros((Q_BLOCK, 128), BF16)
    lane_k = lax.broadcasted_iota(jnp.int32, (kx_ref.shape[1] + 3 * Q_BLOCK, 128), 1)
    nq = q_ref.shape[1] // Q_BLOCK

    def window(prev_ref, cur_ref, next_ref, j, lanes):
        if j < 0:
            return prev_ref[0, :, lanes]
        if j >= nq:
            return next_ref[0, :, lanes]
        return cur_ref[0, j * Q_BLOCK:(j + 1) * Q_BLOCK, lanes]

    def scores(qb, kh):
        kl = slice(kh * 128, (kh + 1) * 128)
        kblocks = [kx_ref[0, :, kl]] + [window(kp_ref, kc_ref, kn_ref, j, kl) for j in (qb - 1, qb, qb + 1)]
        biases = (None, (bias_before, edge_first if qb == 0 else None), None,
                  (bias_after, edge_last if qb == nq - 1 else None))
        rows_q = slice(qb * Q_BLOCK, (qb + 1) * Q_BLOCK)
        q4 = []
        for c in range(2):
            qv = q_ref[0, rows_q, kh * 256 + c * 128:kh * 256 + (c + 1) * 128]
            q4 += [jnp.where(first, qv, zero), jnp.where(first, zero, qv)]
        q4 = jnp.concatenate(q4, axis=0)
        k2 = jnp.concatenate(kblocks, axis=0)
        st = lax.dot_general(k2, q4, (((1,), (1,)), ((), ())), preferred_element_type=F32)
        blocks, m8, r = [], None, 0
        for kb, bias in zip(kblocks, biases):
            sb = st[r:r + kb.shape[0]]
            r += kb.shape[0]
            if bias is not None:
                mask_bias, edge = bias
                cols = [sb[:, j * Q_BLOCK:(j + 1) * Q_BLOCK] + mask_bias for j in range(C_GROUP)]
                sb = jnp.concatenate(cols, axis=1)
                if edge is not None:
                    sb = sb + edge
            blocks.append(sb)
            mb = jnp.max(sb.reshape(sb.shape[0] // 8, 8, gq), axis=0)
            m8 = mb if m8 is None else jnp.maximum(m8, mb)
        return blocks, m8

    def softmax_pv(qb, kh, blocks, m8):
        sink2 = sink_ref[kh] * LOG2E
        m = jnp.maximum(jnp.max(m8, axis=0, keepdims=True), sink2)
        pts = []
        for blk in blocks:
            for r in range(0, blk.shape[0], SOFTMAX_ROWS):
                pts.append(jnp.exp2(blk[r:r + SOFTMAX_ROWS] - m).astype(BF16))
        pt = jnp.concatenate(pts, axis=0)
        vl = slice((kh // 2) * 128, (kh // 2 + 1) * 128)
        vblocks = [vx_ref[0, :, vl]] + [window(vp_ref, vc_ref, vn_ref, j, vl) for j in (qb - 1, qb, qb + 1)]
        vpair = jnp.concatenate(vblocks, axis=0)
        own = (lane_k < C_HEAD_DIM) if kh % 2 == 0 else (lane_k >= C_HEAD_DIM)
        vsum = jnp.where(own, vpair, jnp.ones_like(vpair))
        ot = lax.dot_general(vsum, pt, (((0,), (0,)), ((), ())), preferred_element_type=F32)
        r0 = (kh % 2) * C_HEAD_DIM
        r1 = C_HEAD_DIM - r0
        denom = ot[r1:r1 + C_HEAD_DIM] + jnp.exp2(sink2 - m)
        ot = ot[r0:r0 + C_HEAD_DIM] / denom
        cols = []
        for c in range(2):
            pair = jnp.concatenate([ot[:, (2 * c) * Q_BLOCK:(2 * c + 1) * Q_BLOCK],
                                    ot[:, (2 * c + 1) * Q_BLOCK:(2 * c + 2) * Q_BLOCK]], axis=0)
            cols.append(pair.T)
        return cols

    chains = [(qb, kh) for qb in range(nq) for kh in range(C_KV_HEADS)]
    o_cols = {qb: [] for qb in range(nq)}
    gate = mod_ref[0][:, 2 * D_MODEL:]

    def finish(qb0, qb1):
        rows = slice(qb0 * Q_BLOCK, qb1 * Q_BLOCK)
        o_all = jnp.concatenate([jnp.concatenate(o_cols[qb], axis=1) for qb in range(qb0, qb1)], axis=0)
        y = jnp.dot((o_all * sg_ref[0, rows, :].astype(F32)).astype(BF16), w_ref[...],
                    preferred_element_type=F32)
        x2 = x_ref[0, rows, :] + gate * y
        ms = jnp.mean(x2 * x2, axis=-1, keepdims=True)
        o_ref[0, rows, :] = x2 * lax.rsqrt(ms + NORM_EPS) * fg_ref[...]

    pending = [scores(*ch) for ch in chains[:SCORE_LOOKAHEAD]]
    for idx, (qb, kh) in enumerate(chains):
        if idx + SCORE_LOOKAHEAD < len(chains):
            pending.append(scores(*chains[idx + SCORE_LOOKAHEAD]))
        o_cols[qb] += softmax_pv(qb, kh, *pending.pop(0))
        if kh == C_KV_HEADS - 1 and qb % OUT_PROJ_BLOCKS == OUT_PROJ_BLOCKS - 1:
            finish(qb + 1 - OUT_PROJ_BLOCKS, qb + 1)


def _attn_out_call(sink, q, k, v, kx, vx, sg, xs, mod, w_out, fg):
    bsz, seq_len, _ = xs.shape
    nblk = seq_len // Q_BLOCK
    nq = ATTN_Q_BLOCKS
    nstep = nblk // nq
    lc = kx.shape[1]
    assert lc % SOFTMAX_ROWS == 0 and Q_BLOCK % SOFTMAX_ROWS == 0 and nq % OUT_PROJ_BLOCKS == 0
    cur = lambda w: pl.BlockSpec((1, nq * Q_BLOCK, w), lambda b, n: (b, n, 0))
    prev = lambda w: pl.BlockSpec((1, Q_BLOCK, w), lambda b, n: (b, jnp.maximum(nq * n - 1, 0), 0))
    nxt = lambda w: pl.BlockSpec((1, Q_BLOCK, w), lambda b, n: (b, jnp.minimum(nq * n + nq, nblk - 1), 0))
    ctx_blk = lambda w: pl.BlockSpec((1, lc, w), lambda b, n: (b, 0, 0))
    kw, vw = 2 * C_KV_W, C_KV_W
    sink_t = jnp.repeat(sink.reshape(C_KV_HEADS, 1, C_GROUP), Q_BLOCK, axis=2)
    return pl.pallas_call(
        _attn_out_kernel,
        grid=(bsz, nstep),
        in_specs=[
            pl.BlockSpec((C_KV_HEADS, 1, C_GROUP * Q_BLOCK), lambda b, n: (0, 0, 0)),
            cur(C_Q_W), prev(kw), cur(kw), nxt(kw), prev(vw), cur(vw), nxt(vw), ctx_blk(kw), ctx_blk(vw),
            cur(C_Q_W), cur(D_MODEL),
            pl.BlockSpec((1, 1, 3 * D_MODEL), lambda b, n: (b, 0, 0)),
            pl.BlockSpec((C_Q_W, D_MODEL), lambda b, n: (0, 0)),
            pl.BlockSpec((1, D_MODEL), lambda b, n: (0, 0)),
        ],
        out_specs=cur(D_MODEL),
        out_shape=jax.ShapeDtypeStruct((bsz, seq_len, D_MODEL), F32),
        compiler_params=_cparams(("parallel", "arbitrary")),
        name="attn_out",
    )(sink_t, q, k, k, k, v, v, v, kx, vx, sg, xs, mod, w_out, fg)


def kernel(x, c, ctx, c_ctx, norm_g, ada_w, ada_b, w_in_ab, v_norm_g, spatial_w, spatial_b, w_out_ab,
           w_in_c, sink_logit, w_out_c, final_g):
    bsz, seq_len, _ = x.shape
    depth = ada_w.shape[0]
    assert depth == 2 and bsz + 1 <= MOD_ROWS
    ctx_row = bsz

    cc = jnp.concatenate([c, c_ctx[None, :], jnp.zeros((MOD_ROWS - bsz - 1, D_MODEL), F32)], axis=0)
    mod = _mod_call(cc, ada_w, ada_b)
    mod0 = mod[0].reshape(MOD_ROWS, 1, 3 * D_MODEL)
    mod1 = mod[1].reshape(MOD_ROWS, 1, 3 * D_MODEL)

    cs = jnp.asarray(_channel_dft_matrix())
    ng0 = norm_g[0].reshape(1, D_MODEL)
    ng1 = norm_g[1].reshape(1, D_MODEL)
    w_in0 = w_in_ab[0].astype(BF16)
    w_out0 = w_out_ab[0].astype(BF16)
    vg = v_norm_g[0].reshape(1, A_W)
    sw = spatial_w[0].astype(BF16)
    sb = jnp.broadcast_to(spatial_b[0][:, :, None], (A_HEADS, CHUNK, A_HEAD_DIM))

    x1 = _mixer_ab_layer(x, mod0, None, ng0, w_in0, vg, sw, sb, w_out0, cs, tl2=256, tk=256)
    ctx1 = _mixer_ab_layer(ctx, mod0, ctx_row, ng0, w_in0, vg, sw, sb, w_out0, cs,
                           tl2=ctx.shape[1] // RADIX, tk=ctx.shape[1] // RADIX)

    w_in1 = w_in_c[0].astype(BF16)
    w_out1 = w_out_c[0].astype(BF16)
    cos_t, sin_t = _rope_tables(seq_len)
    q, k, v, sg = _in_c_call(x1, mod1, ng1, w_in1, jnp.asarray(cos_t), jnp.asarray(sin_t), tm=1024)
    kx, vx = _ctx_kv_call(ctx1, mod1, ctx_row, ng1, w_in1[:, C_Q_W:C_Q_W + 2 * C_KV_W])
    return _attn_out_call(sink_logit[0], q, k, v, kx, vx, sg, x1, mod1, w_out1,
                          final_g.reshape(1, D_MODEL))
```

```python
import functools
import math

import numpy as np
import jax
import jax.numpy as jnp
from jax import lax
from jax.experimental import pallas as pl
from jax.experimental.pallas import tpu as pltpu

F32 = jnp.float32
BF16 = jnp.bfloat16

D_MODEL = 1024
GRID_W = 64
CHUNK = 128
A_HEADS = 4
A_HEAD_DIM = 128
A_W = A_HEADS * A_HEAD_DIM
B_GROUPS = 4
B_GROUP_DIM = 128
B_W = B_GROUPS * B_GROUP_DIM
AB_IN = 3 * A_W + 2 * B_W
C_HEADS = 16
C_KV_HEADS = 4
C_GROUP = C_HEADS // C_KV_HEADS
C_HEAD_DIM = 64
C_Q_W = C_HEADS * C_HEAD_DIM
C_KV_W = C_KV_HEADS * C_HEAD_DIM
C_IN = 2 * C_Q_W + 2 * C_KV_W
WINDOW = 128
Q_BLOCK = 128
ROPE_BASE = 10000.0
NORM_EPS = 1e-6
NEG_INF = -1e30
LOG2E = math.log2(math.e)
RADIX = 4
MOD_ROWS = 16
SOFTMAX_ROWS = 64
SCORE_LOOKAHEAD = 2
ATTN_Q_BLOCKS = 4
OUT_PROJ_BLOCKS = 2
V7X_VMEM_LIMIT = 56 * 1024 * 1024


def _silu(x):
    return x * (1.0 / (1.0 + jnp.exp(-x)))


def _cparams(sem):
    return pltpu.CompilerParams(dimension_semantics=sem, vmem_limit_bytes=V7X_VMEM_LIMIT)


def _channel_dft_matrix():
    n = np.arange(B_GROUP_DIM)
    ang = 2.0 * np.pi * np.outer(n, n) / B_GROUP_DIM
    return np.concatenate([np.cos(ang), np.sin(ang)], axis=1).astype(np.float32)


def _position_dft_matrix(seq_len):
    n2 = seq_len // RADIX
    idx = np.arange(n2)
    ang = 2.0 * np.pi * (np.outer(idx, idx) % n2) / n2
    norm = 1.0 / math.sqrt(seq_len * B_GROUP_DIM)
    return np.concatenate([np.cos(ang) * norm, -np.sin(ang) * norm], axis=1).astype(np.float32)


def _twiddle_tables(seq_len):
    n2 = seq_len // RADIX
    l2 = np.arange(n2)[None, :, None]
    k1 = np.arange(RADIX)[:, None, None]
    ang = 2.0 * np.pi * ((l2 * k1) % seq_len) / seq_len
    ang = np.broadcast_to(ang, (RADIX, n2, 128))
    return np.cos(ang).astype(np.float32), np.sin(ang).astype(np.float32)


def _rope_tables(seq_len):
    t = np.arange(seq_len)
    row = (t // GRID_W).astype(np.float64)
    col = (t % GRID_W).astype(np.float64)
    lane = np.arange(128)
    dd = lane % C_HEAD_DIM
    nf = C_HEAD_DIM // 4
    inv = ROPE_BASE ** (-(dd % nf).astype(np.float64) / nf)
    pos = np.where((dd < C_HEAD_DIM // 2)[None, :], row[:, None], col[:, None])
    ang = pos * inv[None, :]
    sign = np.where((dd % (2 * nf)) < nf, -1.0, 1.0)[None, :]
    return np.cos(ang).astype(np.float32), (np.sin(ang) * sign).astype(np.float32)


def _mod_kernel(c_ref, w_ref, b_ref, o_ref):
    s = _silu(c_ref[...]).astype(BF16)
    o_ref[0] = jnp.dot(s, w_ref[0].astype(BF16), preferred_element_type=F32) + b_ref[0]


def _mod_call(cc, ada_w, ada_b):
    depth = ada_w.shape[0]
    tn = 1024
    return pl.pallas_call(
        _mod_kernel,
        grid=(depth, 3 * D_MODEL // tn),
        in_specs=[
            pl.BlockSpec((MOD_ROWS, D_MODEL), lambda l, j: (0, 0)),
            pl.BlockSpec((1, D_MODEL, tn), lambda l, j: (l, 0, j)),
            pl.BlockSpec((1, 1, tn), lambda l, j: (l, 0, j)),
        ],
        out_specs=pl.BlockSpec((1, MOD_ROWS, tn), lambda l, j: (l, 0, j)),
        out_shape=jax.ShapeDtypeStruct((depth, MOD_ROWS, 3 * D_MODEL), F32),
        compiler_params=_cparams(("arbitrary", "arbitrary")),
        name="adaln_mod",
    )(cc, ada_w, ada_b.reshape(depth, 1, 3 * D_MODEL))


def _modulated_norm(x, mod_row, g):
    shift = mod_row[:, :D_MODEL]
    scale = mod_row[:, D_MODEL:2 * D_MODEL]
    ms = jnp.mean(x * x, axis=-1, keepdims=True)
    h = x * lax.rsqrt(ms + NORM_EPS) * g
    return h * (1.0 + scale) + shift


def _in_ab_kernel(x_ref, mod_ref, ng_ref, w_ref, vg_ref, sw_ref, sb_ref, cs_ref, twc_ref, tws_ref,
                  ya_ref, sgb_ref, uv_ref, *, tl2):
    rows = RADIX * tl2
    x = x_ref[0].reshape(rows, D_MODEL)
    h = _modulated_norm(x, mod_ref[0], ng_ref[...])
    z = jnp.dot(h.astype(BF16), w_ref[...], preferred_element_type=F32)

    v = z[:, A_W:2 * A_W]
    mu = jnp.mean(v, axis=-1, keepdims=True)
    vc = v - mu
    var = jnp.mean(vc * vc, axis=-1, keepdims=True)
    vn = (vc * lax.rsqrt(var + NORM_EPS) * vg_ref[...]).astype(BF16)

    piece = min(CHUNK, tl2)
    nchunk = rows // CHUNK
    assert nchunk % 2 == 0
    for hd in range(A_HEADS):
        c0 = hd * A_HEAD_DIM
        for cp in range(nchunk // 2):
            ra, rb = 2 * cp * CHUNK, (2 * cp + 1) * CHUNK
            vpair = jnp.concatenate([vn[ra:ra + CHUNK, c0:c0 + A_HEAD_DIM],
                                     vn[rb:rb + CHUNK, c0:c0 + A_HEAD_DIM]], axis=1)
            sv2 = jnp.dot(sw_ref[hd], vpair, preferred_element_type=F32)
            for half, r0 in enumerate((ra, rb)):
                sv = sv2[:, half * A_HEAD_DIM:(half + 1) * A_HEAD_DIM] + sb_ref[hd]
                u = z[r0:r0 + CHUNK, c0:c0 + A_HEAD_DIM]
                ga = z[r0:r0 + CHUNK, 2 * A_W + c0:2 * A_W + c0 + A_HEAD_DIM]
                ya = (u * sv * _silu(ga)).astype(BF16)
                for p0 in range(0, CHUNK, piece):
                    l1, off = divmod(r0 + p0, tl2)
                    ya_ref[0, l1, off:off + piece, c0:c0 + A_HEAD_DIM] = ya[p0:p0 + piece]

    gb = z[:, 3 * A_W + B_W:]
    sgb_ref[0] = _silu(gb).astype(BF16).reshape(RADIX, tl2, B_W)

    xb = z[:, 3 * A_W:3 * A_W + B_W].astype(BF16)
    cs = cs_ref[...].astype(BF16)
    for g in range(B_GROUPS):
        c0 = g * B_GROUP_DIM
        ps, qs = [], []
        for j in range(RADIX):
            pq = jnp.dot(xb[j * tl2:(j + 1) * tl2, c0:c0 + B_GROUP_DIM], cs,
                         preferred_element_type=F32)
            ps.append(pq[:, :B_GROUP_DIM])
            qs.append(pq[:, B_GROUP_DIM:])
        p02, p13 = ps[0] - ps[2], ps[1] - ps[3]
        q02, q13 = qs[0] - qs[2], qs[1] - qs[3]
        pe, po = ps[0] + ps[2], ps[1] + ps[3]
        qe, qo = qs[0] + qs[2], qs[1] + qs[3]
        us = [pe + po, p02 - q13, pe - po, p02 + q13]
        vs = [qe + qo, q02 + p13, qe - qo, q02 - p13]
        for k1 in range(RADIX):
            if k1 == 0:
                ut, vt = us[0], vs[0]
            else:
                tc, ts = twc_ref[k1], tws_ref[k1]
                ut = us[k1] * tc - vs[k1] * ts
                vt = us[k1] * ts + vs[k1] * tc
            uv_ref[0, k1, 0, :, c0:c0 + B_GROUP_DIM] = ut.astype(BF16)
            uv_ref[0, k1, 1, :, c0:c0 + B_GROUP_DIM] = vt.astype(BF16)


def _in_ab_call(xs, mod, mod_row, ng, w_in, vg, sw, sb, cs, twc, tws, tl2):
    bsz, seq_len, _ = xs.shape
    n2 = seq_len // RADIX
    x4 = xs.reshape(bsz, RADIX, n2, D_MODEL)
    row_of = (lambda b: b) if mod_row is None else (lambda b: mod_row)
    const2 = lambda b, i: (0, 0)
    const3 = lambda b, i: (0, 0, 0)
    return pl.pallas_call(
        functools.partial(_in_ab_kernel, tl2=tl2),
        grid=(bsz, n2 // tl2),
        in_specs=[
            pl.BlockSpec((1, RADIX, tl2, D_MODEL), lambda b, i: (b, 0, i, 0)),
            pl.BlockSpec((1, 1, 3 * D_MODEL), lambda b, i: (row_of(b), 0, 0)),
            pl.BlockSpec((1, D_MODEL), const2),
            pl.BlockSpec((D_MODEL, AB_IN), const2),
            pl.BlockSpec((1, A_W), const2),
            pl.BlockSpec((A_HEADS, CHUNK, CHUNK), const3),
            pl.BlockSpec((A_HEADS, CHUNK, A_HEAD_DIM), const3),
            pl.BlockSpec((B_GROUP_DIM, 2 * B_GROUP_DIM), const2),
            pl.BlockSpec((RADIX, tl2, 128), lambda b, i: (0, i, 0)),
            pl.BlockSpec((RADIX, tl2, 128), lambda b, i: (0, i, 0)),
        ],
        out_specs=[
            pl.BlockSpec((1, RADIX, tl2, A_W), lambda b, i: (b, 0, i, 0)),
            pl.BlockSpec((1, RADIX, tl2, B_W), lambda b, i: (b, 0, i, 0)),
            pl.BlockSpec((1, RADIX, 2, tl2, B_W), lambda b, i: (b, 0, 0, i, 0)),
        ],
        out_shape=[
            jax.ShapeDtypeStruct((bsz, RADIX, n2, A_W), BF16),
            jax.ShapeDtypeStruct((bsz, RADIX, n2, B_W), BF16),
            jax.ShapeDtypeStruct((bsz, RADIX, 2, n2, B_W), BF16),
        ],
        compiler_params=_cparams(("parallel", "arbitrary")),
        name="in_ab",
    )(x4, mod, ng, w_in, vg, sw, sb, cs, twc, tws)


def _dft_out_kernel(g_ref, uv_ref, ya_ref, sgb_ref, x_ref, mod_ref, w_ref, o_ref, f_ref, *, tk):
    gmat = g_ref[...].astype(BF16)
    for k1 in range(RADIX):
        f = jnp.dot(gmat, uv_ref[0, k1], preferred_element_type=F32)
        for g in range(B_GROUPS):
            f_ref[g, pl.ds(k1, tk, stride=RADIX), :] = f[:, g * B_GROUP_DIM:(g + 1) * B_GROUP_DIM]
    fnat = jnp.concatenate([f_ref[g] for g in range(B_GROUPS)], axis=1)
    yb = (fnat * sgb_ref[0].astype(F32)).astype(BF16)
    y = jnp.dot(ya_ref[0], w_ref[:A_W], preferred_element_type=F32)
    y = y + jnp.dot(yb, w_ref[A_W:], preferred_element_type=F32)
    gate = mod_ref[0][:, 2 * D_MODEL:]
    o_ref[0] = x_ref[0] + gate * y


def _dft_out_call(gmat, uv, ya, sgb, xs, mod, mod_row, w_out, tk):
    bsz, seq_len, _ = xs.shape
    n2 = seq_len // RADIX
    rows = RADIX * tk
    row_of = (lambda b: b) if mod_row is None else (lambda b: mod_row)
    nat = lambda w: pl.BlockSpec((1, rows, w), lambda b, i: (b, i, 0))
    return pl.pallas_call(
        functools.partial(_dft_out_kernel, tk=tk),
        grid=(bsz, n2 // tk),
        in_specs=[
            pl.BlockSpec((tk, 2 * n2), lambda b, i: (i, 0)),
            pl.BlockSpec((1, RADIX, 2 * n2, B_W), lambda b, i: (b, 0, 0, 0)),
            nat(A_W), nat(B_W), nat(D_MODEL),
            pl.BlockSpec((1, 1, 3 * D_MODEL), lambda b, i: (row_of(b), 0, 0)),
            pl.BlockSpec((A_W + B_W, D_MODEL), lambda b, i: (0, 0), pipeline_mode=pl.Buffered(1)),
        ],
        out_specs=nat(D_MODEL),
        out_shape=jax.ShapeDtypeStruct((bsz, seq_len, D_MODEL), F32),
        scratch_shapes=[pltpu.VMEM((B_GROUPS, rows, B_GROUP_DIM), F32)],
        compiler_params=_cparams(("parallel", "arbitrary")),
        name="dft_out",
    )(gmat, uv.reshape(bsz, RADIX, 2 * n2, B_W), ya.reshape(bsz, seq_len, A_W),
      sgb.reshape(bsz, seq_len, B_W), xs, mod, w_out)


def _mixer_ab_layer(xs, mod, mod_row, ng, w_in, vg, sw, sb, w_out, cs, tl2, tk):
    seq_len = xs.shape[1]
    twc, tws = _twiddle_tables(seq_len)
    gmat = jnp.asarray(_position_dft_matrix(seq_len))
    ya, sgb, uv = _in_ab_call(xs, mod, mod_row, ng, w_in, vg, sw, sb, cs,
                              jnp.asarray(twc), jnp.asarray(tws), tl2)
    return _dft_out_call(gmat, uv, ya, sgb, xs, mod, mod_row, w_out, tk=tk)


def _rope_block(t, cos, sin_signed, lane_lo):
    nf = C_HEAD_DIM // 4
    swapped = jnp.where(lane_lo, pltpu.roll(t, 128 - nf, axis=1), pltpu.roll(t, nf, axis=1))
    return t * cos + swapped * sin_signed


def _store_dup_heads(k_ref, c, t, lane):
    r = pltpu.roll(t, C_HEAD_DIM, axis=1)
    first = lane < C_HEAD_DIM
    k_ref[0, :, (2 * c) * 128:(2 * c + 1) * 128] = jnp.where(first, t, r).astype(BF16)
    k_ref[0, :, (2 * c + 1) * 128:(2 * c + 2) * 128] = jnp.where(first, r, t).astype(BF16)


def _in_c_kernel(x_ref, mod_ref, ng_ref, w_ref, cos_ref, sin_ref, q_ref, k_ref, v_ref, sg_ref):
    h = _modulated_norm(x_ref[0], mod_ref[0], ng_ref[...])
    z = jnp.dot(h.astype(BF16), w_ref[...], preferred_element_type=F32)
    cos = cos_ref[...]
    sin = sin_ref[...]
    lane = lax.broadcasted_iota(jnp.int32, cos.shape, 1)
    lane_lo = (lane % (C_HEAD_DIM // 2)) < (C_HEAD_DIM // 4)
    qscale = C_HEAD_DIM ** -0.5 * LOG2E
    for c in range(C_Q_W // 128):
        t = _rope_block(z[:, c * 128:(c + 1) * 128], cos, sin, lane_lo)
        q_ref[0, :, c * 128:(c + 1) * 128] = (t * qscale).astype(BF16)
    for c in range(C_KV_W // 128):
        t = _rope_block(z[:, C_Q_W + c * 128:C_Q_W + (c + 1) * 128], cos, sin, lane_lo)
        _store_dup_heads(k_ref, c, t, lane)
    v_ref[0] = z[:, C_Q_W + C_KV_W:C_Q_W + 2 * C_KV_W].T.astype(BF16)
    sg_ref[0] = _silu(z[:, C_Q_W + 2 * C_KV_W:]).astype(BF16)


def _in_c_call(xs, mod, ng, w_in, cos_t, sin_t, tm):
    bsz, seq_len, _ = xs.shape
    const2 = lambda b, i: (0, 0)
    row_blk = lambda w: pl.BlockSpec((1, tm, w), lambda b, i: (b, i, 0))
    return pl.pallas_call(
        _in_c_kernel,
        grid=(bsz, seq_len // tm),
        in_specs=[
            row_blk(D_MODEL),
            pl.BlockSpec((1, 1, 3 * D_MODEL), lambda b, i: (b, 0, 0)),
            pl.BlockSpec((1, D_MODEL), const2),
            pl.BlockSpec((D_MODEL, C_IN), const2),
            pl.BlockSpec((tm, 128), lambda b, i: (i, 0)),
            pl.BlockSpec((tm, 128), lambda b, i: (i, 0)),
        ],
        out_specs=[row_blk(C_Q_W), row_blk(2 * C_KV_W),
                   pl.BlockSpec((1, C_KV_W, tm), lambda b, i: (b, 0, i)), row_blk(C_Q_W)],
        out_shape=[
            jax.ShapeDtypeStruct((bsz, seq_len, C_Q_W), BF16),
            jax.ShapeDtypeStruct((bsz, seq_len, 2 * C_KV_W), BF16),
            jax.ShapeDtypeStruct((bsz, C_KV_W, seq_len), BF16),
            jax.ShapeDtypeStruct((bsz, seq_len, C_Q_W), BF16),
        ],
        compiler_params=_cparams(("parallel", "arbitrary")),
        name="in_c",
    )(xs, mod, ng, w_in, cos_t, sin_t)


def _ctx_kv_kernel(x_ref, mod_ref, ng_ref, w_ref, k_ref, v_ref):
    h = _modulated_norm(x_ref[0], mod_ref[0], ng_ref[...])
    z = jnp.dot(h.astype(BF16), w_ref[...], preferred_element_type=F32)
    lane = lax.broadcasted_iota(jnp.int32, (z.shape[0], 128), 1)
    for c in range(C_KV_W // 128):
        _store_dup_heads(k_ref, c, z[:, c * 128:(c + 1) * 128], lane)
    v_ref[0] = z[:, C_KV_W:].T.astype(BF16)


def _ctx_kv_call(ctx, mod, mod_row, ng, w_kv):
    bsz, lc, _ = ctx.shape
    return pl.pallas_call(
        _ctx_kv_kernel,
        grid=(bsz,),
        in_specs=[
            pl.BlockSpec((1, lc, D_MODEL), lambda b: (b, 0, 0)),
            pl.BlockSpec((1, 1, 3 * D_MODEL), lambda b: (mod_row, 0, 0)),
            pl.BlockSpec((1, D_MODEL), lambda b: (0, 0)),
            pl.BlockSpec((D_MODEL, 2 * C_KV_W), lambda b: (0, 0)),
        ],
        out_specs=[pl.BlockSpec((1, lc, 2 * C_KV_W), lambda b: (b, 0, 0)),
                   pl.BlockSpec((1, C_KV_W, lc), lambda b: (b, 0, 0))],
        out_shape=[jax.ShapeDtypeStruct((bsz, lc, 2 * C_KV_W), BF16),
                   jax.ShapeDtypeStruct((bsz, C_KV_W, lc), BF16)],
        compiler_params=_cparams(("parallel",)),
        name="ctx_kv",
    )(ctx, mod, ng, w_kv)


def _attn_out_kernel(sink_ref, q_ref, kp_ref, kc_ref, kn_ref, vp_ref, vc_ref, vn_ref, kx_ref, vx_ref,
                     sg_ref, x_ref, mod_ref, w_ref, fg_ref, o_ref):
    n = pl.program_id(1)
    last = pl.num_programs(1) - 1
    gq = C_GROUP * Q_BLOCK
    kj = lax.broadcasted_iota(jnp.int32, (Q_BLOCK, Q_BLOCK), 0)
    qi = lax.broadcasted_iota(jnp.int32, (Q_BLOCK, Q_BLOCK), 1)
    bias_before = jnp.where(kj >= qi, 0.0, NEG_INF).astype(F32)
    bias_after = jnp.where(kj <= qi, 0.0, NEG_INF).astype(F32)
    edge_first = jnp.where(n > 0, 0.0, NEG_INF).astype(F32)
    edge_last = jnp.where(n < last, 0.0, NEG_INF).astype(F32)
    lane = lax.broadcasted_iota(jnp.int32, (Q_BLOCK, 128), 1)
    first = lane < C_HEAD_DIM
    zero = jnp.zeros((Q_BLOCK, 128), BF16)
    row_k = lax.broadcasted_iota(jnp.int32, (128, kx_ref.shape[1] + 3 * Q_BLOCK), 0)
    nq = q_ref.shape[1] // Q_BLOCK

    def window(prev_ref, cur_ref, next_ref, j, lanes):
        if j < 0:
            return prev_ref[0, :, lanes]
        if j >= nq:
            return next_ref[0, :, lanes]
        return cur_ref[0, j * Q_BLOCK:(j + 1) * Q_BLOCK, lanes]

    def scores(qb, kh):
        kl = slice(kh * 128, (kh + 1) * 128)
        kblocks = [kx_ref[0, :, kl]] + [window(kp_ref, kc_ref, kn_ref, j, kl) for j in (qb - 1, qb, qb + 1)]
        biases = (None, (bias_before, edge_first if qb == 0 else None), None,
                  (bias_after, edge_last if qb == nq - 1 else None))
        rows_q = slice(qb * Q_BLOCK, (qb + 1) * Q_BLOCK)
        q4 = []
        for c in range(2):
            qv = q_ref[0, rows_q, kh * 256 + c * 128:kh * 256 + (c + 1) * 128]
            q4 += [jnp.where(first, qv, zero), jnp.where(first, zero, qv)]
        q4 = jnp.concatenate(q4, axis=0)
        k2 = jnp.concatenate(kblocks, axis=0)
        st = lax.dot_general(k2, q4, (((1,), (1,)), ((), ())), preferred_element_type=F32)
        blocks, m8, r = [], None, 0
        for kb, bias in zip(kblocks, biases):
            sb = st[r:r + kb.shape[0]]
            r += kb.shape[0]
            if bias is not None:
                mask_bias, edge = bias
                cols = [sb[:, j * Q_BLOCK:(j + 1) * Q_BLOCK] + mask_bias for j in range(C_GROUP)]
                sb = jnp.concatenate(cols, axis=1)
                if edge is not None:
                    sb = sb + edge
            blocks.append(sb)
            mb = jnp.max(sb.reshape(sb.shape[0] // 8, 8, gq), axis=0)
            m8 = mb if m8 is None else jnp.maximum(m8, mb)
        return blocks, m8

    def softmax_pv(qb, kh, blocks, m8):
        sink2 = sink_ref[kh] * LOG2E
        m = jnp.maximum(jnp.max(m8, axis=0, keepdims=True), sink2)
        pts = []
        for blk in blocks:
            for r in range(0, blk.shape[0], SOFTMAX_ROWS):
                pts.append(jnp.exp2(blk[r:r + SOFTMAX_ROWS] - m).astype(BF16))
        pt = jnp.concatenate(pts, axis=0)
        vrows = slice((kh // 2) * 128, (kh // 2 + 1) * 128)
        vblocks = [vx_ref[0, vrows, :]]
        for j in (qb - 1, qb, qb + 1):
            if j < 0:
                vblocks.append(vp_ref[0, vrows, :])
            elif j >= nq:
                vblocks.append(vn_ref[0, vrows, :])
            else:
                vblocks.append(vc_ref[0, vrows, j * Q_BLOCK:(j + 1) * Q_BLOCK])
        vpair = jnp.concatenate(vblocks, axis=1)
        own = (row_k < C_HEAD_DIM) if kh % 2 == 0 else (row_k >= C_HEAD_DIM)
        vsum = jnp.where(own, vpair, jnp.ones_like(vpair))
        ot = jnp.dot(vsum, pt, preferred_element_type=F32)
        r0 = (kh % 2) * C_HEAD_DIM
        r1 = C_HEAD_DIM - r0
        denom = ot[r1:r1 + C_HEAD_DIM] + jnp.exp2(sink2 - m)
        ot = ot[r0:r0 + C_HEAD_DIM] / denom
        cols = []
        for c in range(2):
            pair = jnp.concatenate([ot[:, (2 * c) * Q_BLOCK:(2 * c + 1) * Q_BLOCK],
                                    ot[:, (2 * c + 1) * Q_BLOCK:(2 * c + 2) * Q_BLOCK]], axis=0)
            cols.append(pair.T)
        return cols

    chains = [(qb, kh) for qb in range(nq) for kh in range(C_KV_HEADS)]
    o_cols = {qb: [] for qb in range(nq)}
    gate = mod_ref[0][:, 2 * D_MODEL:]

    def finish(qb0, qb1):
        rows = slice(qb0 * Q_BLOCK, qb1 * Q_BLOCK)
        o_all = jnp.concatenate([jnp.concatenate(o_cols[qb], axis=1) for qb in range(qb0, qb1)], axis=0)
        y = jnp.dot((o_all * sg_ref[0, rows, :].astype(F32)).astype(BF16), w_ref[...],
                    preferred_element_type=F32)
        x2 = x_ref[0, rows, :] + gate * y
        ms = jnp.mean(x2 * x2, axis=-1, keepdims=True)
        o_ref[0, rows, :] = x2 * lax.rsqrt(ms + NORM_EPS) * fg_ref[...]

    pending = [scores(*ch) for ch in chains[:SCORE_LOOKAHEAD]]
    for idx, (qb, kh) in enumerate(chains):
        if idx + SCORE_LOOKAHEAD < len(chains):
            pending.append(scores(*chains[idx + SCORE_LOOKAHEAD]))
        o_cols[qb] += softmax_pv(qb, kh, *pending.pop(0))
        if kh == C_KV_HEADS - 1 and qb % OUT_PROJ_BLOCKS == OUT_PROJ_BLOCKS - 1:
            finish(qb + 1 - OUT_PROJ_BLOCKS, qb + 1)


def _attn_out_call(sink, q, k, v, kx, vx, sg, xs, mod, w_out, fg):
    bsz, seq_len, _ = xs.shape
    nblk = seq_len // Q_BLOCK
    nq = ATTN_Q_BLOCKS
    nstep = nblk // nq
    lc = kx.shape[1]
    assert lc % SOFTMAX_ROWS == 0 and Q_BLOCK % SOFTMAX_ROWS == 0 and nq % OUT_PROJ_BLOCKS == 0
    cur = lambda w: pl.BlockSpec((1, nq * Q_BLOCK, w), lambda b, n: (b, n, 0))
    prev = lambda w: pl.BlockSpec((1, Q_BLOCK, w), lambda b, n: (b, jnp.maximum(nq * n - 1, 0), 0))
    nxt = lambda w: pl.BlockSpec((1, Q_BLOCK, w), lambda b, n: (b, jnp.minimum(nq * n + nq, nblk - 1), 0))
    ctx_blk = lambda w: pl.BlockSpec((1, lc, w), lambda b, n: (b, 0, 0))
    kw = 2 * C_KV_W
    vcur = pl.BlockSpec((1, C_KV_W, nq * Q_BLOCK), lambda b, n: (b, 0, n))
    vprev = pl.BlockSpec((1, C_KV_W, Q_BLOCK), lambda b, n: (b, 0, jnp.maximum(nq * n - 1, 0)))
    vnxt = pl.BlockSpec((1, C_KV_W, Q_BLOCK), lambda b, n: (b, 0, jnp.minimum(nq * n + nq, nblk - 1)))
    vctx = pl.BlockSpec((1, C_KV_W, lc), lambda b, n: (b, 0, 0))
    sink_t = jnp.repeat(sink.reshape(C_KV_HEADS, 1, C_GROUP), Q_BLOCK, axis=2)
    return pl.pallas_call(
        _attn_out_kernel,
        grid=(bsz, nstep),
        in_specs=[
            pl.BlockSpec((C_KV_HEADS, 1, C_GROUP * Q_BLOCK), lambda b, n: (0, 0, 0)),
            cur(C_Q_W), prev(kw), cur(kw), nxt(kw), vprev, vcur, vnxt, ctx_blk(kw), vctx,
            cur(C_Q_W), cur(D_MODEL),
            pl.BlockSpec((1, 1, 3 * D_MODEL), lambda b, n: (b, 0, 0)),
            pl.BlockSpec((C_Q_W, D_MODEL), lambda b, n: (0, 0)),
            pl.BlockSpec((1, D_MODEL), lambda b, n: (0, 0)),
        ],
        out_specs=cur(D_MODEL),
        out_shape=jax.ShapeDtypeStruct((bsz, seq_len, D_MODEL), F32),
        compiler_params=_cparams(("parallel", "arbitrary")),
        name="attn_out",
    )(sink_t, q, k, k, k, v, v, v, kx, vx, sg, xs, mod, w_out, fg)


def kernel(x, c, ctx, c_ctx, norm_g, ada_w, ada_b, w_in_ab, v_norm_g, spatial_w, spatial_b, w_out_ab,
           w_in_c, sink_logit, w_out_c, final_g):
    bsz, seq_len, _ = x.shape
    depth = ada_w.shape[0]
    assert depth == 2 and bsz + 1 <= MOD_ROWS
    ctx_row = bsz

    cc = jnp.concatenate([c, c_ctx[None, :], jnp.zeros((MOD_ROWS - bsz - 1, D_MODEL), F32)], axis=0)
    mod = _mod_call(cc, ada_w, ada_b)
    mod0 = mod[0].reshape(MOD_ROWS, 1, 3 * D_MODEL)
    mod1 = mod[1].reshape(MOD_ROWS, 1, 3 * D_MODEL)

    cs = jnp.asarray(_channel_dft_matrix())
    ng0 = norm_g[0].reshape(1, D_MODEL)
    ng1 = norm_g[1].reshape(1, D_MODEL)
    w_in0 = w_in_ab[0].astype(BF16)
    w_out0 = w_out_ab[0].astype(BF16)
    vg = v_norm_g[0].reshape(1, A_W)
    sw = spatial_w[0].astype(BF16)
    sb = jnp.broadcast_to(spatial_b[0][:, :, None], (A_HEADS, CHUNK, A_HEAD_DIM))

    x1 = _mixer_ab_layer(x, mod0, None, ng0, w_in0, vg, sw, sb, w_out0, cs, tl2=256, tk=256)
    ctx1 = _mixer_ab_layer(ctx, mod0, ctx_row, ng0, w_in0, vg, sw, sb, w_out0, cs,
                           tl2=ctx.shape[1] // RADIX, tk=ctx.shape[1] // RADIX)

    w_in1 = w_in_c[0].astype(BF16)
    w_out1 = w_out_c[0].astype(BF16)
    cos_t, sin_t = _rope_tables(seq_len)
    q, k, v, sg = _in_c_call(x1, mod1, ng1, w_in1, jnp.asarray(cos_t), jnp.asarray(sin_t), tm=1024)
    kx, vx = _ctx_kv_call(ctx1, mod1, ctx_row, ng1, w_in1[:, C_Q_W:C_Q_W + 2 * C_KV_W])
    return _attn_out_call(sink_logit[0], q, k, v, kx, vx, sg, x1, mod1, w_out1,
                          final_g.reshape(1, D_MODEL))
```

```python
import functools
import math

import numpy as np
import jax
import jax.numpy as jnp
from jax import lax
from jax.experimental import pallas as pl
from jax.experimental.pallas import tpu as pltpu

F32 = jnp.float32
BF16 = jnp.bfloat16

D_MODEL = 1024
GRID_W = 64
CHUNK = 128
A_HEADS = 4
A_HEAD_DIM = 128
A_W = A_HEADS * A_HEAD_DIM
B_GROUPS = 4
B_GROUP_DIM = 128
B_W = B_GROUPS * B_GROUP_DIM
AB_IN = 3 * A_W + 2 * B_W
C_HEADS = 16
C_KV_HEADS = 4
C_GROUP = C_HEADS // C_KV_HEADS
C_HEAD_DIM = 64
C_Q_W = C_HEADS * C_HEAD_DIM
C_KV_W = C_KV_HEADS * C_HEAD_DIM
C_IN = 2 * C_Q_W + 2 * C_KV_W
WINDOW = 128
Q_BLOCK = 128
ROPE_BASE = 10000.0
NORM_EPS = 1e-6
NEG_INF = -1e30
LOG2E = math.log2(math.e)
RADIX = 4
MOD_ROWS = 16
SOFTMAX_ROWS = 64
SCORE_LOOKAHEAD = 2
ATTN_Q_BLOCKS = 8
OUT_PROJ_BLOCKS = 2
V7X_VMEM_LIMIT = 56 * 1024 * 1024


def _silu(x):
    return x * (1.0 / (1.0 + jnp.exp(-x)))


def _cparams(sem):
    return pltpu.CompilerParams(dimension_semantics=sem, vmem_limit_bytes=V7X_VMEM_LIMIT)


def _channel_dft_matrix():
    n = np.arange(B_GROUP_DIM)
    ang = 2.0 * np.pi * np.outer(n, n) / B_GROUP_DIM
    return np.concatenate([np.cos(ang), np.sin(ang)], axis=1).astype(np.float32)


def _position_dft_matrix(seq_len):
    n2 = seq_len // RADIX
    idx = np.arange(n2)
    ang = 2.0 * np.pi * (np.outer(idx, idx) % n2) / n2
    norm = 1.0 / math.sqrt(seq_len * B_GROUP_DIM)
    return np.concatenate([np.cos(ang) * norm, -np.sin(ang) * norm], axis=1).astype(np.float32)


def _twiddle_tables(seq_len):
    n2 = seq_len // RADIX
    l2 = np.arange(n2)[None, :, None]
    k1 = np.arange(RADIX)[:, None, None]
    ang = 2.0 * np.pi * ((l2 * k1) % seq_len) / seq_len
    ang = np.broadcast_to(ang, (RADIX, n2, 128))
    return np.cos(ang).astype(np.float32), np.sin(ang).astype(np.float32)


def _rope_tables(seq_len):
    t = np.arange(seq_len)
    row = (t // GRID_W).astype(np.float64)
    col = (t % GRID_W).astype(np.float64)
    lane = np.arange(128)
    dd = lane % C_HEAD_DIM
    nf = C_HEAD_DIM // 4
    inv = ROPE_BASE ** (-(dd % nf).astype(np.float64) / nf)
    pos = np.where((dd < C_HEAD_DIM // 2)[None, :], row[:, None], col[:, None])
    ang = pos * inv[None, :]
    sign = np.where((dd % (2 * nf)) < nf, -1.0, 1.0)[None, :]
    return np.cos(ang).astype(np.float32), (np.sin(ang) * sign).astype(np.float32)


def _mod_kernel(c_ref, w_ref, b_ref, o_ref):
    s = _silu(c_ref[...]).astype(BF16)
    o_ref[0] = jnp.dot(s, w_ref[0].astype(BF16), preferred_element_type=F32) + b_ref[0]


def _mod_call(cc, ada_w, ada_b):
    depth = ada_w.shape[0]
    tn = 1024
    return pl.pallas_call(
        _mod_kernel,
        grid=(depth, 3 * D_MODEL // tn),
        in_specs=[
            pl.BlockSpec((MOD_ROWS, D_MODEL), lambda l, j: (0, 0)),
            pl.BlockSpec((1, D_MODEL, tn), lambda l, j: (l, 0, j)),
            pl.BlockSpec((1, 1, tn), lambda l, j: (l, 0, j)),
        ],
        out_specs=pl.BlockSpec((1, MOD_ROWS, tn), lambda l, j: (l, 0, j)),
        out_shape=jax.ShapeDtypeStruct((depth, MOD_ROWS, 3 * D_MODEL), F32),
        compiler_params=_cparams(("arbitrary", "arbitrary")),
        name="adaln_mod",
    )(cc, ada_w, ada_b.reshape(depth, 1, 3 * D_MODEL))


def _modulated_norm(x, mod_row, g):
    shift = mod_row[:, :D_MODEL]
    scale = mod_row[:, D_MODEL:2 * D_MODEL]
    ms = jnp.mean(x * x, axis=-1, keepdims=True)
    h = x * lax.rsqrt(ms + NORM_EPS) * g
    return h * (1.0 + scale) + shift


def _in_ab_kernel(x_ref, mod_ref, ng_ref, w_ref, vg_ref, sw_ref, sb_ref, cs_ref, twc_ref, tws_ref,
                  ya_ref, sgb_ref, uv_ref, *, tl2, ts):
    rows = RADIX * ts
    piece = min(CHUNK, ts)
    nchunk = rows // CHUNK
    assert nchunk % 2 == 0 and tl2 % ts == 0
    cs = cs_ref[...].astype(BF16)

    def project(o):
        x = x_ref[0, :, o:o + ts, :].reshape(rows, D_MODEL)
        h = _modulated_norm(x, mod_ref[0], ng_ref[...])
        return jnp.dot(h.astype(BF16), w_ref[...], preferred_element_type=F32)

    def mix(o, z):
        v = z[:, A_W:2 * A_W]
        mu = jnp.mean(v, axis=-1, keepdims=True)
        vc = v - mu
        var = jnp.mean(vc * vc, axis=-1, keepdims=True)
        vn = (vc * lax.rsqrt(var + NORM_EPS) * vg_ref[...]).astype(BF16)
        for hd in range(A_HEADS):
            c0 = hd * A_HEAD_DIM
            for cp in range(nchunk // 2):
                ra, rb = 2 * cp * CHUNK, (2 * cp + 1) * CHUNK
                vpair = jnp.concatenate([vn[ra:ra + CHUNK, c0:c0 + A_HEAD_DIM],
                                         vn[rb:rb + CHUNK, c0:c0 + A_HEAD_DIM]], axis=1)
                sv2 = jnp.dot(sw_ref[hd], vpair, preferred_element_type=F32)
                for half, r0 in enumerate((ra, rb)):
                    sv = sv2[:, half * A_HEAD_DIM:(half + 1) * A_HEAD_DIM] + sb_ref[hd]
                    u = z[r0:r0 + CHUNK, c0:c0 + A_HEAD_DIM]
                    ga = z[r0:r0 + CHUNK, 2 * A_W + c0:2 * A_W + c0 + A_HEAD_DIM]
                    ya = (u * sv * _silu(ga)).astype(BF16)
                    for p0 in range(0, CHUNK, piece):
                        l1, off = divmod(r0 + p0, ts)
                        ya_ref[0, l1, o + off:o + off + piece, c0:c0 + A_HEAD_DIM] = ya[p0:p0 + piece]

        gb = z[:, 3 * A_W + B_W:]
        sgb_ref[0, :, o:o + ts, :] = _silu(gb).astype(BF16).reshape(RADIX, ts, B_W)

        xb = z[:, 3 * A_W:3 * A_W + B_W].astype(BF16)
        for g in range(B_GROUPS):
            c0 = g * B_GROUP_DIM
            ps, qs = [], []
            for j in range(RADIX):
                pq = jnp.dot(xb[j * ts:(j + 1) * ts, c0:c0 + B_GROUP_DIM], cs,
                             preferred_element_type=F32)
                ps.append(pq[:, :B_GROUP_DIM])
                qs.append(pq[:, B_GROUP_DIM:])
            p02, p13 = ps[0] - ps[2], ps[1] - ps[3]
            q02, q13 = qs[0] - qs[2], qs[1] - qs[3]
            pe, po = ps[0] + ps[2], ps[1] + ps[3]
            qe, qo = qs[0] + qs[2], qs[1] + qs[3]
            us = [pe + po, p02 - q13, pe - po, p02 + q13]
            vs = [qe + qo, q02 + p13, qe - qo, q02 - p13]
            for k1 in range(RADIX):
                if k1 == 0:
                    ut, vt = us[0], vs[0]
                else:
                    tc, tsn = twc_ref[k1, o:o + ts, :], tws_ref[k1, o:o + ts, :]
                    ut = us[k1] * tc - vs[k1] * tsn
                    vt = us[k1] * tsn + vs[k1] * tc
                uv_ref[0, k1, 0, o:o + ts, c0:c0 + B_GROUP_DIM] = ut.astype(BF16)
                uv_ref[0, k1, 1, o:o + ts, c0:c0 + B_GROUP_DIM] = vt.astype(BF16)

    for o in range(0, tl2, ts):
        mix(o, project(o))


def _in_ab_call(xs, mod, mod_row, ng, w_in, vg, sw, sb, cs, twc, tws, tl2):
    bsz, seq_len, _ = xs.shape
    n2 = seq_len // RADIX
    x4 = xs.reshape(bsz, RADIX, n2, D_MODEL)
    row_of = (lambda b: b) if mod_row is None else (lambda b: mod_row)
    const2 = lambda b, i: (0, 0)
    const3 = lambda b, i: (0, 0, 0)
    return pl.pallas_call(
        functools.partial(_in_ab_kernel, tl2=tl2, ts=tl2),
        grid=(bsz, n2 // tl2),
        in_specs=[
            pl.BlockSpec((1, RADIX, tl2, D_MODEL), lambda b, i: (b, 0, i, 0)),
            pl.BlockSpec((1, 1, 3 * D_MODEL), lambda b, i: (row_of(b), 0, 0)),
            pl.BlockSpec((1, D_MODEL), const2),
            pl.BlockSpec((D_MODEL, AB_IN), const2),
            pl.BlockSpec((1, A_W), const2),
            pl.BlockSpec((A_HEADS, CHUNK, CHUNK), const3),
            pl.BlockSpec((A_HEADS, CHUNK, A_HEAD_DIM), const3),
            pl.BlockSpec((B_GROUP_DIM, 2 * B_GROUP_DIM), const2),
            pl.BlockSpec((RADIX, tl2, 128), lambda b, i: (0, i, 0)),
            pl.BlockSpec((RADIX, tl2, 128), lambda b, i: (0, i, 0)),
        ],
        out_specs=[
            pl.BlockSpec((1, RADIX, tl2, A_W), lambda b, i: (b, 0, i, 0)),
            pl.BlockSpec((1, RADIX, tl2, B_W), lambda b, i: (b, 0, i, 0)),
            pl.BlockSpec((1, RADIX, 2, tl2, B_W), lambda b, i: (b, 0, 0, i, 0)),
        ],
        out_shape=[
            jax.ShapeDtypeStruct((bsz, RADIX, n2, A_W), BF16),
            jax.ShapeDtypeStruct((bsz, RADIX, n2, B_W), BF16),
            jax.ShapeDtypeStruct((bsz, RADIX, 2, n2, B_W), BF16),
        ],
        compiler_params=_cparams(("parallel", "arbitrary")),
        name="in_ab",
    )(x4, mod, ng, w_in, vg, sw, sb, cs, twc, tws)


def _dft_out_kernel(g_ref, uv_ref, ya_ref, sgb_ref, x_ref, mod_ref, w_ref, o_ref, f_ref, *, tk):
    r0 = pl.multiple_of(pl.program_id(1) * tk, tk)
    gmat = g_ref[pl.ds(r0, tk), :].astype(BF16)
    for k1 in range(RADIX):
        f = jnp.dot(gmat, uv_ref[0, k1], preferred_element_type=F32)
        for g in range(B_GROUPS):
            f_ref[g, pl.ds(k1, tk, stride=RADIX), :] = f[:, g * B_GROUP_DIM:(g + 1) * B_GROUP_DIM]
    fnat = jnp.concatenate([f_ref[g] for g in range(B_GROUPS)], axis=1)
    yb = (fnat * sgb_ref[0].astype(F32)).astype(BF16)
    y = jnp.dot(ya_ref[0], w_ref[:A_W], preferred_element_type=F32)
    y = y + jnp.dot(yb, w_ref[A_W:], preferred_element_type=F32)
    gate = mod_ref[0][:, 2 * D_MODEL:]
    o_ref[0] = x_ref[0] + gate * y


def _dft_out_call(gmat, uv, ya, sgb, xs, mod, mod_row, w_out, tk):
    bsz, seq_len, _ = xs.shape
    n2 = seq_len // RADIX
    rows = RADIX * tk
    row_of = (lambda b: b) if mod_row is None else (lambda b: mod_row)
    nat = lambda w: pl.BlockSpec((1, rows, w), lambda b, i: (b, i, 0))
    return pl.pallas_call(
        functools.partial(_dft_out_kernel, tk=tk),
        grid=(bsz, n2 // tk),
        in_specs=[
            pl.BlockSpec((n2, 2 * n2), lambda b, i: (0, 0), pipeline_mode=pl.Buffered(1)),
            pl.BlockSpec((1, RADIX, 2 * n2, B_W), lambda b, i: (b, 0, 0, 0)),
            nat(A_W), nat(B_W), nat(D_MODEL),
            pl.BlockSpec((1, 1, 3 * D_MODEL), lambda b, i: (row_of(b), 0, 0)),
            pl.BlockSpec((A_W + B_W, D_MODEL), lambda b, i: (0, 0), pipeline_mode=pl.Buffered(1)),
        ],
        out_specs=nat(D_MODEL),
        out_shape=jax.ShapeDtypeStruct((bsz, seq_len, D_MODEL), F32),
        scratch_shapes=[pltpu.VMEM((B_GROUPS, rows, B_GROUP_DIM), F32)],
        compiler_params=_cparams(("parallel", "arbitrary")),
        name="dft_out",
    )(gmat, uv.reshape(bsz, RADIX, 2 * n2, B_W), ya.reshape(bsz, seq_len, A_W),
      sgb.reshape(bsz, seq_len, B_W), xs, mod, w_out)


def _mixer_ab_layer(xs, mod, mod_row, ng, w_in, vg, sw, sb, w_out, cs, tl2, tk):
    seq_len = xs.shape[1]
    twc, tws = _twiddle_tables(seq_len)
    gmat = jnp.asarray(_position_dft_matrix(seq_len))
    ya, sgb, uv = _in_ab_call(xs, mod, mod_row, ng, w_in, vg, sw, sb, cs,
                              jnp.asarray(twc), jnp.asarray(tws), tl2)
    return _dft_out_call(gmat, uv, ya, sgb, xs, mod, mod_row, w_out, tk=tk)


def _rope_block(t, cos, sin_signed, lane_lo):
    nf = C_HEAD_DIM // 4
    swapped = jnp.where(lane_lo, pltpu.roll(t, 128 - nf, axis=1), pltpu.roll(t, nf, axis=1))
    return t * cos + swapped * sin_signed


def _store_dup_heads(k_ref, c, t, lane):
    r = pltpu.roll(t, C_HEAD_DIM, axis=1)
    first = lane < C_HEAD_DIM
    k_ref[0, :, (2 * c) * 128:(2 * c + 1) * 128] = jnp.where(first, t, r).astype(BF16)
    k_ref[0, :, (2 * c + 1) * 128:(2 * c + 2) * 128] = jnp.where(first, r, t).astype(BF16)


def _in_c_kernel(x_ref, mod_ref, ng_ref, w_ref, cos_ref, sin_ref, q_ref, k_ref, v_ref, sg_ref):
    h = _modulated_norm(x_ref[0], mod_ref[0], ng_ref[...])
    z = jnp.dot(h.astype(BF16), w_ref[...], preferred_element_type=F32)
    cos = cos_ref[...]
    sin = sin_ref[...]
    lane = lax.broadcasted_iota(jnp.int32, cos.shape, 1)
    lane_lo = (lane % (C_HEAD_DIM // 2)) < (C_HEAD_DIM // 4)
    qscale = C_HEAD_DIM ** -0.5 * LOG2E
    for c in range(C_Q_W // 128):
        t = _rope_block(z[:, c * 128:(c + 1) * 128], cos, sin, lane_lo)
        q_ref[0, :, c * 128:(c + 1) * 128] = (t * qscale).astype(BF16)
    for c in range(C_KV_W // 128):
        t = _rope_block(z[:, C_Q_W + c * 128:C_Q_W + (c + 1) * 128], cos, sin, lane_lo)
        _store_dup_heads(k_ref, c, t, lane)
    v_ref[0] = z[:, C_Q_W + C_KV_W:C_Q_W + 2 * C_KV_W].T.astype(BF16)
    sg_ref[0] = _silu(z[:, C_Q_W + 2 * C_KV_W:]).astype(BF16)


def _in_c_call(xs, mod, ng, w_in, cos_t, sin_t, tm):
    bsz, seq_len, _ = xs.shape
    const2 = lambda b, i: (0, 0)
    row_blk = lambda w: pl.BlockSpec((1, tm, w), lambda b, i: (b, i, 0))
    return pl.pallas_call(
        _in_c_kernel,
        grid=(bsz, seq_len // tm),
        in_specs=[
            row_blk(D_MODEL),
            pl.BlockSpec((1, 1, 3 * D_MODEL), lambda b, i: (b, 0, 0)),
            pl.BlockSpec((1, D_MODEL), const2),
            pl.BlockSpec((D_MODEL, C_IN), const2),
            pl.BlockSpec((tm, 128), lambda b, i: (i, 0)),
            pl.BlockSpec((tm, 128), lambda b, i: (i, 0)),
        ],
        out_specs=[row_blk(C_Q_W), row_blk(2 * C_KV_W),
                   pl.BlockSpec((1, C_KV_W, tm), lambda b, i: (b, 0, i)), row_blk(C_Q_W)],
        out_shape=[
            jax.ShapeDtypeStruct((bsz, seq_len, C_Q_W), BF16),
            jax.ShapeDtypeStruct((bsz, seq_len, 2 * C_KV_W), BF16),
            jax.ShapeDtypeStruct((bsz, C_KV_W, seq_len), BF16),
            jax.ShapeDtypeStruct((bsz, seq_len, C_Q_W), BF16),
        ],
        compiler_params=_cparams(("parallel", "arbitrary")),
        name="in_c",
    )(xs, mod, ng, w_in, cos_t, sin_t)


def _ctx_kv_kernel(x_ref, mod_ref, ng_ref, w_ref, k_ref, v_ref):
    h = _modulated_norm(x_ref[0], mod_ref[0], ng_ref[...])
    z = jnp.dot(h.astype(BF16), w_ref[...], preferred_element_type=F32)
    lane = lax.broadcasted_iota(jnp.int32, (z.shape[0], 128), 1)
    for c in range(C_KV_W // 128):
        _store_dup_heads(k_ref, c, z[:, c * 128:(c + 1) * 128], lane)
    v_ref[0] = z[:, C_KV_W:].T.astype(BF16)


def _ctx_kv_call(ctx, mod, mod_row, ng, w_kv):
    bsz, lc, _ = ctx.shape
    return pl.pallas_call(
        _ctx_kv_kernel,
        grid=(bsz,),
        in_specs=[
            pl.BlockSpec((1, lc, D_MODEL), lambda b: (b, 0, 0)),
            pl.BlockSpec((1, 1, 3 * D_MODEL), lambda b: (mod_row, 0, 0)),
            pl.BlockSpec((1, D_MODEL), lambda b: (0, 0)),
            pl.BlockSpec((D_MODEL, 2 * C_KV_W), lambda b: (0, 0)),
        ],
        out_specs=[pl.BlockSpec((1, lc, 2 * C_KV_W), lambda b: (b, 0, 0)),
                   pl.BlockSpec((1, C_KV_W, lc), lambda b: (b, 0, 0))],
        out_shape=[jax.ShapeDtypeStruct((bsz, lc, 2 * C_KV_W), BF16),
                   jax.ShapeDtypeStruct((bsz, C_KV_W, lc), BF16)],
        compiler_params=_cparams(("parallel",)),
        name="ctx_kv",
    )(ctx, mod, ng, w_kv)


def _attn_out_kernel(sink_ref, q_ref, kp_ref, kc_ref, kn_ref, vp_ref, vc_ref, vn_ref, kx_ref, vx_ref,
                     sg_ref, x_ref, mod_ref, w_ref, fg_ref, o_ref):
    n = pl.program_id(1)
    last = pl.num_programs(1) - 1
    gq = C_GROUP * Q_BLOCK
    kj = lax.broadcasted_iota(jnp.int32, (Q_BLOCK, Q_BLOCK), 0)
    qi = lax.broadcasted_iota(jnp.int32, (Q_BLOCK, Q_BLOCK), 1)
    bias_before = jnp.where(kj >= qi, 0.0, NEG_INF).astype(F32)
    bias_after = jnp.where(kj <= qi, 0.0, NEG_INF).astype(F32)
    edge_first = jnp.where(n > 0, 0.0, NEG_INF).astype(F32)
    edge_last = jnp.where(n < last, 0.0, NEG_INF).astype(F32)
    lane = lax.broadcasted_iota(jnp.int32, (Q_BLOCK, 128), 1)
    first = lane < C_HEAD_DIM
    zero = jnp.zeros((Q_BLOCK, 128), BF16)
    row_k = lax.broadcasted_iota(jnp.int32, (128, kx_ref.shape[1] + 3 * Q_BLOCK), 0)
    nq = q_ref.shape[1] // Q_BLOCK

    def window(prev_ref, cur_ref, next_ref, j, lanes):
        if j < 0:
            return prev_ref[0, :, lanes]
        if j >= nq:
            return next_ref[0, :, lanes]
        return cur_ref[0, j * Q_BLOCK:(j + 1) * Q_BLOCK, lanes]

    def scores(qb, kh):
        kl = slice(kh * 128, (kh + 1) * 128)
        kblocks = [kx_ref[0, :, kl]] + [window(kp_ref, kc_ref, kn_ref, j, kl) for j in (qb - 1, qb, qb + 1)]
        biases = (None, (bias_before, edge_first if qb == 0 else None), None,
                  (bias_after, edge_last if qb == nq - 1 else None))
        rows_q = slice(qb * Q_BLOCK, (qb + 1) * Q_BLOCK)
        q4 = []
        for c in range(2):
            qv = q_ref[0, rows_q, kh * 256 + c * 128:kh * 256 + (c + 1) * 128]
            q4 += [jnp.where(first, qv, zero), jnp.where(first, zero, qv)]
        q4 = jnp.concatenate(q4, axis=0)
        k2 = jnp.concatenate(kblocks, axis=0)
        st = lax.dot_general(k2, q4, (((1,), (1,)), ((), ())), preferred_element_type=F32)
        blocks, m8, r = [], None, 0
        for kb, bias in zip(kblocks, biases):
            sb = st[r:r + kb.shape[0]]
            r += kb.shape[0]
            if bias is not None:
                mask_bias, edge = bias
                cols = [sb[:, j * Q_BLOCK:(j + 1) * Q_BLOCK] + mask_bias for j in range(C_GROUP)]
                sb = jnp.concatenate(cols, axis=1)
                if edge is not None:
                    sb = sb + edge
            blocks.append(sb)
            mb = jnp.max(sb.reshape(sb.shape[0] // 8, 8, gq), axis=0)
            m8 = mb if m8 is None else jnp.maximum(m8, mb)
        return blocks, m8

    def softmax_pv(qb, kh, blocks, m8):
        sink2 = sink_ref[kh] * LOG2E
        m = jnp.maximum(jnp.max(m8, axis=0, keepdims=True), sink2)
        pts = []
        for blk in blocks:
            for r in range(0, blk.shape[0], SOFTMAX_ROWS):
                pts.append(jnp.exp2(blk[r:r + SOFTMAX_ROWS] - m).astype(BF16))
        pt = jnp.concatenate(pts, axis=0)
        vrows = slice((kh // 2) * 128, (kh // 2 + 1) * 128)
        vblocks = [vx_ref[0, vrows, :]]
        for j in (qb - 1, qb, qb + 1):
            if j < 0:
                vblocks.append(vp_ref[0, vrows, :])
            elif j >= nq:
                vblocks.append(vn_ref[0, vrows, :])
            else:
                vblocks.append(vc_ref[0, vrows, j * Q_BLOCK:(j + 1) * Q_BLOCK])
        vpair = jnp.concatenate(vblocks, axis=1)
        own = (row_k < C_HEAD_DIM) if kh % 2 == 0 else (row_k >= C_HEAD_DIM)
        vsum = jnp.where(own, vpair, jnp.ones_like(vpair))
        ot = jnp.dot(vsum, pt, preferred_element_type=F32)
        r0 = (kh % 2) * C_HEAD_DIM
        r1 = C_HEAD_DIM - r0
        denom = ot[r1:r1 + C_HEAD_DIM] + jnp.exp2(sink2 - m)
        ot = ot[r0:r0 + C_HEAD_DIM] / denom
        cols = []
        for c in range(2):
            pair = jnp.concatenate([ot[:, (2 * c) * Q_BLOCK:(2 * c + 1) * Q_BLOCK],
                                    ot[:, (2 * c + 1) * Q_BLOCK:(2 * c + 2) * Q_BLOCK]], axis=0)
            cols.append(pair.T)
        return cols

    chains = [(qb, kh) for qb in range(nq) for kh in range(C_KV_HEADS)]
    o_cols = {qb: [] for qb in range(nq)}
    gate = mod_ref[0][:, 2 * D_MODEL:]

    def finish(qb0, qb1):
        rows = slice(qb0 * Q_BLOCK, qb1 * Q_BLOCK)
        o_all = jnp.concatenate([jnp.concatenate(o_cols[qb], axis=1) for qb in range(qb0, qb1)], axis=0)
        y = jnp.dot((o_all * sg_ref[0, rows, :].astype(F32)).astype(BF16), w_ref[...],
                    preferred_element_type=F32)
        x2 = x_ref[0, rows, :] + gate * y
        ms = jnp.mean(x2 * x2, axis=-1, keepdims=True)
        o_ref[0, rows, :] = x2 * lax.rsqrt(ms + NORM_EPS) * fg_ref[...]

    pending = [scores(*ch) for ch in chains[:SCORE_LOOKAHEAD]]
    for idx, (qb, kh) in enumerate(chains):
        if idx + SCORE_LOOKAHEAD < len(chains):
            pending.append(scores(*chains[idx + SCORE_LOOKAHEAD]))
        o_cols[qb] += softmax_pv(qb, kh, *pending.pop(0))
        if kh == C_KV_HEADS - 1 and qb % OUT_PROJ_BLOCKS == OUT_PROJ_BLOCKS - 1:
            finish(qb + 1 - OUT_PROJ_BLOCKS, qb + 1)


def _attn_out_call(sink, q, k, v, kx, vx, sg, xs, mod, w_out, fg):
    bsz, seq_len, _ = xs.shape
    nblk = seq_len // Q_BLOCK
    nq = ATTN_Q_BLOCKS
    nstep = nblk // nq
    lc = kx.shape[1]
    assert lc % SOFTMAX_ROWS == 0 and Q_BLOCK % SOFTMAX_ROWS == 0 and nq % OUT_PROJ_BLOCKS == 0
    cur = lambda w: pl.BlockSpec((1, nq * Q_BLOCK, w), lambda b, n: (b, n, 0))
    prev = lambda w: pl.BlockSpec((1, Q_BLOCK, w), lambda b, n: (b, jnp.maximum(nq * n - 1, 0), 0))
    nxt = lambda w: pl.BlockSpec((1, Q_BLOCK, w), lambda b, n: (b, jnp.minimum(nq * n + nq, nblk - 1), 0))
    ctx_blk = lambda w: pl.BlockSpec((1, lc, w), lambda b, n: (b, 0, 0))
    kw = 2 * C_KV_W
    vcur = pl.BlockSpec((1, C_KV_W, nq * Q_BLOCK), lambda b, n: (b, 0, n))
    vprev = pl.BlockSpec((1, C_KV_W, Q_BLOCK), lambda b, n: (b, 0, jnp.maximum(nq * n - 1, 0)))
    vnxt = pl.BlockSpec((1, C_KV_W, Q_BLOCK), lambda b, n: (b, 0, jnp.minimum(nq * n + nq, nblk - 1)))
    vctx = pl.BlockSpec((1, C_KV_W, lc), lambda b, n: (b, 0, 0))
    sink_t = jnp.repeat(sink.reshape(C_KV_HEADS, 1, C_GROUP), Q_BLOCK, axis=2)
    return pl.pallas_call(
        _attn_out_kernel,
        grid=(bsz, nstep),
        in_specs=[
            pl.BlockSpec((C_KV_HEADS, 1, C_GROUP * Q_BLOCK), lambda b, n: (0, 0, 0)),
            cur(C_Q_W), prev(kw), cur(kw), nxt(kw), vprev, vcur, vnxt, ctx_blk(kw), vctx,
            cur(C_Q_W), cur(D_MODEL),
            pl.BlockSpec((1, 1, 3 * D_MODEL), lambda b, n: (b, 0, 0)),
            pl.BlockSpec((C_Q_W, D_MODEL), lambda b, n: (0, 0)),
            pl.BlockSpec((1, D_MODEL), lambda b, n: (0, 0)),
        ],
        out_specs=cur(D_MODEL),
        out_shape=jax.ShapeDtypeStruct((bsz, seq_len, D_MODEL), F32),
        compiler_params=_cparams(("parallel", "arbitrary")),
        name="attn_out",
    )(sink_t, q, k, k, k, v, v, v, kx, vx, sg, xs, mod, w_out, fg)


def kernel(x, c, ctx, c_ctx, norm_g, ada_w, ada_b, w_in_ab, v_norm_g, spatial_w, spatial_b, w_out_ab,
           w_in_c, sink_logit, w_out_c, final_g):
    bsz, seq_len, _ = x.shape
    depth = ada_w.shape[0]
    assert depth == 2 and bsz + 1 <= MOD_ROWS
    ctx_row = bsz

    cc = jnp.concatenate([c, c_ctx[None, :], jnp.zeros((MOD_ROWS - bsz - 1, D_MODEL), F32)], axis=0)
    mod = _mod_call(cc, ada_w, ada_b)
    mod0 = mod[0].reshape(MOD_ROWS, 1, 3 * D_MODEL)
    mod1 = mod[1].reshape(MOD_ROWS, 1, 3 * D_MODEL)

    cs = jnp.asarray(_channel_dft_matrix())
    ng0 = norm_g[0].reshape(1, D_MODEL)
    ng1 = norm_g[1].reshape(1, D_MODEL)
    w_in0 = w_in_ab[0].astype(BF16)
    w_out0 = w_out_ab[0].astype(BF16)
    vg = v_norm_g[0].reshape(1, A_W)
    sw = spatial_w[0].astype(BF16)
    sb = jnp.broadcast_to(spatial_b[0][:, :, None], (A_HEADS, CHUNK, A_HEAD_DIM))

    x1 = _mixer_ab_layer(x, mod0, None, ng0, w_in0, vg, sw, sb, w_out0, cs, tl2=256, tk=256)
    ctx1 = _mixer_ab_layer(ctx, mod0, ctx_row, ng0, w_in0, vg, sw, sb, w_out0, cs,
                           tl2=ctx.shape[1] // RADIX, tk=ctx.shape[1] // RADIX)

    w_in1 = w_in_c[0].astype(BF16)
    w_out1 = w_out_c[0].astype(BF16)
    cos_t, sin_t = _rope_tables(seq_len)
    q, k, v, sg = _in_c_call(x1, mod1, ng1, w_in1, jnp.asarray(cos_t), jnp.asarray(sin_t), tm=1024)
    kx, vx = _ctx_kv_call(ctx1, mod1, ctx_row, ng1, w_in1[:, C_Q_W:C_Q_W + 2 * C_KV_W])
    return _attn_out_call(sink_logit[0], q, k, v, kx, vx, sg, x1, mod1, w_out1,
                          final_g.reshape(1, D_MODEL))
```

```python
import functools
import math

import numpy as np
import jax
import jax.numpy as jnp
from jax import lax
from jax.experimental import pallas as pl
from jax.experimental.pallas import tpu as pltpu

F32 = jnp.float32
BF16 = jnp.bfloat16

D_MODEL = 1024
GRID_W = 64
CHUNK = 128
A_HEADS = 4
A_HEAD_DIM = 128
A_W = A_HEADS * A_HEAD_DIM
B_GROUPS = 4
B_GROUP_DIM = 128
B_W = B_GROUPS * B_GROUP_DIM
AB_IN = 3 * A_W + 2 * B_W
C_HEADS = 16
C_KV_HEADS = 4
C_GROUP = C_HEADS // C_KV_HEADS
C_HEAD_DIM = 64
C_Q_W = C_HEADS * C_HEAD_DIM
C_KV_W = C_KV_HEADS * C_HEAD_DIM
C_IN = 2 * C_Q_W + 2 * C_KV_W
WINDOW = 128
Q_BLOCK = 128
ROPE_BASE = 10000.0
NORM_EPS = 1e-6
NEG_INF = -1e30
LOG2E = math.log2(math.e)
RADIX = 4
MOD_ROWS = 16
SOFTMAX_ROWS = 64
SCORE_LOOKAHEAD = 2
OUT_PROJ_COLS = 256
ATTN_Q_BLOCKS = 4
OUT_PROJ_BLOCKS = 2
V7X_VMEM_LIMIT = 56 * 1024 * 1024


def _silu(x):
    return x * (1.0 / (1.0 + jnp.exp(-x)))


def _cparams(sem):
    return pltpu.CompilerParams(dimension_semantics=sem, vmem_limit_bytes=V7X_VMEM_LIMIT)


def _channel_dft_matrix():
    n = np.arange(B_GROUP_DIM)
    ang = 2.0 * np.pi * np.outer(n, n) / B_GROUP_DIM
    return np.concatenate([np.cos(ang), np.sin(ang)], axis=1).astype(np.float32)


def _position_dft_matrix(seq_len):
    n2 = seq_len // RADIX
    idx = np.arange(n2)
    ang = 2.0 * np.pi * (np.outer(idx, idx) % n2) / n2
    norm = 1.0 / math.sqrt(seq_len * B_GROUP_DIM)
    return np.concatenate([np.cos(ang) * norm, -np.sin(ang) * norm], axis=1).astype(np.float32)


def _twiddle_tables(seq_len):
    n2 = seq_len // RADIX
    l2 = np.arange(n2)[None, :, None]
    k1 = np.arange(RADIX)[:, None, None]
    ang = 2.0 * np.pi * ((l2 * k1) % seq_len) / seq_len
    ang = np.broadcast_to(ang, (RADIX, n2, 128))
    return np.cos(ang).astype(np.float32), np.sin(ang).astype(np.float32)


def _rope_tables(seq_len):
    t = np.arange(seq_len)
    row = (t // GRID_W).astype(np.float64)
    col = (t % GRID_W).astype(np.float64)
    lane = np.arange(128)
    dd = lane % C_HEAD_DIM
    nf = C_HEAD_DIM // 4
    inv = ROPE_BASE ** (-(dd % nf).astype(np.float64) / nf)
    pos = np.where((dd < C_HEAD_DIM // 2)[None, :], row[:, None], col[:, None])
    ang = pos * inv[None, :]
    sign = np.where((dd % (2 * nf)) < nf, -1.0, 1.0)[None, :]
    return np.cos(ang).astype(np.float32), (np.sin(ang) * sign).astype(np.float32)


def _mod_kernel(c_ref, w_ref, b_ref, o_ref):
    s = _silu(c_ref[...]).astype(BF16)
    o_ref[0] = jnp.dot(s, w_ref[0].astype(BF16), preferred_element_type=F32) + b_ref[0]


def _mod_call(cc, ada_w, ada_b):
    depth = ada_w.shape[0]
    tn = 1024
    return pl.pallas_call(
        _mod_kernel,
        grid=(depth, 3 * D_MODEL // tn),
        in_specs=[
            pl.BlockSpec((MOD_ROWS, D_MODEL), lambda l, j: (0, 0)),
            pl.BlockSpec((1, D_MODEL, tn), lambda l, j: (l, 0, j)),
            pl.BlockSpec((1, 1, tn), lambda l, j: (l, 0, j)),
        ],
        out_specs=pl.BlockSpec((1, MOD_ROWS, tn), lambda l, j: (l, 0, j)),
        out_shape=jax.ShapeDtypeStruct((depth, MOD_ROWS, 3 * D_MODEL), F32),
        compiler_params=_cparams(("arbitrary", "arbitrary")),
        name="adaln_mod",
    )(cc, ada_w, ada_b.reshape(depth, 1, 3 * D_MODEL))


def _modulated_norm(x, mod_row, g):
    shift = mod_row[:, :D_MODEL]
    scale = mod_row[:, D_MODEL:2 * D_MODEL]
    ms = jnp.mean(x * x, axis=-1, keepdims=True)
    h = x * lax.rsqrt(ms + NORM_EPS) * g
    return h * (1.0 + scale) + shift


def _in_ab_kernel(x_ref, mod_ref, ng_ref, w_ref, vg_ref, sw_ref, sb_ref, cs_ref, twc_ref, tws_ref,
                  ya_ref, sgb_ref, uv_ref, *, tl2, ts):
    rows = RADIX * ts
    piece = min(CHUNK, ts)
    nchunk = rows // CHUNK
    assert nchunk % 2 == 0 and tl2 % ts == 0
    cs = cs_ref[...].astype(BF16)

    def project(o):
        x = x_ref[0, :, o:o + ts, :].reshape(rows, D_MODEL)
        h = _modulated_norm(x, mod_ref[0], ng_ref[...])
        return jnp.dot(h.astype(BF16), w_ref[...], preferred_element_type=F32)

    def mix(o, z):
        v = z[:, A_W:2 * A_W]
        mu = jnp.mean(v, axis=-1, keepdims=True)
        vc = v - mu
        var = jnp.mean(vc * vc, axis=-1, keepdims=True)
        vn = (vc * lax.rsqrt(var + NORM_EPS) * vg_ref[...]).astype(BF16)
        for hd in range(A_HEADS):
            c0 = hd * A_HEAD_DIM
            for cp in range(nchunk // 2):
                ra, rb = 2 * cp * CHUNK, (2 * cp + 1) * CHUNK
                vpair = jnp.concatenate([vn[ra:ra + CHUNK, c0:c0 + A_HEAD_DIM],
                                         vn[rb:rb + CHUNK, c0:c0 + A_HEAD_DIM]], axis=1)
                sv2 = jnp.dot(sw_ref[hd], vpair, preferred_element_type=F32)
                for half, r0 in enumerate((ra, rb)):
                    sv = sv2[:, half * A_HEAD_DIM:(half + 1) * A_HEAD_DIM] + sb_ref[hd]
                    u = z[r0:r0 + CHUNK, c0:c0 + A_HEAD_DIM]
                    ga = z[r0:r0 + CHUNK, 2 * A_W + c0:2 * A_W + c0 + A_HEAD_DIM]
                    ya = (u * sv * _silu(ga)).astype(BF16)
                    for p0 in range(0, CHUNK, piece):
                        l1, off = divmod(r0 + p0, ts)
                        ya_ref[0, l1, o + off:o + off + piece, c0:c0 + A_HEAD_DIM] = ya[p0:p0 + piece]

        gb = z[:, 3 * A_W + B_W:]
        sgb_ref[0, :, o:o + ts, :] = _silu(gb).astype(BF16).reshape(RADIX, ts, B_W)

        xb = z[:, 3 * A_W:3 * A_W + B_W].astype(BF16)
        for g in range(B_GROUPS):
            c0 = g * B_GROUP_DIM
            ps, qs = [], []
            for j in range(RADIX):
                pq = jnp.dot(xb[j * ts:(j + 1) * ts, c0:c0 + B_GROUP_DIM], cs,
                             preferred_element_type=F32)
                ps.append(pq[:, :B_GROUP_DIM])
                qs.append(pq[:, B_GROUP_DIM:])
            p02, p13 = ps[0] - ps[2], ps[1] - ps[3]
            q02, q13 = qs[0] - qs[2], qs[1] - qs[3]
            pe, po = ps[0] + ps[2], ps[1] + ps[3]
            qe, qo = qs[0] + qs[2], qs[1] + qs[3]
            us = [pe + po, p02 - q13, pe - po, p02 + q13]
            vs = [qe + qo, q02 + p13, qe - qo, q02 - p13]
            for k1 in range(RADIX):
                if k1 == 0:
                    ut, vt = us[0], vs[0]
                else:
                    tc, tsn = twc_ref[k1, o:o + ts, :], tws_ref[k1, o:o + ts, :]
                    ut = us[k1] * tc - vs[k1] * tsn
                    vt = us[k1] * tsn + vs[k1] * tc
                uv_ref[0, k1, 0, o:o + ts, c0:c0 + B_GROUP_DIM] = ut.astype(BF16)
                uv_ref[0, k1, 1, o:o + ts, c0:c0 + B_GROUP_DIM] = vt.astype(BF16)

    for o in range(0, tl2, ts):
        mix(o, project(o))


def _in_ab_call(xs, mod, mod_row, ng, w_in, vg, sw, sb, cs, twc, tws, tl2):
    bsz, seq_len, _ = xs.shape
    n2 = seq_len // RADIX
    x4 = xs.reshape(bsz, RADIX, n2, D_MODEL)
    row_of = (lambda b: b) if mod_row is None else (lambda b: mod_row)
    const2 = lambda b, i: (0, 0)
    const3 = lambda b, i: (0, 0, 0)
    return pl.pallas_call(
        functools.partial(_in_ab_kernel, tl2=tl2, ts=tl2),
        grid=(bsz, n2 // tl2),
        in_specs=[
            pl.BlockSpec((1, RADIX, tl2, D_MODEL), lambda b, i: (b, 0, i, 0)),
            pl.BlockSpec((1, 1, 3 * D_MODEL), lambda b, i: (row_of(b), 0, 0)),
            pl.BlockSpec((1, D_MODEL), const2),
            pl.BlockSpec((D_MODEL, AB_IN), const2),
            pl.BlockSpec((1, A_W), const2),
            pl.BlockSpec((A_HEADS, CHUNK, CHUNK), const3),
            pl.BlockSpec((A_HEADS, CHUNK, A_HEAD_DIM), const3),
            pl.BlockSpec((B_GROUP_DIM, 2 * B_GROUP_DIM), const2),
            pl.BlockSpec((RADIX, tl2, 128), lambda b, i: (0, i, 0)),
            pl.BlockSpec((RADIX, tl2, 128), lambda b, i: (0, i, 0)),
        ],
        out_specs=[
            pl.BlockSpec((1, RADIX, tl2, A_W), lambda b, i: (b, 0, i, 0)),
            pl.BlockSpec((1, RADIX, tl2, B_W), lambda b, i: (b, 0, i, 0)),
            pl.BlockSpec((1, RADIX, 2, tl2, B_W), lambda b, i: (b, 0, 0, i, 0)),
        ],
        out_shape=[
            jax.ShapeDtypeStruct((bsz, RADIX, n2, A_W), BF16),
            jax.ShapeDtypeStruct((bsz, RADIX, n2, B_W), BF16),
            jax.ShapeDtypeStruct((bsz, RADIX, 2, n2, B_W), BF16),
        ],
        compiler_params=_cparams(("parallel", "arbitrary")),
        name="in_ab",
    )(x4, mod, ng, w_in, vg, sw, sb, cs, twc, tws)


def _dft_out_kernel(g_ref, uv_ref, ya_ref, sgb_ref, x_ref, mod_ref, w_ref, o_ref, f_ref, *, tk):
    r0 = pl.multiple_of(pl.program_id(1) * tk, tk)
    gmat = g_ref[pl.ds(r0, tk), :].astype(BF16)
    for k1 in range(RADIX):
        f = jnp.dot(gmat, uv_ref[0, k1], preferred_element_type=F32)
        for g in range(B_GROUPS):
            f_ref[g, pl.ds(k1, tk, stride=RADIX), :] = f[:, g * B_GROUP_DIM:(g + 1) * B_GROUP_DIM]
    fnat = jnp.concatenate([f_ref[g] for g in range(B_GROUPS)], axis=1)
    yb = (fnat * sgb_ref[0].astype(F32)).astype(BF16)
    y = jnp.dot(ya_ref[0], w_ref[:A_W], preferred_element_type=F32)
    y = y + jnp.dot(yb, w_ref[A_W:], preferred_element_type=F32)
    gate = mod_ref[0][:, 2 * D_MODEL:]
    o_ref[0] = x_ref[0] + gate * y


def _dft_out_call(gmat, uv, ya, sgb, xs, mod, mod_row, w_out, tk):
    bsz, seq_len, _ = xs.shape
    n2 = seq_len // RADIX
    rows = RADIX * tk
    row_of = (lambda b: b) if mod_row is None else (lambda b: mod_row)
    nat = lambda w: pl.BlockSpec((1, rows, w), lambda b, i: (b, i, 0))
    return pl.pallas_call(
        functools.partial(_dft_out_kernel, tk=tk),
        grid=(bsz, n2 // tk),
        in_specs=[
            pl.BlockSpec((n2, 2 * n2), lambda b, i: (0, 0), pipeline_mode=pl.Buffered(1)),
            pl.BlockSpec((1, RADIX, 2 * n2, B_W), lambda b, i: (b, 0, 0, 0)),
            nat(A_W), nat(B_W), nat(D_MODEL),
            pl.BlockSpec((1, 1, 3 * D_MODEL), lambda b, i: (row_of(b), 0, 0)),
            pl.BlockSpec((A_W + B_W, D_MODEL), lambda b, i: (0, 0), pipeline_mode=pl.Buffered(1)),
        ],
        out_specs=nat(D_MODEL),
        out_shape=jax.ShapeDtypeStruct((bsz, seq_len, D_MODEL), F32),
        scratch_shapes=[pltpu.VMEM((B_GROUPS, rows, B_GROUP_DIM), F32)],
        compiler_params=_cparams(("parallel", "arbitrary")),
        name="dft_out",
    )(gmat, uv.reshape(bsz, RADIX, 2 * n2, B_W), ya.reshape(bsz, seq_len, A_W),
      sgb.reshape(bsz, seq_len, B_W), xs, mod, w_out)


def _mixer_ab_layer(xs, mod, mod_row, ng, w_in, vg, sw, sb, w_out, cs, tl2, tk):
    seq_len = xs.shape[1]
    twc, tws = _twiddle_tables(seq_len)
    gmat = jnp.asarray(_position_dft_matrix(seq_len))
    ya, sgb, uv = _in_ab_call(xs, mod, mod_row, ng, w_in, vg, sw, sb, cs,
                              jnp.asarray(twc), jnp.asarray(tws), tl2)
    return _dft_out_call(gmat, uv, ya, sgb, xs, mod, mod_row, w_out, tk=tk)


def _rope_block(t, cos, sin_signed, lane_lo):
    nf = C_HEAD_DIM // 4
    swapped = jnp.where(lane_lo, pltpu.roll(t, 128 - nf, axis=1), pltpu.roll(t, nf, axis=1))
    return t * cos + swapped * sin_signed


def _store_dup_heads(k_ref, c, t, lane):
    r = pltpu.roll(t, C_HEAD_DIM, axis=1)
    first = lane < C_HEAD_DIM
    k_ref[0, :, (2 * c) * 128:(2 * c + 1) * 128] = jnp.where(first, t, r).astype(BF16)
    k_ref[0, :, (2 * c + 1) * 128:(2 * c + 2) * 128] = jnp.where(first, r, t).astype(BF16)


def _in_c_kernel(x_ref, mod_ref, ng_ref, w_ref, cos_ref, sin_ref, q_ref, k_ref, v_ref, sg_ref):
    h = _modulated_norm(x_ref[0], mod_ref[0], ng_ref[...])
    z = jnp.dot(h.astype(BF16), w_ref[...], preferred_element_type=F32)
    cos = cos_ref[...]
    sin = sin_ref[...]
    lane = lax.broadcasted_iota(jnp.int32, cos.shape, 1)
    lane_lo = (lane % (C_HEAD_DIM // 2)) < (C_HEAD_DIM // 4)
    qscale = C_HEAD_DIM ** -0.5 * LOG2E
    for c in range(C_Q_W // 128):
        t = _rope_block(z[:, c * 128:(c + 1) * 128], cos, sin, lane_lo)
        q_ref[0, :, c * 128:(c + 1) * 128] = (t * qscale).astype(BF16)
    for c in range(C_KV_W // 128):
        t = _rope_block(z[:, C_Q_W + c * 128:C_Q_W + (c + 1) * 128], cos, sin, lane_lo)
        _store_dup_heads(k_ref, c, t, lane)
    v_ref[0] = z[:, C_Q_W + C_KV_W:C_Q_W + 2 * C_KV_W].T.astype(BF16)
    sg_ref[0] = _silu(z[:, C_Q_W + 2 * C_KV_W:]).astype(BF16)


def _in_c_call(xs, mod, ng, w_in, cos_t, sin_t, tm):
    bsz, seq_len, _ = xs.shape
    const2 = lambda b, i: (0, 0)
    row_blk = lambda w: pl.BlockSpec((1, tm, w), lambda b, i: (b, i, 0))
    return pl.pallas_call(
        _in_c_kernel,
        grid=(bsz, seq_len // tm),
        in_specs=[
            row_blk(D_MODEL),
            pl.BlockSpec((1, 1, 3 * D_MODEL), lambda b, i: (b, 0, 0)),
            pl.BlockSpec((1, D_MODEL), const2),
            pl.BlockSpec((D_MODEL, C_IN), const2),
            pl.BlockSpec((tm, 128), lambda b, i: (i, 0)),
            pl.BlockSpec((tm, 128), lambda b, i: (i, 0)),
        ],
        out_specs=[row_blk(C_Q_W), row_blk(2 * C_KV_W),
                   pl.BlockSpec((1, C_KV_W, tm), lambda b, i: (b, 0, i)), row_blk(C_Q_W)],
        out_shape=[
            jax.ShapeDtypeStruct((bsz, seq_len, C_Q_W), BF16),
            jax.ShapeDtypeStruct((bsz, seq_len, 2 * C_KV_W), BF16),
            jax.ShapeDtypeStruct((bsz, C_KV_W, seq_len), BF16),
            jax.ShapeDtypeStruct((bsz, seq_len, C_Q_W), BF16),
        ],
        compiler_params=_cparams(("parallel", "arbitrary")),
        name="in_c",
    )(xs, mod, ng, w_in, cos_t, sin_t)


def _ctx_kv_kernel(x_ref, mod_ref, ng_ref, w_ref, k_ref, v_ref):
    h = _modulated_norm(x_ref[0], mod_ref[0], ng_ref[...])
    z = jnp.dot(h.astype(BF16), w_ref[...], preferred_element_type=F32)
    lane = lax.broadcasted_iota(jnp.int32, (z.shape[0], 128), 1)
    for c in range(C_KV_W // 128):
        _store_dup_heads(k_ref, c, z[:, c * 128:(c + 1) * 128], lane)
    v_ref[0] = z[:, C_KV_W:].T.astype(BF16)


def _ctx_kv_call(ctx, mod, mod_row, ng, w_kv):
    bsz, lc, _ = ctx.shape
    return pl.pallas_call(
        _ctx_kv_kernel,
        grid=(bsz,),
        in_specs=[
            pl.BlockSpec((1, lc, D_MODEL), lambda b: (b, 0, 0)),
            pl.BlockSpec((1, 1, 3 * D_MODEL), lambda b: (mod_row, 0, 0)),
            pl.BlockSpec((1, D_MODEL), lambda b: (0, 0)),
            pl.BlockSpec((D_MODEL, 2 * C_KV_W), lambda b: (0, 0)),
        ],
        out_specs=[pl.BlockSpec((1, lc, 2 * C_KV_W), lambda b: (b, 0, 0)),
                   pl.BlockSpec((1, C_KV_W, lc), lambda b: (b, 0, 0))],
        out_shape=[jax.ShapeDtypeStruct((bsz, lc, 2 * C_KV_W), BF16),
                   jax.ShapeDtypeStruct((bsz, C_KV_W, lc), BF16)],
        compiler_params=_cparams(("parallel",)),
        name="ctx_kv",
    )(ctx, mod, ng, w_kv)


def _attn_out_kernel(sink_ref, q_ref, kp_ref, kc_ref, kn_ref, vp_ref, vc_ref, vn_ref, kx_ref, vx_ref,
                     sg_ref, x_ref, mod_ref, w_ref, fg_ref, o_ref):
    n = pl.program_id(1)
    last = pl.num_programs(1) - 1
    gq = C_GROUP * Q_BLOCK
    kj = lax.broadcasted_iota(jnp.int32, (Q_BLOCK, Q_BLOCK), 0)
    qi = lax.broadcasted_iota(jnp.int32, (Q_BLOCK, Q_BLOCK), 1)
    bias_before = jnp.where(kj >= qi, 0.0, NEG_INF).astype(F32)
    bias_after = jnp.where(kj <= qi, 0.0, NEG_INF).astype(F32)
    edge_first = jnp.where(n > 0, 0.0, NEG_INF).astype(F32)
    edge_last = jnp.where(n < last, 0.0, NEG_INF).astype(F32)
    lane = lax.broadcasted_iota(jnp.int32, (Q_BLOCK, 128), 1)
    first = lane < C_HEAD_DIM
    zero = jnp.zeros((Q_BLOCK, 128), BF16)
    row_k = lax.broadcasted_iota(jnp.int32, (128, kx_ref.shape[1] + 3 * Q_BLOCK), 0)
    nq = q_ref.shape[1] // Q_BLOCK

    def window(prev_ref, cur_ref, next_ref, j, lanes):
        if j < 0:
            return prev_ref[0, :, lanes]
        if j >= nq:
            return next_ref[0, :, lanes]
        return cur_ref[0, j * Q_BLOCK:(j + 1) * Q_BLOCK, lanes]

    def scores(qb, kh):
        kl = slice(kh * 128, (kh + 1) * 128)
        kblocks = [kx_ref[0, :, kl]] + [window(kp_ref, kc_ref, kn_ref, j, kl) for j in (qb - 1, qb, qb + 1)]
        biases = (None, (bias_before, edge_first if qb == 0 else None), None,
                  (bias_after, edge_last if qb == nq - 1 else None))
        rows_q = slice(qb * Q_BLOCK, (qb + 1) * Q_BLOCK)
        q4 = []
        for c in range(2):
            qv = q_ref[0, rows_q, kh * 256 + c * 128:kh * 256 + (c + 1) * 128]
            q4 += [jnp.where(first, qv, zero), jnp.where(first, zero, qv)]
        q4 = jnp.concatenate(q4, axis=0)
        k2 = jnp.concatenate(kblocks, axis=0)
        st = lax.dot_general(k2, q4, (((1,), (1,)), ((), ())), preferred_element_type=F32)
        blocks, m8, r = [], None, 0
        for kb, bias in zip(kblocks, biases):
            sb = st[r:r + kb.shape[0]]
            r += kb.shape[0]
            if bias is not None:
                mask_bias, edge = bias
                cols = [sb[:, j * Q_BLOCK:(j + 1) * Q_BLOCK] + mask_bias for j in range(C_GROUP)]
                sb = jnp.concatenate(cols, axis=1)
                if edge is not None:
                    sb = sb + edge
            blocks.append(sb)
            mb = jnp.max(sb.reshape(sb.shape[0] // 8, 8, gq), axis=0)
            m8 = mb if m8 is None else jnp.maximum(m8, mb)
        return blocks, m8

    def softmax_pv(qb, kh, blocks, m8):
        sink2 = sink_ref[kh] * LOG2E
        m = jnp.maximum(jnp.max(m8, axis=0, keepdims=True), sink2)
        pts = []
        for blk in blocks:
            for r in range(0, blk.shape[0], SOFTMAX_ROWS):
                pts.append(jnp.exp2(blk[r:r + SOFTMAX_ROWS] - m).astype(BF16))
        pt = jnp.concatenate(pts, axis=0)
        vrows = slice((kh // 2) * 128, (kh // 2 + 1) * 128)
        vblocks = [vx_ref[0, vrows, :]]
        for j in (qb - 1, qb, qb + 1):
            if j < 0:
                vblocks.append(vp_ref[0, vrows, :])
            elif j >= nq:
                vblocks.append(vn_ref[0, vrows, :])
            else:
                vblocks.append(vc_ref[0, vrows, j * Q_BLOCK:(j + 1) * Q_BLOCK])
        vpair = jnp.concatenate(vblocks, axis=1)
        own = (row_k < C_HEAD_DIM) if kh % 2 == 0 else (row_k >= C_HEAD_DIM)
        vsum = jnp.where(own, vpair, jnp.ones_like(vpair))
        ot = jnp.dot(vsum, pt, preferred_element_type=F32)
        r0 = (kh % 2) * C_HEAD_DIM
        r1 = C_HEAD_DIM - r0
        denom = ot[r1:r1 + C_HEAD_DIM] + jnp.exp2(sink2 - m)
        ot = ot[r0:r0 + C_HEAD_DIM] / denom
        cols = []
        for c in range(2):
            pair = jnp.concatenate([ot[:, (2 * c) * Q_BLOCK:(2 * c + 1) * Q_BLOCK],
                                    ot[:, (2 * c + 1) * Q_BLOCK:(2 * c + 2) * Q_BLOCK]], axis=0)
            cols.append(pair.T)
        return cols

    chains = [(qb, kh) for qb in range(nq) for kh in range(C_KV_HEADS)]
    o_cols = {qb: [] for qb in range(nq)}
    gate = mod_ref[0][:, 2 * D_MODEL:]

    def finish_tasks(qb0, qb1):
        rows = slice(qb0 * Q_BLOCK, qb1 * Q_BLOCK)
        state = {}

        def gate_values():
            o_all = jnp.concatenate([jnp.concatenate(o_cols[qb], axis=1) for qb in range(qb0, qb1)], axis=0)
            state["o"] = (o_all * sg_ref[0, rows, :].astype(F32)).astype(BF16)
            state["y"] = []

        def project(c0):
            state["y"].append(jnp.dot(state["o"], w_ref[:, c0:c0 + OUT_PROJ_COLS],
                                      preferred_element_type=F32))

        def residual_norm():
            y = jnp.concatenate(state["y"], axis=1)
            x2 = x_ref[0, rows, :] + gate * y
            ms = jnp.mean(x2 * x2, axis=-1, keepdims=True)
            o_ref[0, rows, :] = x2 * lax.rsqrt(ms + NORM_EPS) * fg_ref[...]

        tasks = [gate_values]
        tasks += [functools.partial(project, c0) for c0 in range(0, D_MODEL, OUT_PROJ_COLS)]
        return tasks + [residual_norm]

    pending = [scores(*ch) for ch in chains[:SCORE_LOOKAHEAD]]
    deferred = []
    for idx, (qb, kh) in enumerate(chains):
        if idx + SCORE_LOOKAHEAD < len(chains):
            pending.append(scores(*chains[idx + SCORE_LOOKAHEAD]))
        o_cols[qb] += softmax_pv(qb, kh, *pending.pop(0))
        if deferred:
            deferred.pop(0)()
        if kh == C_KV_HEADS - 1 and qb % OUT_PROJ_BLOCKS == OUT_PROJ_BLOCKS - 1:
            deferred += finish_tasks(qb + 1 - OUT_PROJ_BLOCKS, qb + 1)
    for task in deferred:
        task()


def _attn_out_call(sink, q, k, v, kx, vx, sg, xs, mod, w_out, fg):
    bsz, seq_len, _ = xs.shape
    nblk = seq_len // Q_BLOCK
    nq = ATTN_Q_BLOCKS
    nstep = nblk // nq
    lc = kx.shape[1]
    assert lc % SOFTMAX_ROWS == 0 and Q_BLOCK % SOFTMAX_ROWS == 0 and nq % OUT_PROJ_BLOCKS == 0
    cur = lambda w: pl.BlockSpec((1, nq * Q_BLOCK, w), lambda b, n: (b, n, 0))
    prev = lambda w: pl.BlockSpec((1, Q_BLOCK, w), lambda b, n: (b, jnp.maximum(nq * n - 1, 0), 0))
    nxt = lambda w: pl.BlockSpec((1, Q_BLOCK, w), lambda b, n: (b, jnp.minimum(nq * n + nq, nblk - 1), 0))
    ctx_blk = lambda w: pl.BlockSpec((1, lc, w), lambda b, n: (b, 0, 0))
    kw = 2 * C_KV_W
    vcur = pl.BlockSpec((1, C_KV_W, nq * Q_BLOCK), lambda b, n: (b, 0, n))
    vprev = pl.BlockSpec((1, C_KV_W, Q_BLOCK), lambda b, n: (b, 0, jnp.maximum(nq * n - 1, 0)))
    vnxt = pl.BlockSpec((1, C_KV_W, Q_BLOCK), lambda b, n: (b, 0, jnp.minimum(nq * n + nq, nblk - 1)))
    vctx = pl.BlockSpec((1, C_KV_W, lc), lambda b, n: (b, 0, 0))
    sink_t = jnp.repeat(sink.reshape(C_KV_HEADS, 1, C_GROUP), Q_BLOCK, axis=2)
    return pl.pallas_call(
        _attn_out_kernel,
        grid=(bsz, nstep),
        in_specs=[
            pl.BlockSpec((C_KV_HEADS, 1, C_GROUP * Q_BLOCK), lambda b, n: (0, 0, 0)),
            cur(C_Q_W), prev(kw), cur(kw), nxt(kw), vprev, vcur, vnxt, ctx_blk(kw), vctx,
            cur(C_Q_W), cur(D_MODEL),
            pl.BlockSpec((1, 1, 3 * D_MODEL), lambda b, n: (b, 0, 0)),
            pl.BlockSpec((C_Q_W, D_MODEL), lambda b, n: (0, 0)),
            pl.BlockSpec((1, D_MODEL), lambda b, n: (0, 0)),
        ],
        out_specs=cur(D_MODEL),
        out_shape=jax.ShapeDtypeStruct((bsz, seq_len, D_MODEL), F32),
        compiler_params=_cparams(("parallel", "arbitrary")),
        name="attn_out",
    )(sink_t, q, k, k, k, v, v, v, kx, vx, sg, xs, mod, w_out, fg)


def kernel(x, c, ctx, c_ctx, norm_g, ada_w, ada_b, w_in_ab, v_norm_g, spatial_w, spatial_b, w_out_ab,
           w_in_c, sink_logit, w_out_c, final_g):
    bsz, seq_len, _ = x.shape
    depth = ada_w.shape[0]
    assert depth == 2 and bsz + 1 <= MOD_ROWS
    ctx_row = bsz

    cc = jnp.concatenate([c, c_ctx[None, :], jnp.zeros((MOD_ROWS - bsz - 1, D_MODEL), F32)], axis=0)
    mod = _mod_call(cc, ada_w, ada_b)
    mod0 = mod[0].reshape(MOD_ROWS, 1, 3 * D_MODEL)
    mod1 = mod[1].reshape(MOD_ROWS, 1, 3 * D_MODEL)

    cs = jnp.asarray(_channel_dft_matrix())
    ng0 = norm_g[0].reshape(1, D_MODEL)
    ng1 = norm_g[1].reshape(1, D_MODEL)
    w_in0 = w_in_ab[0].astype(BF16)
    w_out0 = w_out_ab[0].astype(BF16)
    vg = v_norm_g[0].reshape(1, A_W)
    sw = spatial_w[0].astype(BF16)
    sb = jnp.broadcast_to(spatial_b[0][:, :, None], (A_HEADS, CHUNK, A_HEAD_DIM))

    x1 = _mixer_ab_layer(x, mod0, None, ng0, w_in0, vg, sw, sb, w_out0, cs, tl2=256, tk=256)
    ctx1 = _mixer_ab_layer(ctx, mod0, ctx_row, ng0, w_in0, vg, sw, sb, w_out0, cs,
                           tl2=ctx.shape[1] // RADIX, tk=ctx.shape[1] // RADIX)

    w_in1 = w_in_c[0].astype(BF16)
    w_out1 = w_out_c[0].astype(BF16)
    cos_t, sin_t = _rope_tables(seq_len)
    q, k, v, sg = _in_c_call(x1, mod1, ng1, w_in1, jnp.asarray(cos_t), jnp.asarray(sin_t), tm=1024)
    kx, vx = _ctx_kv_call(ctx1, mod1, ctx_row, ng1, w_in1[:, C_Q_W:C_Q_W + 2 * C_KV_W])
    return _attn_out_call(sink_logit[0], q, k, v, kx, vx, sg, x1, mod1, w_out1,
                          final_g.reshape(1, D_MODEL))
```

```python
import functools
import math

import numpy as np
import jax
import jax.numpy as jnp
from jax import lax
from jax.experimental import pallas as pl
from jax.experimental.pallas import tpu as pltpu

F32 = jnp.float32
BF16 = jnp.bfloat16

D_MODEL = 1024
GRID_W = 64
CHUNK = 128
A_HEADS = 4
A_HEAD_DIM = 128
A_W = A_HEADS * A_HEAD_DIM
B_GROUPS = 4
B_GROUP_DIM = 128
B_W = B_GROUPS * B_GROUP_DIM
AB_IN = 3 * A_W + 2 * B_W
C_HEADS = 16
C_KV_HEADS = 4
C_GROUP = C_HEADS // C_KV_HEADS
C_HEAD_DIM = 64
C_Q_W = C_HEADS * C_HEAD_DIM
C_KV_W = C_KV_HEADS * C_HEAD_DIM
C_IN = 2 * C_Q_W + 2 * C_KV_W
WINDOW = 128
Q_BLOCK = 128
ROPE_BASE = 10000.0
NORM_EPS = 1e-6
NEG_INF = -1e30
LOG2E = math.log2(math.e)
RADIX = 4
MOD_ROWS = 16
SOFTMAX_ROWS = 64
SCORE_LOOKAHEAD = 2
OUT_PROJ_COLS = 256
PV_LAG = 1
ATTN_Q_BLOCKS = 4
OUT_PROJ_BLOCKS = 2
V7X_VMEM_LIMIT = 56 * 1024 * 1024


def _silu(x):
    return x * (1.0 / (1.0 + jnp.exp(-x)))


def _cparams(sem):
    return pltpu.CompilerParams(dimension_semantics=sem, vmem_limit_bytes=V7X_VMEM_LIMIT)


def _channel_dft_matrix():
    n = np.arange(B_GROUP_DIM)
    ang = 2.0 * np.pi * np.outer(n, n) / B_GROUP_DIM
    return np.concatenate([np.cos(ang), np.sin(ang)], axis=1).astype(np.float32)


def _position_dft_matrix(seq_len):
    n2 = seq_len // RADIX
    idx = np.arange(n2)
    ang = 2.0 * np.pi * (np.outer(idx, idx) % n2) / n2
    norm = 1.0 / math.sqrt(seq_len * B_GROUP_DIM)
    return np.concatenate([np.cos(ang) * norm, -np.sin(ang) * norm], axis=1).astype(np.float32)


def _twiddle_tables(seq_len):
    n2 = seq_len // RADIX
    l2 = np.arange(n2)[None, :, None]
    k1 = np.arange(RADIX)[:, None, None]
    ang = 2.0 * np.pi * ((l2 * k1) % seq_len) / seq_len
    ang = np.broadcast_to(ang, (RADIX, n2, 128))
    return np.cos(ang).astype(np.float32), np.sin(ang).astype(np.float32)


def _rope_tables(seq_len):
    t = np.arange(seq_len)
    row = (t // GRID_W).astype(np.float64)
    col = (t % GRID_W).astype(np.float64)
    lane = np.arange(128)
    dd = lane % C_HEAD_DIM
    nf = C_HEAD_DIM // 4
    inv = ROPE_BASE ** (-(dd % nf).astype(np.float64) / nf)
    pos = np.where((dd < C_HEAD_DIM // 2)[None, :], row[:, None], col[:, None])
    ang = pos * inv[None, :]
    sign = np.where((dd % (2 * nf)) < nf, -1.0, 1.0)[None, :]
    return np.cos(ang).astype(np.float32), (np.sin(ang) * sign).astype(np.float32)


def _mod_kernel(c_ref, w_ref, b_ref, o_ref):
    s = _silu(c_ref[...]).astype(BF16)
    o_ref[0] = jnp.dot(s, w_ref[0].astype(BF16), preferred_element_type=F32) + b_ref[0]


def _mod_call(cc, ada_w, ada_b):
    depth = ada_w.shape[0]
    tn = 1024
    return pl.pallas_call(
        _mod_kernel,
        grid=(depth, 3 * D_MODEL // tn),
        in_specs=[
            pl.BlockSpec((MOD_ROWS, D_MODEL), lambda l, j: (0, 0)),
            pl.BlockSpec((1, D_MODEL, tn), lambda l, j: (l, 0, j)),
            pl.BlockSpec((1, 1, tn), lambda l, j: (l, 0, j)),
        ],
        out_specs=pl.BlockSpec((1, MOD_ROWS, tn), lambda l, j: (l, 0, j)),
        out_shape=jax.ShapeDtypeStruct((depth, MOD_ROWS, 3 * D_MODEL), F32),
        compiler_params=_cparams(("arbitrary", "arbitrary")),
        name="adaln_mod",
    )(cc, ada_w, ada_b.reshape(depth, 1, 3 * D_MODEL))


def _modulated_norm(x, mod_row, g):
    shift = mod_row[:, :D_MODEL]
    scale = mod_row[:, D_MODEL:2 * D_MODEL]
    ms = jnp.mean(x * x, axis=-1, keepdims=True)
    h = x * lax.rsqrt(ms + NORM_EPS) * g
    return h * (1.0 + scale) + shift


def _in_ab_kernel(x_ref, mod_ref, ng_ref, w_ref, vg_ref, sw_ref, sb_ref, cs_ref, twc_ref, tws_ref,
                  ya_ref, sgb_ref, uv_ref, *, tl2, ts):
    rows = RADIX * ts
    piece = min(CHUNK, ts)
    nchunk = rows // CHUNK
    assert nchunk % 2 == 0 and tl2 % ts == 0
    cs = cs_ref[...].astype(BF16)

    def project(o):
        x = x_ref[0, :, o:o + ts, :].reshape(rows, D_MODEL)
        h = _modulated_norm(x, mod_ref[0], ng_ref[...])
        return jnp.dot(h.astype(BF16), w_ref[...], preferred_element_type=F32)

    def mix(o, z):
        v = z[:, A_W:2 * A_W]
        mu = jnp.mean(v, axis=-1, keepdims=True)
        vc = v - mu
        var = jnp.mean(vc * vc, axis=-1, keepdims=True)
        vn = (vc * lax.rsqrt(var + NORM_EPS) * vg_ref[...]).astype(BF16)
        for hd in range(A_HEADS):
            c0 = hd * A_HEAD_DIM
            for cp in range(nchunk // 2):
                ra, rb = 2 * cp * CHUNK, (2 * cp + 1) * CHUNK
                vpair = jnp.concatenate([vn[ra:ra + CHUNK, c0:c0 + A_HEAD_DIM],
                                         vn[rb:rb + CHUNK, c0:c0 + A_HEAD_DIM]], axis=1)
                sv2 = jnp.dot(sw_ref[hd], vpair, preferred_element_type=F32)
                for half, r0 in enumerate((ra, rb)):
                    sv = sv2[:, half * A_HEAD_DIM:(half + 1) * A_HEAD_DIM] + sb_ref[hd]
                    u = z[r0:r0 + CHUNK, c0:c0 + A_HEAD_DIM]
                    ga = z[r0:r0 + CHUNK, 2 * A_W + c0:2 * A_W + c0 + A_HEAD_DIM]
                    ya = (u * sv * _silu(ga)).astype(BF16)
                    for p0 in range(0, CHUNK, piece):
                        l1, off = divmod(r0 + p0, ts)
                        ya_ref[0, l1, o + off:o + off + piece, c0:c0 + A_HEAD_DIM] = ya[p0:p0 + piece]

        gb = z[:, 3 * A_W + B_W:]
        sgb_ref[0, :, o:o + ts, :] = _silu(gb).astype(BF16).reshape(RADIX, ts, B_W)

        xb = z[:, 3 * A_W:3 * A_W + B_W].astype(BF16)
        for g in range(B_GROUPS):
            c0 = g * B_GROUP_DIM
            ps, qs = [], []
            for j in range(RADIX):
                pq = jnp.dot(xb[j * ts:(j + 1) * ts, c0:c0 + B_GROUP_DIM], cs,
                             preferred_element_type=F32)
                ps.append(pq[:, :B_GROUP_DIM])
                qs.append(pq[:, B_GROUP_DIM:])
            p02, p13 = ps[0] - ps[2], ps[1] - ps[3]
            q02, q13 = qs[0] - qs[2], qs[1] - qs[3]
            pe, po = ps[0] + ps[2], ps[1] + ps[3]
            qe, qo = qs[0] + qs[2], qs[1] + qs[3]
            us = [pe + po, p02 - q13, pe - po, p02 + q13]
            vs = [qe + qo, q02 + p13, qe - qo, q02 - p13]
            for k1 in range(RADIX):
                if k1 == 0:
                    ut, vt = us[0], vs[0]
                else:
                    tc, tsn = twc_ref[k1, o:o + ts, :], tws_ref[k1, o:o + ts, :]
                    ut = us[k1] * tc - vs[k1] * tsn
                    vt = us[k1] * tsn + vs[k1] * tc
                uv_ref[0, k1, 0, o:o + ts, c0:c0 + B_GROUP_DIM] = ut.astype(BF16)
                uv_ref[0, k1, 1, o:o + ts, c0:c0 + B_GROUP_DIM] = vt.astype(BF16)

    for o in range(0, tl2, ts):
        mix(o, project(o))


def _in_ab_call(xs, mod, mod_row, ng, w_in, vg, sw, sb, cs, twc, tws, tl2):
    bsz, seq_len, _ = xs.shape
    n2 = seq_len // RADIX
    x4 = xs.reshape(bsz, RADIX, n2, D_MODEL)
    row_of = (lambda b: b) if mod_row is None else (lambda b: mod_row)
    const2 = lambda b, i: (0, 0)
    const3 = lambda b, i: (0, 0, 0)
    return pl.pallas_call(
        functools.partial(_in_ab_kernel, tl2=tl2, ts=tl2),
        grid=(bsz, n2 // tl2),
        in_specs=[
            pl.BlockSpec((1, RADIX, tl2, D_MODEL), lambda b, i: (b, 0, i, 0)),
            pl.BlockSpec((1, 1, 3 * D_MODEL), lambda b, i: (row_of(b), 0, 0)),
            pl.BlockSpec((1, D_MODEL), const2),
            pl.BlockSpec((D_MODEL, AB_IN), const2),
            pl.BlockSpec((1, A_W), const2),
            pl.BlockSpec((A_HEADS, CHUNK, CHUNK), const3),
            pl.BlockSpec((A_HEADS, CHUNK, A_HEAD_DIM), const3),
            pl.BlockSpec((B_GROUP_DIM, 2 * B_GROUP_DIM), const2),
            pl.BlockSpec((RADIX, tl2, 128), lambda b, i: (0, i, 0)),
            pl.BlockSpec((RADIX, tl2, 128), lambda b, i: (0, i, 0)),
        ],
        out_specs=[
            pl.BlockSpec((1, RADIX, tl2, A_W), lambda b, i: (b, 0, i, 0)),
            pl.BlockSpec((1, RADIX, tl2, B_W), lambda b, i: (b, 0, i, 0)),
            pl.BlockSpec((1, RADIX, 2, tl2, B_W), lambda b, i: (b, 0, 0, i, 0)),
        ],
        out_shape=[
            jax.ShapeDtypeStruct((bsz, RADIX, n2, A_W), BF16),
            jax.ShapeDtypeStruct((bsz, RADIX, n2, B_W), BF16),
            jax.ShapeDtypeStruct((bsz, RADIX, 2, n2, B_W), BF16),
        ],
        compiler_params=_cparams(("parallel", "arbitrary")),
        name="in_ab",
    )(x4, mod, ng, w_in, vg, sw, sb, cs, twc, tws)


def _dft_out_kernel(g_ref, uv_ref, ya_ref, sgb_ref, x_ref, mod_ref, w_ref, o_ref, f_ref, *, tk):
    r0 = pl.multiple_of(pl.program_id(1) * tk, tk)
    gmat = g_ref[pl.ds(r0, tk), :].astype(BF16)
    for k1 in range(RADIX):
        f = jnp.dot(gmat, uv_ref[0, k1], preferred_element_type=F32)
        for g in range(B_GROUPS):
            f_ref[g, pl.ds(k1, tk, stride=RADIX), :] = f[:, g * B_GROUP_DIM:(g + 1) * B_GROUP_DIM]
    fnat = jnp.concatenate([f_ref[g] for g in range(B_GROUPS)], axis=1)
    yb = (fnat * sgb_ref[0].astype(F32)).astype(BF16)
    y = jnp.dot(ya_ref[0], w_ref[:A_W], preferred_element_type=F32)
    y = y + jnp.dot(yb, w_ref[A_W:], preferred_element_type=F32)
    gate = mod_ref[0][:, 2 * D_MODEL:]
    o_ref[0] = x_ref[0] + gate * y


def _dft_out_call(gmat, uv, ya, sgb, xs, mod, mod_row, w_out, tk):
    bsz, seq_len, _ = xs.shape
    n2 = seq_len // RADIX
    rows = RADIX * tk
    row_of = (lambda b: b) if mod_row is None else (lambda b: mod_row)
    nat = lambda w: pl.BlockSpec((1, rows, w), lambda b, i: (b, i, 0))
    return pl.pallas_call(
        functools.partial(_dft_out_kernel, tk=tk),
        grid=(bsz, n2 // tk),
        in_specs=[
            pl.BlockSpec((n2, 2 * n2), lambda b, i: (0, 0), pipeline_mode=pl.Buffered(1)),
            pl.BlockSpec((1, RADIX, 2 * n2, B_W), lambda b, i: (b, 0, 0, 0)),
            nat(A_W), nat(B_W), nat(D_MODEL),
            pl.BlockSpec((1, 1, 3 * D_MODEL), lambda b, i: (row_of(b), 0, 0)),
            pl.BlockSpec((A_W + B_W, D_MODEL), lambda b, i: (0, 0), pipeline_mode=pl.Buffered(1)),
        ],
        out_specs=nat(D_MODEL),
        out_shape=jax.ShapeDtypeStruct((bsz, seq_len, D_MODEL), F32),
        scratch_shapes=[pltpu.VMEM((B_GROUPS, rows, B_GROUP_DIM), F32)],
        compiler_params=_cparams(("parallel", "arbitrary")),
        name="dft_out",
    )(gmat, uv.reshape(bsz, RADIX, 2 * n2, B_W), ya.reshape(bsz, seq_len, A_W),
      sgb.reshape(bsz, seq_len, B_W), xs, mod, w_out)


def _mixer_ab_layer(xs, mod, mod_row, ng, w_in, vg, sw, sb, w_out, cs, tl2, tk):
    seq_len = xs.shape[1]
    twc, tws = _twiddle_tables(seq_len)
    gmat = jnp.asarray(_position_dft_matrix(seq_len))
    ya, sgb, uv = _in_ab_call(xs, mod, mod_row, ng, w_in, vg, sw, sb, cs,
                              jnp.asarray(twc), jnp.asarray(tws), tl2)
    return _dft_out_call(gmat, uv, ya, sgb, xs, mod, mod_row, w_out, tk=tk)


def _rope_block(t, cos, sin_signed, lane_lo):
    nf = C_HEAD_DIM // 4
    swapped = jnp.where(lane_lo, pltpu.roll(t, 128 - nf, axis=1), pltpu.roll(t, nf, axis=1))
    return t * cos + swapped * sin_signed


def _store_dup_heads(k_ref, c, t, lane):
    r = pltpu.roll(t, C_HEAD_DIM, axis=1)
    first = lane < C_HEAD_DIM
    k_ref[0, :, (2 * c) * 128:(2 * c + 1) * 128] = jnp.where(first, t, r).astype(BF16)
    k_ref[0, :, (2 * c + 1) * 128:(2 * c + 2) * 128] = jnp.where(first, r, t).astype(BF16)


def _in_c_kernel(x_ref, mod_ref, ng_ref, w_ref, cos_ref, sin_ref, q_ref, k_ref, v_ref, sg_ref):
    h = _modulated_norm(x_ref[0], mod_ref[0], ng_ref[...])
    z = jnp.dot(h.astype(BF16), w_ref[...], preferred_element_type=F32)
    cos = cos_ref[...]
    sin = sin_ref[...]
    lane = lax.broadcasted_iota(jnp.int32, cos.shape, 1)
    lane_lo = (lane % (C_HEAD_DIM // 2)) < (C_HEAD_DIM // 4)
    qscale = C_HEAD_DIM ** -0.5 * LOG2E
    for c in range(C_Q_W // 128):
        t = _rope_block(z[:, c * 128:(c + 1) * 128], cos, sin, lane_lo)
        q_ref[0, :, c * 128:(c + 1) * 128] = (t * qscale).astype(BF16)
    for c in range(C_KV_W // 128):
        t = _rope_block(z[:, C_Q_W + c * 128:C_Q_W + (c + 1) * 128], cos, sin, lane_lo)
        _store_dup_heads(k_ref, c, t, lane)
    v_ref[0] = z[:, C_Q_W + C_KV_W:C_Q_W + 2 * C_KV_W].T.astype(BF16)
    sg_ref[0] = _silu(z[:, C_Q_W + 2 * C_KV_W:]).astype(BF16)


def _in_c_call(xs, mod, ng, w_in, cos_t, sin_t, tm):
    bsz, seq_len, _ = xs.shape
    const2 = lambda b, i: (0, 0)
    row_blk = lambda w: pl.BlockSpec((1, tm, w), lambda b, i: (b, i, 0))
    return pl.pallas_call(
        _in_c_kernel,
        grid=(bsz, seq_len // tm),
        in_specs=[
            row_blk(D_MODEL),
            pl.BlockSpec((1, 1, 3 * D_MODEL), lambda b, i: (b, 0, 0)),
            pl.BlockSpec((1, D_MODEL), const2),
            pl.BlockSpec((D_MODEL, C_IN), const2),
            pl.BlockSpec((tm, 128), lambda b, i: (i, 0)),
            pl.BlockSpec((tm, 128), lambda b, i: (i, 0)),
        ],
        out_specs=[row_blk(C_Q_W), row_blk(2 * C_KV_W),
                   pl.BlockSpec((1, C_KV_W, tm), lambda b, i: (b, 0, i)), row_blk(C_Q_W)],
        out_shape=[
            jax.ShapeDtypeStruct((bsz, seq_len, C_Q_W), BF16),
            jax.ShapeDtypeStruct((bsz, seq_len, 2 * C_KV_W), BF16),
            jax.ShapeDtypeStruct((bsz, C_KV_W, seq_len), BF16),
            jax.ShapeDtypeStruct((bsz, seq_len, C_Q_W), BF16),
        ],
        compiler_params=_cparams(("parallel", "arbitrary")),
        name="in_c",
    )(xs, mod, ng, w_in, cos_t, sin_t)


def _ctx_kv_kernel(x_ref, mod_ref, ng_ref, w_ref, k_ref, v_ref):
    h = _modulated_norm(x_ref[0], mod_ref[0], ng_ref[...])
    z = jnp.dot(h.astype(BF16), w_ref[...], preferred_element_type=F32)
    lane = lax.broadcasted_iota(jnp.int32, (z.shape[0], 128), 1)
    for c in range(C_KV_W // 128):
        _store_dup_heads(k_ref, c, z[:, c * 128:(c + 1) * 128], lane)
    v_ref[0] = z[:, C_KV_W:].T.astype(BF16)


def _ctx_kv_call(ctx, mod, mod_row, ng, w_kv):
    bsz, lc, _ = ctx.shape
    return pl.pallas_call(
        _ctx_kv_kernel,
        grid=(bsz,),
        in_specs=[
            pl.BlockSpec((1, lc, D_MODEL), lambda b: (b, 0, 0)),
            pl.BlockSpec((1, 1, 3 * D_MODEL), lambda b: (mod_row, 0, 0)),
            pl.BlockSpec((1, D_MODEL), lambda b: (0, 0)),
            pl.BlockSpec((D_MODEL, 2 * C_KV_W), lambda b: (0, 0)),
        ],
        out_specs=[pl.BlockSpec((1, lc, 2 * C_KV_W), lambda b: (b, 0, 0)),
                   pl.BlockSpec((1, C_KV_W, lc), lambda b: (b, 0, 0))],
        out_shape=[jax.ShapeDtypeStruct((bsz, lc, 2 * C_KV_W), BF16),
                   jax.ShapeDtypeStruct((bsz, C_KV_W, lc), BF16)],
        compiler_params=_cparams(("parallel",)),
        name="ctx_kv",
    )(ctx, mod, ng, w_kv)


def _attn_out_kernel(sink_ref, q_ref, kp_ref, kc_ref, kn_ref, vp_ref, vc_ref, vn_ref, kx_ref, vx_ref,
                     sg_ref, x_ref, mod_ref, w_ref, fg_ref, o_ref):
    n = pl.program_id(1)
    last = pl.num_programs(1) - 1
    gq = C_GROUP * Q_BLOCK
    kj = lax.broadcasted_iota(jnp.int32, (Q_BLOCK, Q_BLOCK), 0)
    qi = lax.broadcasted_iota(jnp.int32, (Q_BLOCK, Q_BLOCK), 1)
    bias_before = jnp.where(kj >= qi, 0.0, NEG_INF).astype(F32)
    bias_after = jnp.where(kj <= qi, 0.0, NEG_INF).astype(F32)
    edge_first = jnp.where(n > 0, 0.0, NEG_INF).astype(F32)
    edge_last = jnp.where(n < last, 0.0, NEG_INF).astype(F32)
    lane = lax.broadcasted_iota(jnp.int32, (Q_BLOCK, 128), 1)
    first = lane < C_HEAD_DIM
    zero = jnp.zeros((Q_BLOCK, 128), BF16)
    row_k = lax.broadcasted_iota(jnp.int32, (128, kx_ref.shape[1] + 3 * Q_BLOCK), 0)
    nq = q_ref.shape[1] // Q_BLOCK

    def window(prev_ref, cur_ref, next_ref, j, lanes):
        if j < 0:
            return prev_ref[0, :, lanes]
        if j >= nq:
            return next_ref[0, :, lanes]
        return cur_ref[0, j * Q_BLOCK:(j + 1) * Q_BLOCK, lanes]

    def scores(qb, kh):
        kl = slice(kh * 128, (kh + 1) * 128)
        kblocks = [kx_ref[0, :, kl]] + [window(kp_ref, kc_ref, kn_ref, j, kl) for j in (qb - 1, qb, qb + 1)]
        biases = (None, (bias_before, edge_first if qb == 0 else None), None,
                  (bias_after, edge_last if qb == nq - 1 else None))
        rows_q = slice(qb * Q_BLOCK, (qb + 1) * Q_BLOCK)
        q4 = []
        for c in range(2):
            qv = q_ref[0, rows_q, kh * 256 + c * 128:kh * 256 + (c + 1) * 128]
            q4 += [jnp.where(first, qv, zero), jnp.where(first, zero, qv)]
        q4 = jnp.concatenate(q4, axis=0)
        k2 = jnp.concatenate(kblocks, axis=0)
        st = lax.dot_general(k2, q4, (((1,), (1,)), ((), ())), preferred_element_type=F32)
        blocks, m8, r = [], None, 0
        for kb, bias in zip(kblocks, biases):
            sb = st[r:r + kb.shape[0]]
            r += kb.shape[0]
            if bias is not None:
                mask_bias, edge = bias
                cols = [sb[:, j * Q_BLOCK:(j + 1) * Q_BLOCK] + mask_bias for j in range(C_GROUP)]
                sb = jnp.concatenate(cols, axis=1)
                if edge is not None:
                    sb = sb + edge
            blocks.append(sb)
            mb = jnp.max(sb.reshape(sb.shape[0] // 8, 8, gq), axis=0)
            m8 = mb if m8 is None else jnp.maximum(m8, mb)
        return blocks, m8

    def softmax(kh, blocks, m8):
        sink2 = sink_ref[kh] * LOG2E
        m = jnp.maximum(jnp.max(m8, axis=0, keepdims=True), sink2)
        pts = []
        for blk in blocks:
            for r in range(0, blk.shape[0], SOFTMAX_ROWS):
                pts.append(jnp.exp2(blk[r:r + SOFTMAX_ROWS] - m).astype(BF16))
        return jnp.concatenate(pts, axis=0), jnp.exp2(sink2 - m)

    def weighted_values(qb, kh, pt, sink_term):
        vrows = slice((kh // 2) * 128, (kh // 2 + 1) * 128)
        vblocks = [vx_ref[0, vrows, :]]
        for j in (qb - 1, qb, qb + 1):
            if j < 0:
                vblocks.append(vp_ref[0, vrows, :])
            elif j >= nq:
                vblocks.append(vn_ref[0, vrows, :])
            else:
                vblocks.append(vc_ref[0, vrows, j * Q_BLOCK:(j + 1) * Q_BLOCK])
        vpair = jnp.concatenate(vblocks, axis=1)
        own = (row_k < C_HEAD_DIM) if kh % 2 == 0 else (row_k >= C_HEAD_DIM)
        vsum = jnp.where(own, vpair, jnp.ones_like(vpair))
        ot = jnp.dot(vsum, pt, preferred_element_type=F32)
        r0 = (kh % 2) * C_HEAD_DIM
        r1 = C_HEAD_DIM - r0
        denom = ot[r1:r1 + C_HEAD_DIM] + sink_term
        ot = ot[r0:r0 + C_HEAD_DIM] / denom
        cols = []
        for c in range(2):
            pair = jnp.concatenate([ot[:, (2 * c) * Q_BLOCK:(2 * c + 1) * Q_BLOCK],
                                    ot[:, (2 * c + 1) * Q_BLOCK:(2 * c + 2) * Q_BLOCK]], axis=0)
            cols.append(pair.T)
        return cols

    chains = [(qb, kh) for qb in range(nq) for kh in range(C_KV_HEADS)]
    o_cols = {qb: [] for qb in range(nq)}
    gate = mod_ref[0][:, 2 * D_MODEL:]

    def finish_tasks(qb0, qb1):
        rows = slice(qb0 * Q_BLOCK, qb1 * Q_BLOCK)
        state = {}

        def gate_values():
            o_all = jnp.concatenate([jnp.concatenate(o_cols[qb], axis=1) for qb in range(qb0, qb1)], axis=0)
            state["o"] = (o_all * sg_ref[0, rows, :].astype(F32)).astype(BF16)
            state["y"] = []

        def project(c0):
            state["y"].append(jnp.dot(state["o"], w_ref[:, c0:c0 + OUT_PROJ_COLS],
                                      preferred_element_type=F32))

        def residual_norm():
            y = jnp.concatenate(state["y"], axis=1)
            x2 = x_ref[0, rows, :] + gate * y
            ms = jnp.mean(x2 * x2, axis=-1, keepdims=True)
            o_ref[0, rows, :] = x2 * lax.rsqrt(ms + NORM_EPS) * fg_ref[...]

        tasks = [gate_values]
        tasks += [functools.partial(project, c0) for c0 in range(0, D_MODEL, OUT_PROJ_COLS)]
        return tasks + [residual_norm]

    pending = [scores(*ch) for ch in chains[:SCORE_LOOKAHEAD]]
    probs = []
    deferred = []

    def values_for(idx):
        qb, kh = chains[idx]
        o_cols[qb] += weighted_values(qb, kh, *probs.pop(0))
        if kh == C_KV_HEADS - 1 and qb % OUT_PROJ_BLOCKS == OUT_PROJ_BLOCKS - 1:
            deferred.extend(finish_tasks(qb + 1 - OUT_PROJ_BLOCKS, qb + 1))

    for idx, (qb, kh) in enumerate(chains):
        if idx + SCORE_LOOKAHEAD < len(chains):
            pending.append(scores(*chains[idx + SCORE_LOOKAHEAD]))
        probs.append(softmax(kh, *pending.pop(0)))
        if idx >= PV_LAG:
            values_for(idx - PV_LAG)
        if deferred:
            deferred.pop(0)()
    for idx in range(len(chains) - PV_LAG, len(chains)):
        values_for(idx)
    for task in deferred:
        task()


def _attn_out_call(sink, q, k, v, kx, vx, sg, xs, mod, w_out, fg):
    bsz, seq_len, _ = xs.shape
    nblk = seq_len // Q_BLOCK
    nq = ATTN_Q_BLOCKS
    nstep = nblk // nq
    lc = kx.shape[1]
    assert lc % SOFTMAX_ROWS == 0 and Q_BLOCK % SOFTMAX_ROWS == 0 and nq % OUT_PROJ_BLOCKS == 0
    cur = lambda w: pl.BlockSpec((1, nq * Q_BLOCK, w), lambda b, n: (b, n, 0))
    prev = lambda w: pl.BlockSpec((1, Q_BLOCK, w), lambda b, n: (b, jnp.maximum(nq * n - 1, 0), 0))
    nxt = lambda w: pl.BlockSpec((1, Q_BLOCK, w), lambda b, n: (b, jnp.minimum(nq * n + nq, nblk - 1), 0))
    ctx_blk = lambda w: pl.BlockSpec((1, lc, w), lambda b, n: (b, 0, 0))
    kw = 2 * C_KV_W
    vcur = pl.BlockSpec((1, C_KV_W, nq * Q_BLOCK), lambda b, n: (b, 0, n))
    vprev = pl.BlockSpec((1, C_KV_W, Q_BLOCK), lambda b, n: (b, 0, jnp.maximum(nq * n - 1, 0)))
    vnxt = pl.BlockSpec((1, C_KV_W, Q_BLOCK), lambda b, n: (b, 0, jnp.minimum(nq * n + nq, nblk - 1)))
    vctx = pl.BlockSpec((1, C_KV_W, lc), lambda b, n: (b, 0, 0))
    sink_t = jnp.repeat(sink.reshape(C_KV_HEADS, 1, C_GROUP), Q_BLOCK, axis=2)
    return pl.pallas_call(
        _attn_out_kernel,
        grid=(bsz, nstep),
        in_specs=[
            pl.BlockSpec((C_KV_HEADS, 1, C_GROUP * Q_BLOCK), lambda b, n: (0, 0, 0)),
            cur(C_Q_W), prev(kw), cur(kw), nxt(kw), vprev, vcur, vnxt, ctx_blk(kw), vctx,
            cur(C_Q_W), cur(D_MODEL),
            pl.BlockSpec((1, 1, 3 * D_MODEL), lambda b, n: (b, 0, 0)),
            pl.BlockSpec((C_Q_W, D_MODEL), lambda b, n: (0, 0)),
            pl.BlockSpec((1, D_MODEL), lambda b, n: (0, 0)),
        ],
        out_specs=cur(D_MODEL),
        out_shape=jax.ShapeDtypeStruct((bsz, seq_len, D_MODEL), F32),
        compiler_params=_cparams(("parallel", "arbitrary")),
        name="attn_out",
    )(sink_t, q, k, k, k, v, v, v, kx, vx, sg, xs, mod, w_out, fg)


def kernel(x, c, ctx, c_ctx, norm_g, ada_w, ada_b, w_in_ab, v_norm_g, spatial_w, spatial_b, w_out_ab,
           w_in_c, sink_logit, w_out_c, final_g):
    bsz, seq_len, _ = x.shape
    depth = ada_w.shape[0]
    assert depth == 2 and bsz + 1 <= MOD_ROWS
    ctx_row = bsz

    cc = jnp.concatenate([c, c_ctx[None, :], jnp.zeros((MOD_ROWS - bsz - 1, D_MODEL), F32)], axis=0)
    mod = _mod_call(cc, ada_w, ada_b)
    mod0 = mod[0].reshape(MOD_ROWS, 1, 3 * D_MODEL)
    mod1 = mod[1].reshape(MOD_ROWS, 1, 3 * D_MODEL)

    cs = jnp.asarray(_channel_dft_matrix())
    ng0 = norm_g[0].reshape(1, D_MODEL)
    ng1 = norm_g[1].reshape(1, D_MODEL)
    w_in0 = w_in_ab[0].astype(BF16)
    w_out0 = w_out_ab[0].astype(BF16)
    vg = v_norm_g[0].reshape(1, A_W)
    sw = spatial_w[0].astype(BF16)
    sb = jnp.broadcast_to(spatial_b[0][:, :, None], (A_HEADS, CHUNK, A_HEAD_DIM))

    x1 = _mixer_ab_layer(x, mod0, None, ng0, w_in0, vg, sw, sb, w_out0, cs, tl2=256, tk=256)
    ctx1 = _mixer_ab_layer(ctx, mod0, ctx_row, ng0, w_in0, vg, sw, sb, w_out0, cs,
                           tl2=ctx.shape[1] // RADIX, tk=ctx.shape[1] // RADIX)

    w_in1 = w_in_c[0].astype(BF16)
    w_out1 = w_out_c[0].astype(BF16)
    cos_t, sin_t = _rope_tables(seq_len)
    q, k, v, sg = _in_c_call(x1, mod1, ng1, w_in1, jnp.asarray(cos_t), jnp.asarray(sin_t), tm=1024)
    kx, vx = _ctx_kv_call(ctx1, mod1, ctx_row, ng1, w_in1[:, C_Q_W:C_Q_W + 2 * C_KV_W])
    return _attn_out_call(sink_logit[0], q, k, v, kx, vx, sg, x1, mod1, w_out1,
                          final_g.reshape(1, D_MODEL))
```

```python
import functools
import math

import numpy as np
import jax
import jax.numpy as jnp
from jax import lax
from jax.experimental import pallas as pl
from jax.experimental.pallas import tpu as pltpu

F32 = jnp.float32
BF16 = jnp.bfloat16

D_MODEL = 1024
GRID_W = 64
CHUNK = 128
A_HEADS = 4
A_HEAD_DIM = 128
A_W = A_HEADS * A_HEAD_DIM
B_GROUPS = 4
B_GROUP_DIM = 128
B_W = B_GROUPS * B_GROUP_DIM
AB_IN = 3 * A_W + 2 * B_W
C_HEADS = 16
C_KV_HEADS = 4
C_GROUP = C_HEADS // C_KV_HEADS
C_HEAD_DIM = 64
C_Q_W = C_HEADS * C_HEAD_DIM
C_KV_W = C_KV_HEADS * C_HEAD_DIM
C_IN = 2 * C_Q_W + 2 * C_KV_W
WINDOW = 128
Q_BLOCK = 128
ROPE_BASE = 10000.0
NORM_EPS = 1e-6
NEG_INF = -1e30
LOG2E = math.log2(math.e)
RADIX = 4
MOD_ROWS = 16
SOFTMAX_ROWS = 64
SCORE_LOOKAHEAD = 2
OUT_PROJ_COLS = 256
PV_LAG = 1
ATTN_Q_BLOCKS = 4
OUT_PROJ_BLOCKS = 2
V7X_VMEM_LIMIT = 56 * 1024 * 1024


def _silu(x):
    return x * (1.0 / (1.0 + jnp.exp(-x)))


def _cparams(sem):
    return pltpu.CompilerParams(dimension_semantics=sem, vmem_limit_bytes=V7X_VMEM_LIMIT)


def _channel_dft_matrix():
    n = np.arange(B_GROUP_DIM)
    ang = 2.0 * np.pi * np.outer(n, n) / B_GROUP_DIM
    return np.concatenate([np.cos(ang), np.sin(ang)], axis=1).astype(np.float32)


def _position_dft_matrix(seq_len):
    n2 = seq_len // RADIX
    idx = np.arange(n2)
    ang = 2.0 * np.pi * (np.outer(idx, idx) % n2) / n2
    norm = 1.0 / math.sqrt(seq_len * B_GROUP_DIM)
    return np.concatenate([np.cos(ang) * norm, -np.sin(ang) * norm], axis=1).astype(np.float32)


def _twiddle_tables(seq_len):
    n2 = seq_len // RADIX
    l2 = np.arange(n2)[None, :, None]
    k1 = np.arange(RADIX)[:, None, None]
    ang = 2.0 * np.pi * ((l2 * k1) % seq_len) / seq_len
    ang = np.broadcast_to(ang, (RADIX, n2, 128))
    return np.cos(ang).astype(np.float32), np.sin(ang).astype(np.float32)


def _rope_tables(seq_len):
    t = np.arange(seq_len)
    row = (t // GRID_W).astype(np.float64)
    col = (t % GRID_W).astype(np.float64)
    lane = np.arange(128)
    dd = lane % C_HEAD_DIM
    nf = C_HEAD_DIM // 4
    inv = ROPE_BASE ** (-(dd % nf).astype(np.float64) / nf)
    pos = np.where((dd < C_HEAD_DIM // 2)[None, :], row[:, None], col[:, None])
    ang = pos * inv[None, :]
    sign = np.where((dd % (2 * nf)) < nf, -1.0, 1.0)[None, :]
    return np.cos(ang).astype(np.float32), (np.sin(ang) * sign).astype(np.float32)


def _mod_kernel(c_ref, w_ref, b_ref, o_ref):
    s = _silu(c_ref[...]).astype(BF16)
    o_ref[0] = jnp.dot(s, w_ref[0].astype(BF16), preferred_element_type=F32) + b_ref[0]


def _mod_call(cc, ada_w, ada_b):
    depth = ada_w.shape[0]
    tn = 1024
    return pl.pallas_call(
        _mod_kernel,
        grid=(depth, 3 * D_MODEL // tn),
        in_specs=[
            pl.BlockSpec((MOD_ROWS, D_MODEL), lambda l, j: (0, 0)),
            pl.BlockSpec((1, D_MODEL, tn), lambda l, j: (l, 0, j)),
            pl.BlockSpec((1, 1, tn), lambda l, j: (l, 0, j)),
        ],
        out_specs=pl.BlockSpec((1, MOD_ROWS, tn), lambda l, j: (l, 0, j)),
        out_shape=jax.ShapeDtypeStruct((depth, MOD_ROWS, 3 * D_MODEL), F32),
        compiler_params=_cparams(("arbitrary", "arbitrary")),
        name="adaln_mod",
    )(cc, ada_w, ada_b.reshape(depth, 1, 3 * D_MODEL))


def _modulated_norm(x, mod_row, g):
    shift = mod_row[:, :D_MODEL]
    scale = mod_row[:, D_MODEL:2 * D_MODEL]
    ms = jnp.mean(x * x, axis=-1, keepdims=True)
    h = x * lax.rsqrt(ms + NORM_EPS) * g
    return h * (1.0 + scale) + shift


def _in_ab_kernel(x_ref, mod_ref, ng_ref, w_ref, vg_ref, sw_ref, sb_ref, cs_ref, twc_ref, tws_ref,
                  ya_ref, sgb_ref, uv_ref, *, tl2, ts):
    rows = RADIX * ts
    piece = min(CHUNK, ts)
    nchunk = rows // CHUNK
    assert nchunk % 2 == 0 and tl2 % ts == 0
    cs = cs_ref[...].astype(BF16)

    nb = x_ref.shape[0]

    def project(o):
        x = x_ref[:, :, o:o + ts, :].reshape(nb * rows, D_MODEL)
        h = _modulated_norm(x, mod_ref[0], ng_ref[...])
        return jnp.dot(h.astype(BF16), w_ref[...], preferred_element_type=F32)

    def mix(bb, o, z):
        v = z[:, A_W:2 * A_W]
        mu = jnp.mean(v, axis=-1, keepdims=True)
        vc = v - mu
        var = jnp.mean(vc * vc, axis=-1, keepdims=True)
        vn = (vc * lax.rsqrt(var + NORM_EPS) * vg_ref[...]).astype(BF16)
        for hd in range(A_HEADS):
            c0 = hd * A_HEAD_DIM
            for cp in range(nchunk // 2):
                ra, rb = 2 * cp * CHUNK, (2 * cp + 1) * CHUNK
                vpair = jnp.concatenate([vn[ra:ra + CHUNK, c0:c0 + A_HEAD_DIM],
                                         vn[rb:rb + CHUNK, c0:c0 + A_HEAD_DIM]], axis=1)
                sv2 = jnp.dot(sw_ref[hd], vpair, preferred_element_type=F32)
                for half, r0 in enumerate((ra, rb)):
                    sv = sv2[:, half * A_HEAD_DIM:(half + 1) * A_HEAD_DIM] + sb_ref[hd]
                    u = z[r0:r0 + CHUNK, c0:c0 + A_HEAD_DIM]
                    ga = z[r0:r0 + CHUNK, 2 * A_W + c0:2 * A_W + c0 + A_HEAD_DIM]
                    ya = (u * sv * _silu(ga)).astype(BF16)
                    for p0 in range(0, CHUNK, piece):
                        l1, off = divmod(r0 + p0, ts)
                        ya_ref[bb, l1, o + off:o + off + piece, c0:c0 + A_HEAD_DIM] = ya[p0:p0 + piece]

        gb = z[:, 3 * A_W + B_W:]
        sgb_ref[bb, :, o:o + ts, :] = _silu(gb).astype(BF16).reshape(RADIX, ts, B_W)

        xb = z[:, 3 * A_W:3 * A_W + B_W].astype(BF16)
        for g in range(B_GROUPS):
            c0 = g * B_GROUP_DIM
            ps, qs = [], []
            for j in range(RADIX):
                pq = jnp.dot(xb[j * ts:(j + 1) * ts, c0:c0 + B_GROUP_DIM], cs,
                             preferred_element_type=F32)
                ps.append(pq[:, :B_GROUP_DIM])
                qs.append(pq[:, B_GROUP_DIM:])
            p02, p13 = ps[0] - ps[2], ps[1] - ps[3]
            q02, q13 = qs[0] - qs[2], qs[1] - qs[3]
            pe, po = ps[0] + ps[2], ps[1] + ps[3]
            qe, qo = qs[0] + qs[2], qs[1] + qs[3]
            us = [pe + po, p02 - q13, pe - po, p02 + q13]
            vs = [qe + qo, q02 + p13, qe - qo, q02 - p13]
            for k1 in range(RADIX):
                if k1 == 0:
                    ut, vt = us[0], vs[0]
                else:
                    tc, tsn = twc_ref[k1, o:o + ts, :], tws_ref[k1, o:o + ts, :]
                    ut = us[k1] * tc - vs[k1] * tsn
                    vt = us[k1] * tsn + vs[k1] * tc
                uv_ref[bb, k1, 0, o:o + ts, c0:c0 + B_GROUP_DIM] = ut.astype(BF16)
                uv_ref[bb, k1, 1, o:o + ts, c0:c0 + B_GROUP_DIM] = vt.astype(BF16)

    for o in range(0, tl2, ts):
        z = project(o)
        for bb in range(nb):
            mix(bb, o, z[bb * rows:(bb + 1) * rows])


def _streams_per_step(bsz, mod_row):
    return 4 if (mod_row is not None and bsz % 4 == 0) else 1


def _in_ab_call(xs, mod, mod_row, ng, w_in, vg, sw, sb, cs, twc, tws, tl2):
    bsz, seq_len, _ = xs.shape
    n2 = seq_len // RADIX
    x4 = xs.reshape(bsz, RADIX, n2, D_MODEL)
    row_of = (lambda b: b) if mod_row is None else (lambda b: mod_row)
    const2 = lambda b, i: (0, 0)
    const3 = lambda b, i: (0, 0, 0)
    nb = _streams_per_step(bsz, mod_row)
    return pl.pallas_call(
        functools.partial(_in_ab_kernel, tl2=tl2, ts=tl2),
        grid=(bsz // nb, n2 // tl2),
        in_specs=[
            pl.BlockSpec((nb, RADIX, tl2, D_MODEL), lambda b, i: (b, 0, i, 0)),
            pl.BlockSpec((1, 1, 3 * D_MODEL), lambda b, i: (row_of(b), 0, 0)),
            pl.BlockSpec((1, D_MODEL), const2),
            pl.BlockSpec((D_MODEL, AB_IN), const2),
            pl.BlockSpec((1, A_W), const2),
            pl.BlockSpec((A_HEADS, CHUNK, CHUNK), const3),
            pl.BlockSpec((A_HEADS, CHUNK, A_HEAD_DIM), const3),
            pl.BlockSpec((B_GROUP_DIM, 2 * B_GROUP_DIM), const2),
            pl.BlockSpec((RADIX, tl2, 128), lambda b, i: (0, i, 0)),
            pl.BlockSpec((RADIX, tl2, 128), lambda b, i: (0, i, 0)),
        ],
        out_specs=[
            pl.BlockSpec((nb, RADIX, tl2, A_W), lambda b, i: (b, 0, i, 0)),
            pl.BlockSpec((nb, RADIX, tl2, B_W), lambda b, i: (b, 0, i, 0)),
            pl.BlockSpec((nb, RADIX, 2, tl2, B_W), lambda b, i: (b, 0, 0, i, 0)),
        ],
        out_shape=[
            jax.ShapeDtypeStruct((bsz, RADIX, n2, A_W), BF16),
            jax.ShapeDtypeStruct((bsz, RADIX, n2, B_W), BF16),
            jax.ShapeDtypeStruct((bsz, RADIX, 2, n2, B_W), BF16),
        ],
        compiler_params=_cparams(("parallel", "arbitrary")),
        name="in_ab",
    )(x4, mod, ng, w_in, vg, sw, sb, cs, twc, tws)


def _dft_out_kernel(g_ref, uv_ref, ya_ref, sgb_ref, mod_ref, w_ref, o_ref, f_ref, *, tk):
    nb, rows, _ = o_ref.shape
    r0 = pl.multiple_of(pl.program_id(1) * tk, tk)
    gmat = g_ref[pl.ds(r0, tk), :].astype(BF16)
    for bb in range(nb):
        for k1 in range(RADIX):
            f = jnp.dot(gmat, uv_ref[bb, k1], preferred_element_type=F32)
            for g in range(B_GROUPS):
                f_ref[g, pl.ds(bb * rows + k1, tk, stride=RADIX), :] = f[:, g * B_GROUP_DIM:(g + 1) * B_GROUP_DIM]
    fnat = jnp.concatenate([f_ref[g] for g in range(B_GROUPS)], axis=1)
    yb = (fnat * sgb_ref[...].reshape(nb * rows, B_W).astype(F32)).astype(BF16)
    y = jnp.dot(ya_ref[...].reshape(nb * rows, A_W), w_ref[:A_W], preferred_element_type=F32)
    y = y + jnp.dot(yb, w_ref[A_W:], preferred_element_type=F32)
    gate = mod_ref[0][:, 2 * D_MODEL:]
    o_ref[...] = (gate * y).reshape(nb, rows, D_MODEL)


def _dft_out_call(gmat, uv, ya, sgb, mod, mod_row, w_out, tk):
    bsz, seq_len = ya.shape[0], ya.shape[1] * ya.shape[2]
    n2 = seq_len // RADIX
    rows = RADIX * tk
    row_of = (lambda b: b) if mod_row is None else (lambda b: mod_row)
    nb = _streams_per_step(bsz, mod_row)
    nat = lambda w: pl.BlockSpec((nb, rows, w), lambda b, i: (b, i, 0))
    return pl.pallas_call(
        functools.partial(_dft_out_kernel, tk=tk),
        grid=(bsz // nb, n2 // tk),
        in_specs=[
            pl.BlockSpec((n2, 2 * n2), lambda b, i: (0, 0), pipeline_mode=pl.Buffered(1)),
            pl.BlockSpec((nb, RADIX, 2 * n2, B_W), lambda b, i: (b, 0, 0, 0)),
            nat(A_W), nat(B_W),
            pl.BlockSpec((1, 1, 3 * D_MODEL), lambda b, i: (row_of(b), 0, 0)),
            pl.BlockSpec((A_W + B_W, D_MODEL), lambda b, i: (0, 0), pipeline_mode=pl.Buffered(1)),
        ],
        out_specs=nat(D_MODEL),
        out_shape=jax.ShapeDtypeStruct((bsz, seq_len, D_MODEL), F32),
        scratch_shapes=[pltpu.VMEM((B_GROUPS, nb * rows, B_GROUP_DIM), F32)],
        compiler_params=_cparams(("parallel", "arbitrary")),
        name="dft_out",
    )(gmat, uv.reshape(bsz, RADIX, 2 * n2, B_W), ya.reshape(bsz, seq_len, A_W),
      sgb.reshape(bsz, seq_len, B_W), mod, w_out)


def _mixer_ab_layer(xs, mod, mod_row, ng, w_in, vg, sw, sb, w_out, cs, tl2, tk):
    seq_len = xs.shape[1]
    twc, tws = _twiddle_tables(seq_len)
    gmat = jnp.asarray(_position_dft_matrix(seq_len))
    ya, sgb, uv = _in_ab_call(xs, mod, mod_row, ng, w_in, vg, sw, sb, cs,
                              jnp.asarray(twc), jnp.asarray(tws), tl2)
    return _dft_out_call(gmat, uv, ya, sgb, mod, mod_row, w_out, tk=tk)


def _rope_block(t, cos, sin_signed, lane_lo):
    nf = C_HEAD_DIM // 4
    swapped = jnp.where(lane_lo, pltpu.roll(t, 128 - nf, axis=1), pltpu.roll(t, nf, axis=1))
    return t * cos + swapped * sin_signed


def _store_dup_heads(k_ref, c, t, lane):
    r = pltpu.roll(t, C_HEAD_DIM, axis=1)
    first = lane < C_HEAD_DIM
    k_ref[0, :, (2 * c) * 128:(2 * c + 1) * 128] = jnp.where(first, t, r).astype(BF16)
    k_ref[0, :, (2 * c + 1) * 128:(2 * c + 2) * 128] = jnp.where(first, r, t).astype(BF16)


def _in_c_kernel(x_ref, u_ref, mod_ref, ng_ref, w_ref, cos_ref, sin_ref, q_ref, k_ref, v_ref, sg_ref):
    h = _modulated_norm(x_ref[0] + u_ref[0], mod_ref[0], ng_ref[...])
    z = jnp.dot(h.astype(BF16), w_ref[...], preferred_element_type=F32)
    cos = cos_ref[...]
    sin = sin_ref[...]
    lane = lax.broadcasted_iota(jnp.int32, cos.shape, 1)
    lane_lo = (lane % (C_HEAD_DIM // 2)) < (C_HEAD_DIM // 4)
    qscale = C_HEAD_DIM ** -0.5 * LOG2E
    for c in range(C_Q_W // 128):
        t = _rope_block(z[:, c * 128:(c + 1) * 128], cos, sin, lane_lo)
        q_ref[0, :, c * 128:(c + 1) * 128] = (t * qscale).astype(BF16)
    for c in range(C_KV_W // 128):
        t = _rope_block(z[:, C_Q_W + c * 128:C_Q_W + (c + 1) * 128], cos, sin, lane_lo)
        _store_dup_heads(k_ref, c, t, lane)
    v_ref[0] = z[:, C_Q_W + C_KV_W:C_Q_W + 2 * C_KV_W].T.astype(BF16)
    sg_ref[0] = _silu(z[:, C_Q_W + 2 * C_KV_W:]).astype(BF16)


def _in_c_call(xs, upd, mod, ng, w_in, cos_t, sin_t, tm):
    bsz, seq_len, _ = xs.shape
    const2 = lambda b, i: (0, 0)
    row_blk = lambda w: pl.BlockSpec((1, tm, w), lambda b, i: (b, i, 0))
    return pl.pallas_call(
        _in_c_kernel,
        grid=(bsz, seq_len // tm),
        in_specs=[
            row_blk(D_MODEL), row_blk(D_MODEL),
            pl.BlockSpec((1, 1, 3 * D_MODEL), lambda b, i: (b, 0, 0)),
            pl.BlockSpec((1, D_MODEL), const2),
            pl.BlockSpec((D_MODEL, C_IN), const2),
            pl.BlockSpec((tm, 128), lambda b, i: (i, 0)),
            pl.BlockSpec((tm, 128), lambda b, i: (i, 0)),
        ],
        out_specs=[row_blk(C_Q_W), row_blk(2 * C_KV_W),
                   pl.BlockSpec((1, C_KV_W, tm), lambda b, i: (b, 0, i)), row_blk(C_Q_W)],
        out_shape=[
            jax.ShapeDtypeStruct((bsz, seq_len, C_Q_W), BF16),
            jax.ShapeDtypeStruct((bsz, seq_len, 2 * C_KV_W), BF16),
            jax.ShapeDtypeStruct((bsz, C_KV_W, seq_len), BF16),
            jax.ShapeDtypeStruct((bsz, seq_len, C_Q_W), BF16),
        ],
        compiler_params=_cparams(("parallel", "arbitrary")),
        name="in_c",
    )(xs, upd, mod, ng, w_in, cos_t, sin_t)


def _ctx_kv_kernel(x_ref, u_ref, mod_ref, ng_ref, w_ref, k_ref, v_ref):
    nb, lc, _ = x_ref.shape
    ctx1 = (x_ref[...] + u_ref[...]).reshape(nb * lc, D_MODEL)
    h = _modulated_norm(ctx1, mod_ref[0], ng_ref[...])
    z = jnp.dot(h.astype(BF16), w_ref[...], preferred_element_type=F32)
    lane = lax.broadcasted_iota(jnp.int32, (lc, 128), 1)
    for bb in range(nb):
        zb = z[bb * lc:(bb + 1) * lc]
        for c in range(C_KV_W // 128):
            _store_dup_heads(k_ref.at[pl.ds(bb, 1)], c, zb[:, c * 128:(c + 1) * 128], lane)
        v_ref[bb] = zb[:, C_KV_W:].T.astype(BF16)


def _ctx_kv_call(ctx, upd, mod, mod_row, ng, w_kv):
    bsz, lc, _ = ctx.shape
    nb = 4 if bsz % 4 == 0 else 1
    return pl.pallas_call(
        _ctx_kv_kernel,
        grid=(bsz // nb,),
        in_specs=[
            pl.BlockSpec((nb, lc, D_MODEL), lambda b: (b, 0, 0)),
            pl.BlockSpec((nb, lc, D_MODEL), lambda b: (b, 0, 0)),
            pl.BlockSpec((1, 1, 3 * D_MODEL), lambda b: (mod_row, 0, 0)),
            pl.BlockSpec((1, D_MODEL), lambda b: (0, 0)),
            pl.BlockSpec((D_MODEL, 2 * C_KV_W), lambda b: (0, C_Q_W // (2 * C_KV_W))),
        ],
        out_specs=[pl.BlockSpec((nb, lc, 2 * C_KV_W), lambda b: (b, 0, 0)),
                   pl.BlockSpec((nb, C_KV_W, lc), lambda b: (b, 0, 0))],
        out_shape=[jax.ShapeDtypeStruct((bsz, lc, 2 * C_KV_W), BF16),
                   jax.ShapeDtypeStruct((bsz, C_KV_W, lc), BF16)],
        compiler_params=_cparams(("parallel",)),
        name="ctx_kv",
    )(ctx, upd, mod, ng, w_kv)


def _attn_out_kernel(sink_ref, q_ref, kp_ref, kc_ref, kn_ref, vp_ref, vc_ref, vn_ref, kx_ref, vx_ref,
                     sg_ref, x_ref, u_ref, mod_ref, w_ref, fg_ref, o_ref):
    n = pl.program_id(1)
    last = pl.num_programs(1) - 1
    gq = C_GROUP * Q_BLOCK
    kj = lax.broadcasted_iota(jnp.int32, (Q_BLOCK, Q_BLOCK), 0)
    qi = lax.broadcasted_iota(jnp.int32, (Q_BLOCK, Q_BLOCK), 1)
    bias_before = jnp.where(kj >= qi, 0.0, NEG_INF).astype(F32)
    bias_after = jnp.where(kj <= qi, 0.0, NEG_INF).astype(F32)
    edge_first = jnp.where(n > 0, 0.0, NEG_INF).astype(F32)
    edge_last = jnp.where(n < last, 0.0, NEG_INF).astype(F32)
    lane = lax.broadcasted_iota(jnp.int32, (Q_BLOCK, 128), 1)
    first = lane < C_HEAD_DIM
    zero = jnp.zeros((Q_BLOCK, 128), BF16)
    row_k = lax.broadcasted_iota(jnp.int32, (128, kx_ref.shape[1] + 3 * Q_BLOCK), 0)
    nq = q_ref.shape[1] // Q_BLOCK

    def window(prev_ref, cur_ref, next_ref, j, lanes):
        if j < 0:
            return prev_ref[0, :, lanes]
        if j >= nq:
            return next_ref[0, :, lanes]
        return cur_ref[0, j * Q_BLOCK:(j + 1) * Q_BLOCK, lanes]

    def scores(qb, kh):
        kl = slice(kh * 128, (kh + 1) * 128)
        kblocks = [kx_ref[0, :, kl]] + [window(kp_ref, kc_ref, kn_ref, j, kl) for j in (qb - 1, qb, qb + 1)]
        biases = (None, (bias_before, edge_first if qb == 0 else None), None,
                  (bias_after, edge_last if qb == nq - 1 else None))
        rows_q = slice(qb * Q_BLOCK, (qb + 1) * Q_BLOCK)
        q4 = []
        for c in range(2):
            qv = q_ref[0, rows_q, kh * 256 + c * 128:kh * 256 + (c + 1) * 128]
            q4 += [jnp.where(first, qv, zero), jnp.where(first, zero, qv)]
        q4 = jnp.concatenate(q4, axis=0)
        k2 = jnp.concatenate(kblocks, axis=0)
        st = lax.dot_general(k2, q4, (((1,), (1,)), ((), ())), preferred_element_type=F32)
        blocks, m8, r = [], None, 0
        for kb, bias in zip(kblocks, biases):
            sb = st[r:r + kb.shape[0]]
            r += kb.shape[0]
            if bias is not None:
                mask_bias, edge = bias
                cols = [sb[:, j * Q_BLOCK:(j + 1) * Q_BLOCK] + mask_bias for j in range(C_GROUP)]
                sb = jnp.concatenate(cols, axis=1)
                if edge is not None:
                    sb = sb + edge
            blocks.append(sb)
            mb = jnp.max(sb.reshape(sb.shape[0] // 8, 8, gq), axis=0)
            m8 = mb if m8 is None else jnp.maximum(m8, mb)
        return blocks, m8

    def softmax(kh, blocks, m8):
        sink2 = sink_ref[kh] * LOG2E
        m = jnp.maximum(jnp.max(m8, axis=0, keepdims=True), sink2)
        pts = []
        for blk in blocks:
            for r in range(0, blk.shape[0], SOFTMAX_ROWS):
                pts.append(jnp.exp2(blk[r:r + SOFTMAX_ROWS] - m).astype(BF16))
        return jnp.concatenate(pts, axis=0), jnp.exp2(sink2 - m)

    def weighted_values(qb, kh, pt, sink_term):
        vrows = slice((kh // 2) * 128, (kh // 2 + 1) * 128)
        vblocks = [vx_ref[0, vrows, :]]
        for j in (qb - 1, qb, qb + 1):
            if j < 0:
                vblocks.append(vp_ref[0, vrows, :])
            elif j >= nq:
                vblocks.append(vn_ref[0, vrows, :])
            else:
                vblocks.append(vc_ref[0, vrows, j * Q_BLOCK:(j + 1) * Q_BLOCK])
        vpair = jnp.concatenate(vblocks, axis=1)
        own = (row_k < C_HEAD_DIM) if kh % 2 == 0 else (row_k >= C_HEAD_DIM)
        vsum = jnp.where(own, vpair, jnp.ones_like(vpair))
        ot = jnp.dot(vsum, pt, preferred_element_type=F32)
        r0 = (kh % 2) * C_HEAD_DIM
        r1 = C_HEAD_DIM - r0
        denom = ot[r1:r1 + C_HEAD_DIM] + sink_term
        ot = ot[r0:r0 + C_HEAD_DIM] / denom
        cols = []
        for c in range(2):
            pair = jnp.concatenate([ot[:, (2 * c) * Q_BLOCK:(2 * c + 1) * Q_BLOCK],
                                    ot[:, (2 * c + 1) * Q_BLOCK:(2 * c + 2) * Q_BLOCK]], axis=0)
            cols.append(pair.T)
        return cols

    chains = [(qb, kh) for qb in range(nq) for kh in range(C_KV_HEADS)]
    o_cols = {qb: [] for qb in range(nq)}
    gate = mod_ref[0][:, 2 * D_MODEL:]

    def finish_tasks(qb0, qb1):
        rows = slice(qb0 * Q_BLOCK, qb1 * Q_BLOCK)
        state = {}

        def gate_values():
            o_all = jnp.concatenate([jnp.concatenate(o_cols[qb], axis=1) for qb in range(qb0, qb1)], axis=0)
            state["o"] = (o_all * sg_ref[0, rows, :].astype(F32)).astype(BF16)
            state["y"] = []

        def project(c0):
            state["y"].append(jnp.dot(state["o"], w_ref[:, c0:c0 + OUT_PROJ_COLS],
                                      preferred_element_type=F32))

        def residual_norm():
            y = jnp.concatenate(state["y"], axis=1)
            x2 = (x_ref[0, rows, :] + u_ref[0, rows, :]) + gate * y
            ms = jnp.mean(x2 * x2, axis=-1, keepdims=True)
            o_ref[0, rows, :] = x2 * lax.rsqrt(ms + NORM_EPS) * fg_ref[...]

        tasks = [gate_values]
        tasks += [functools.partial(project, c0) for c0 in range(0, D_MODEL, OUT_PROJ_COLS)]
        return tasks + [residual_norm]

    pending = [scores(*ch) for ch in chains[:SCORE_LOOKAHEAD]]
    probs = []
    deferred = []

    def values_for(idx):
        qb, kh = chains[idx]
        o_cols[qb] += weighted_values(qb, kh, *probs.pop(0))
        if kh == C_KV_HEADS - 1 and qb % OUT_PROJ_BLOCKS == OUT_PROJ_BLOCKS - 1:
            deferred.extend(finish_tasks(qb + 1 - OUT_PROJ_BLOCKS, qb + 1))

    for idx, (qb, kh) in enumerate(chains):
        if idx + SCORE_LOOKAHEAD < len(chains):
            pending.append(scores(*chains[idx + SCORE_LOOKAHEAD]))
        probs.append(softmax(kh, *pending.pop(0)))
        if idx >= PV_LAG:
            values_for(idx - PV_LAG)
        if deferred:
            deferred.pop(0)()
    for idx in range(len(chains) - PV_LAG, len(chains)):
        values_for(idx)
    for task in deferred:
        task()


def _attn_out_call(sink, q, k, v, kx, vx, sg, xs, upd, mod, w_out, fg):
    bsz, seq_len, _ = xs.shape
    nblk = seq_len // Q_BLOCK
    nq = ATTN_Q_BLOCKS
    nstep = nblk // nq
    lc = kx.shape[1]
    assert lc % SOFTMAX_ROWS == 0 and Q_BLOCK % SOFTMAX_ROWS == 0 and nq % OUT_PROJ_BLOCKS == 0
    cur = lambda w: pl.BlockSpec((1, nq * Q_BLOCK, w), lambda b, n: (b, n, 0))
    prev = lambda w: pl.BlockSpec((1, Q_BLOCK, w), lambda b, n: (b, jnp.maximum(nq * n - 1, 0), 0))
    nxt = lambda w: pl.BlockSpec((1, Q_BLOCK, w), lambda b, n: (b, jnp.minimum(nq * n + nq, nblk - 1), 0))
    ctx_blk = lambda w: pl.BlockSpec((1, lc, w), lambda b, n: (b, 0, 0))
    kw = 2 * C_KV_W
    vcur = pl.BlockSpec((1, C_KV_W, nq * Q_BLOCK), lambda b, n: (b, 0, n))
    vprev = pl.BlockSpec((1, C_KV_W, Q_BLOCK), lambda b, n: (b, 0, jnp.maximum(nq * n - 1, 0)))
    vnxt = pl.BlockSpec((1, C_KV_W, Q_BLOCK), lambda b, n: (b, 0, jnp.minimum(nq * n + nq, nblk - 1)))
    vctx = pl.BlockSpec((1, C_KV_W, lc), lambda b, n: (b, 0, 0))
    sink_t = jnp.repeat(sink.reshape(C_KV_HEADS, 1, C_GROUP), Q_BLOCK, axis=2)
    return pl.pallas_call(
        _attn_out_kernel,
        grid=(bsz, nstep),
        in_specs=[
            pl.BlockSpec((C_KV_HEADS, 1, C_GROUP * Q_BLOCK), lambda b, n: (0, 0, 0)),
            cur(C_Q_W), prev(kw), cur(kw), nxt(kw), vprev, vcur, vnxt, ctx_blk(kw), vctx,
            cur(C_Q_W), cur(D_MODEL), cur(D_MODEL),
            pl.BlockSpec((1, 1, 3 * D_MODEL), lambda b, n: (b, 0, 0)),
            pl.BlockSpec((C_Q_W, D_MODEL), lambda b, n: (0, 0)),
            pl.BlockSpec((1, D_MODEL), lambda b, n: (0, 0)),
        ],
        out_specs=cur(D_MODEL),
        out_shape=jax.ShapeDtypeStruct((bsz, seq_len, D_MODEL), F32),
        compiler_params=_cparams(("parallel", "arbitrary")),
        name="attn_out",
    )(sink_t, q, k, k, k, v, v, v, kx, vx, sg, xs, upd, mod, w_out, fg)


def kernel(x, c, ctx, c_ctx, norm_g, ada_w, ada_b, w_in_ab, v_norm_g, spatial_w, spatial_b, w_out_ab,
           w_in_c, sink_logit, w_out_c, final_g):
    bsz, seq_len, _ = x.shape
    depth = ada_w.shape[0]
    assert depth == 2 and bsz + 1 <= MOD_ROWS
    ctx_row = bsz

    cc = jnp.concatenate([c, c_ctx[None, :], jnp.zeros((MOD_ROWS - bsz - 1, D_MODEL), F32)], axis=0)
    mod = _mod_call(cc, ada_w, ada_b)
    mod0 = mod[0].reshape(MOD_ROWS, 1, 3 * D_MODEL)
    mod1 = mod[1].reshape(MOD_ROWS, 1, 3 * D_MODEL)

    cs = jnp.asarray(_channel_dft_matrix())
    ng0 = norm_g[0].reshape(1, D_MODEL)
    ng1 = norm_g[1].reshape(1, D_MODEL)
    w_in0 = w_in_ab[0].astype(BF16)
    w_out0 = w_out_ab[0].astype(BF16)
    vg = v_norm_g[0].reshape(1, A_W)
    sw = spatial_w[0].astype(BF16)
    sb = jnp.broadcast_to(spatial_b[0][:, :, None], (A_HEADS, CHUNK, A_HEAD_DIM))

    dx = _mixer_ab_layer(x, mod0, None, ng0, w_in0, vg, sw, sb, w_out0, cs, tl2=256, tk=256)
    dctx = _mixer_ab_layer(ctx, mod0, ctx_row, ng0, w_in0, vg, sw, sb, w_out0, cs,
                           tl2=ctx.shape[1] // RADIX, tk=ctx.shape[1] // RADIX)

    w_in1 = w_in_c[0].astype(BF16)
    w_out1 = w_out_c[0].astype(BF16)
    cos_t, sin_t = _rope_tables(seq_len)
    q, k, v, sg = _in_c_call(x, dx, mod1, ng1, w_in1, jnp.asarray(cos_t), jnp.asarray(sin_t), tm=1024)
    kx, vx = _ctx_kv_call(ctx, dctx, mod1, ctx_row, ng1, w_in1)
    return _attn_out_call(sink_logit[0], q, k, v, kx, vx, sg, x, dx, mod1, w_out1,
                          final_g.reshape(1, D_MODEL))
```

```python
import functools
import math

import numpy as np
import jax
import jax.numpy as jnp
from jax import lax
from jax.experimental import pallas as pl
from jax.experimental.pallas import tpu as pltpu

F32 = jnp.float32
BF16 = jnp.bfloat16

D_MODEL = 1024
GRID_W = 64
CHUNK = 128
A_HEADS = 4
A_HEAD_DIM = 128
A_W = A_HEADS * A_HEAD_DIM
B_GROUPS = 4
B_GROUP_DIM = 128
B_W = B_GROUPS * B_GROUP_DIM
AB_IN = 3 * A_W + 2 * B_W
C_HEADS = 16
C_KV_HEADS = 4
C_GROUP = C_HEADS // C_KV_HEADS
C_HEAD_DIM = 64
C_Q_W = C_HEADS * C_HEAD_DIM
C_KV_W = C_KV_HEADS * C_HEAD_DIM
C_IN = 2 * C_Q_W + 2 * C_KV_W
WINDOW = 128
Q_BLOCK = 128
ROPE_BASE = 10000.0
NORM_EPS = 1e-6
NEG_INF = -1e30
LOG2E = math.log2(math.e)
RADIX = 4
MOD_ROWS = 16
SOFTMAX_ROWS = 64
SCORE_LOOKAHEAD = 2
OUT_PROJ_COLS = 256
PV_LAG = 1
ATTN_Q_BLOCKS = 8
OUT_PROJ_BLOCKS = 2
V7X_VMEM_LIMIT = 56 * 1024 * 1024


def _silu(x):
    return x * (1.0 / (1.0 + jnp.exp(-x)))


def _cparams(sem):
    return pltpu.CompilerParams(dimension_semantics=sem, vmem_limit_bytes=V7X_VMEM_LIMIT)


def _channel_dft_matrix():
    n = np.arange(B_GROUP_DIM)
    ang = 2.0 * np.pi * np.outer(n, n) / B_GROUP_DIM
    return np.concatenate([np.cos(ang), np.sin(ang)], axis=1).astype(np.float32)


def _position_dft_matrix(seq_len):
    n2 = seq_len // RADIX
    idx = np.arange(n2)
    ang = 2.0 * np.pi * (np.outer(idx, idx) % n2) / n2
    norm = 1.0 / math.sqrt(seq_len * B_GROUP_DIM)
    return np.concatenate([np.cos(ang) * norm, -np.sin(ang) * norm], axis=1).astype(np.float32)


def _twiddle_tables(seq_len):
    n2 = seq_len // RADIX
    l2 = np.arange(n2)[None, :, None]
    k1 = np.arange(RADIX)[:, None, None]
    ang = 2.0 * np.pi * ((l2 * k1) % seq_len) / seq_len
    ang = np.broadcast_to(ang, (RADIX, n2, 128))
    return np.cos(ang).astype(np.float32), np.sin(ang).astype(np.float32)


def _rope_tables(seq_len):
    t = np.arange(seq_len)
    row = (t // GRID_W).astype(np.float64)
    col = (t % GRID_W).astype(np.float64)
    lane = np.arange(128)
    dd = lane % C_HEAD_DIM
    nf = C_HEAD_DIM // 4
    inv = ROPE_BASE ** (-(dd % nf).astype(np.float64) / nf)
    pos = np.where((dd < C_HEAD_DIM // 2)[None, :], row[:, None], col[:, None])
    ang = pos * inv[None, :]
    sign = np.where((dd % (2 * nf)) < nf, -1.0, 1.0)[None, :]
    return np.cos(ang).astype(np.float32), (np.sin(ang) * sign).astype(np.float32)


def _mod_kernel(c_ref, w_ref, b_ref, o_ref):
    s = _silu(c_ref[...]).astype(BF16)
    o_ref[0] = jnp.dot(s, w_ref[0].astype(BF16), preferred_element_type=F32) + b_ref[0]


def _mod_call(cc, ada_w, ada_b):
    depth = ada_w.shape[0]
    tn = 1024
    return pl.pallas_call(
        _mod_kernel,
        grid=(depth, 3 * D_MODEL // tn),
        in_specs=[
            pl.BlockSpec((MOD_ROWS, D_MODEL), lambda l, j: (0, 0)),
            pl.BlockSpec((1, D_MODEL, tn), lambda l, j: (l, 0, j)),
            pl.BlockSpec((1, 1, tn), lambda l, j: (l, 0, j)),
        ],
        out_specs=pl.BlockSpec((1, MOD_ROWS, tn), lambda l, j: (l, 0, j)),
        out_shape=jax.ShapeDtypeStruct((depth, MOD_ROWS, 3 * D_MODEL), F32),
        compiler_params=_cparams(("arbitrary", "arbitrary")),
        name="adaln_mod",
    )(cc, ada_w, ada_b.reshape(depth, 1, 3 * D_MODEL))


def _modulated_norm(x, mod_row, g):
    shift = mod_row[:, :D_MODEL]
    scale = mod_row[:, D_MODEL:2 * D_MODEL]
    ms = jnp.mean(x * x, axis=-1, keepdims=True)
    h = x * lax.rsqrt(ms + NORM_EPS) * g
    return h * (1.0 + scale) + shift


def _in_ab_kernel(x_ref, mod_ref, ng_ref, w_ref, vg_ref, sw_ref, sb_ref, cs_ref, twc_ref, tws_ref,
                  ya_ref, sgb_ref, uv_ref, *, tl2, ts):
    rows = RADIX * ts
    piece = min(CHUNK, ts)
    nchunk = rows // CHUNK
    assert nchunk % 2 == 0 and tl2 % ts == 0
    cs = cs_ref[...].astype(BF16)

    nb = x_ref.shape[0]

    def project(o):
        x = x_ref[:, :, o:o + ts, :].reshape(nb * rows, D_MODEL)
        h = _modulated_norm(x, mod_ref[0], ng_ref[...])
        return jnp.dot(h.astype(BF16), w_ref[...], preferred_element_type=F32)

    def mix(bb, o, z):
        v = z[:, A_W:2 * A_W]
        mu = jnp.mean(v, axis=-1, keepdims=True)
        vc = v - mu
        var = jnp.mean(vc * vc, axis=-1, keepdims=True)
        vn = (vc * lax.rsqrt(var + NORM_EPS) * vg_ref[...]).astype(BF16)
        for hd in range(A_HEADS):
            c0 = hd * A_HEAD_DIM
            for cp in range(nchunk // 2):
                ra, rb = 2 * cp * CHUNK, (2 * cp + 1) * CHUNK
                vpair = jnp.concatenate([vn[ra:ra + CHUNK, c0:c0 + A_HEAD_DIM],
                                         vn[rb:rb + CHUNK, c0:c0 + A_HEAD_DIM]], axis=1)
                sv2 = jnp.dot(sw_ref[hd], vpair, preferred_element_type=F32)
                for half, r0 in enumerate((ra, rb)):
                    sv = sv2[:, half * A_HEAD_DIM:(half + 1) * A_HEAD_DIM] + sb_ref[hd]
                    u = z[r0:r0 + CHUNK, c0:c0 + A_HEAD_DIM]
                    ga = z[r0:r0 + CHUNK, 2 * A_W + c0:2 * A_W + c0 + A_HEAD_DIM]
                    ya = (u * sv * _silu(ga)).astype(BF16)
                    for p0 in range(0, CHUNK, piece):
                        l1, off = divmod(r0 + p0, ts)
                        ya_ref[bb, l1, o + off:o + off + piece, c0:c0 + A_HEAD_DIM] = ya[p0:p0 + piece]

        gb = z[:, 3 * A_W + B_W:]
        sgb_ref[bb, :, o:o + ts, :] = _silu(gb).astype(BF16).reshape(RADIX, ts, B_W)

        xb = z[:, 3 * A_W:3 * A_W + B_W].astype(BF16)
        for g in range(B_GROUPS):
            c0 = g * B_GROUP_DIM
            ps, qs = [], []
            for j in range(RADIX):
                pq = jnp.dot(xb[j * ts:(j + 1) * ts, c0:c0 + B_GROUP_DIM], cs,
                             preferred_element_type=F32)
                ps.append(pq[:, :B_GROUP_DIM])
                qs.append(pq[:, B_GROUP_DIM:])
            p02, p13 = ps[0] - ps[2], ps[1] - ps[3]
            q02, q13 = qs[0] - qs[2], qs[1] - qs[3]
            pe, po = ps[0] + ps[2], ps[1] + ps[3]
            qe, qo = qs[0] + qs[2], qs[1] + qs[3]
            us = [pe + po, p02 - q13, pe - po, p02 + q13]
            vs = [qe + qo, q02 + p13, qe - qo, q02 - p13]
            for k1 in range(RADIX):
                if k1 == 0:
                    ut, vt = us[0], vs[0]
                else:
                    tc, tsn = twc_ref[k1, o:o + ts, :], tws_ref[k1, o:o + ts, :]
                    ut = us[k1] * tc - vs[k1] * tsn
                    vt = us[k1] * tsn + vs[k1] * tc
                uv_ref[bb, k1, 0, o:o + ts, c0:c0 + B_GROUP_DIM] = ut.astype(BF16)
                uv_ref[bb, k1, 1, o:o + ts, c0:c0 + B_GROUP_DIM] = vt.astype(BF16)

    for o in range(0, tl2, ts):
        z = project(o)
        for bb in range(nb):
            mix(bb, o, z[bb * rows:(bb + 1) * rows])


def _streams_per_step(bsz, mod_row):
    return 4 if (mod_row is not None and bsz % 4 == 0) else 1


def _in_ab_call(xs, mod, mod_row, ng, w_in, vg, sw, sb, cs, twc, tws, tl2):
    bsz, seq_len, _ = xs.shape
    n2 = seq_len // RADIX
    x4 = xs.reshape(bsz, RADIX, n2, D_MODEL)
    row_of = (lambda b: b) if mod_row is None else (lambda b: mod_row)
    const2 = lambda b, i: (0, 0)
    const3 = lambda b, i: (0, 0, 0)
    nb = _streams_per_step(bsz, mod_row)
    return pl.pallas_call(
        functools.partial(_in_ab_kernel, tl2=tl2, ts=tl2),
        grid=(bsz // nb, n2 // tl2),
        in_specs=[
            pl.BlockSpec((nb, RADIX, tl2, D_MODEL), lambda b, i: (b, 0, i, 0)),
            pl.BlockSpec((1, 1, 3 * D_MODEL), lambda b, i: (row_of(b), 0, 0)),
            pl.BlockSpec((1, D_MODEL), const2),
            pl.BlockSpec((D_MODEL, AB_IN), const2),
            pl.BlockSpec((1, A_W), const2),
            pl.BlockSpec((A_HEADS, CHUNK, CHUNK), const3),
            pl.BlockSpec((A_HEADS, CHUNK, A_HEAD_DIM), const3),
            pl.BlockSpec((B_GROUP_DIM, 2 * B_GROUP_DIM), const2),
            pl.BlockSpec((RADIX, tl2, 128), lambda b, i: (0, i, 0)),
            pl.BlockSpec((RADIX, tl2, 128), lambda b, i: (0, i, 0)),
        ],
        out_specs=[
            pl.BlockSpec((nb, RADIX, tl2, A_W), lambda b, i: (b, 0, i, 0)),
            pl.BlockSpec((nb, RADIX, tl2, B_W), lambda b, i: (b, 0, i, 0)),
            pl.BlockSpec((nb, RADIX, 2, tl2, B_W), lambda b, i: (b, 0, 0, i, 0)),
        ],
        out_shape=[
            jax.ShapeDtypeStruct((bsz, RADIX, n2, A_W), BF16),
            jax.ShapeDtypeStruct((bsz, RADIX, n2, B_W), BF16),
            jax.ShapeDtypeStruct((bsz, RADIX, 2, n2, B_W), BF16),
        ],
        compiler_params=_cparams(("parallel", "arbitrary")),
        name="in_ab",
    )(x4, mod, ng, w_in, vg, sw, sb, cs, twc, tws)


def _dft_out_kernel(g_ref, uv_ref, ya_ref, sgb_ref, x_ref, mod_ref, w_ref, o_ref, f_ref, *, tk):
    nb, rows, _ = x_ref.shape
    r0 = pl.multiple_of(pl.program_id(1) * tk, tk)
    gmat = g_ref[pl.ds(r0, tk), :].astype(BF16)
    for bb in range(nb):
        for k1 in range(RADIX):
            f = jnp.dot(gmat, uv_ref[bb, k1], preferred_element_type=F32)
            for g in range(B_GROUPS):
                f_ref[g, pl.ds(bb * rows + k1, tk, stride=RADIX), :] = f[:, g * B_GROUP_DIM:(g + 1) * B_GROUP_DIM]
    fnat = jnp.concatenate([f_ref[g] for g in range(B_GROUPS)], axis=1)
    yb = (fnat * sgb_ref[...].reshape(nb * rows, B_W).astype(F32)).astype(BF16)
    y = jnp.dot(ya_ref[...].reshape(nb * rows, A_W), w_ref[:A_W], preferred_element_type=F32)
    y = y + jnp.dot(yb, w_ref[A_W:], preferred_element_type=F32)
    gate = mod_ref[0][:, 2 * D_MODEL:]
    o_ref[...] = x_ref[...] + (gate * y).reshape(nb, rows, D_MODEL)


def _dft_out_call(gmat, uv, ya, sgb, xs, mod, mod_row, w_out, tk):
    bsz, seq_len, _ = xs.shape
    n2 = seq_len // RADIX
    rows = RADIX * tk
    row_of = (lambda b: b) if mod_row is None else (lambda b: mod_row)
    nb = _streams_per_step(bsz, mod_row)
    nat = lambda w: pl.BlockSpec((nb, rows, w), lambda b, i: (b, i, 0))
    return pl.pallas_call(
        functools.partial(_dft_out_kernel, tk=tk),
        grid=(bsz // nb, n2 // tk),
        in_specs=[
            pl.BlockSpec((n2, 2 * n2), lambda b, i: (0, 0), pipeline_mode=pl.Buffered(1)),
            pl.BlockSpec((nb, RADIX, 2 * n2, B_W), lambda b, i: (b, 0, 0, 0)),
            nat(A_W), nat(B_W), nat(D_MODEL),
            pl.BlockSpec((1, 1, 3 * D_MODEL), lambda b, i: (row_of(b), 0, 0)),
            pl.BlockSpec((A_W + B_W, D_MODEL), lambda b, i: (0, 0), pipeline_mode=pl.Buffered(1)),
        ],
        out_specs=nat(D_MODEL),
        out_shape=jax.ShapeDtypeStruct((bsz, seq_len, D_MODEL), F32),
        scratch_shapes=[pltpu.VMEM((B_GROUPS, nb * rows, B_GROUP_DIM), F32)],
        compiler_params=_cparams(("parallel", "arbitrary")),
        name="dft_out",
    )(gmat, uv.reshape(bsz, RADIX, 2 * n2, B_W), ya.reshape(bsz, seq_len, A_W),
      sgb.reshape(bsz, seq_len, B_W), xs, mod, w_out)


def _mixer_ab_layer(xs, mod, mod_row, ng, w_in, vg, sw, sb, w_out, cs, tl2, tk):
    seq_len = xs.shape[1]
    twc, tws = _twiddle_tables(seq_len)
    gmat = jnp.asarray(_position_dft_matrix(seq_len))
    ya, sgb, uv = _in_ab_call(xs, mod, mod_row, ng, w_in, vg, sw, sb, cs,
                              jnp.asarray(twc), jnp.asarray(tws), tl2)
    return _dft_out_call(gmat, uv, ya, sgb, xs, mod, mod_row, w_out, tk=tk)


def _rope_block(t, cos, sin_signed, lane_lo):
    nf = C_HEAD_DIM // 4
    swapped = jnp.where(lane_lo, pltpu.roll(t, 128 - nf, axis=1), pltpu.roll(t, nf, axis=1))
    return t * cos + swapped * sin_signed


def _store_dup_heads(k_ref, c, t, lane):
    r = pltpu.roll(t, C_HEAD_DIM, axis=1)
    first = lane < C_HEAD_DIM
    k_ref[0, :, (2 * c) * 128:(2 * c + 1) * 128] = jnp.where(first, t, r).astype(BF16)
    k_ref[0, :, (2 * c + 1) * 128:(2 * c + 2) * 128] = jnp.where(first, r, t).astype(BF16)


def _in_c_kernel(x_ref, mod_ref, ng_ref, w_ref, cos_ref, sin_ref, q_ref, k_ref, v_ref, sg_ref):
    h = _modulated_norm(x_ref[0], mod_ref[0], ng_ref[...])
    z = jnp.dot(h.astype(BF16), w_ref[...], preferred_element_type=F32)
    cos = cos_ref[...]
    sin = sin_ref[...]
    lane = lax.broadcasted_iota(jnp.int32, cos.shape, 1)
    lane_lo = (lane % (C_HEAD_DIM // 2)) < (C_HEAD_DIM // 4)
    qscale = C_HEAD_DIM ** -0.5 * LOG2E
    for c in range(C_Q_W // 128):
        t = _rope_block(z[:, c * 128:(c + 1) * 128], cos, sin, lane_lo)
        q_ref[0, :, c * 128:(c + 1) * 128] = (t * qscale).astype(BF16)
    for c in range(C_KV_W // 128):
        t = _rope_block(z[:, C_Q_W + c * 128:C_Q_W + (c + 1) * 128], cos, sin, lane_lo)
        _store_dup_heads(k_ref, c, t, lane)
    v_ref[0] = z[:, C_Q_W + C_KV_W:C_Q_W + 2 * C_KV_W].T.astype(BF16)
    sg_ref[0] = _silu(z[:, C_Q_W + 2 * C_KV_W:]).astype(BF16)


def _in_c_call(xs, mod, ng, w_in, cos_t, sin_t, tm):
    bsz, seq_len, _ = xs.shape
    const2 = lambda b, i: (0, 0)
    row_blk = lambda w: pl.BlockSpec((1, tm, w), lambda b, i: (b, i, 0))
    return pl.pallas_call(
        _in_c_kernel,
        grid=(bsz, seq_len // tm),
        in_specs=[
            row_blk(D_MODEL),
            pl.BlockSpec((1, 1, 3 * D_MODEL), lambda b, i: (b, 0, 0)),
            pl.BlockSpec((1, D_MODEL), const2),
            pl.BlockSpec((D_MODEL, C_IN), const2),
            pl.BlockSpec((tm, 128), lambda b, i: (i, 0)),
            pl.BlockSpec((tm, 128), lambda b, i: (i, 0)),
        ],
        out_specs=[row_blk(C_Q_W), row_blk(2 * C_KV_W),
                   pl.BlockSpec((1, C_KV_W, tm), lambda b, i: (b, 0, i)), row_blk(C_Q_W)],
        out_shape=[
            jax.ShapeDtypeStruct((bsz, seq_len, C_Q_W), BF16),
            jax.ShapeDtypeStruct((bsz, seq_len, 2 * C_KV_W), BF16),
            jax.ShapeDtypeStruct((bsz, C_KV_W, seq_len), BF16),
            jax.ShapeDtypeStruct((bsz, seq_len, C_Q_W), BF16),
        ],
        compiler_params=_cparams(("parallel", "arbitrary")),
        name="in_c",
    )(xs, mod, ng, w_in, cos_t, sin_t)


def _ctx_kv_kernel(x_ref, mod_ref, ng_ref, w_ref, k_ref, v_ref):
    nb, lc, _ = x_ref.shape
    h = _modulated_norm(x_ref[...].reshape(nb * lc, D_MODEL), mod_ref[0], ng_ref[...])
    z = jnp.dot(h.astype(BF16), w_ref[...], preferred_element_type=F32)
    lane = lax.broadcasted_iota(jnp.int32, (lc, 128), 1)
    for bb in range(nb):
        zb = z[bb * lc:(bb + 1) * lc]
        for c in range(C_KV_W // 128):
            _store_dup_heads(k_ref.at[pl.ds(bb, 1)], c, zb[:, c * 128:(c + 1) * 128], lane)
        v_ref[bb] = zb[:, C_KV_W:].T.astype(BF16)


def _ctx_kv_call(ctx, mod, mod_row, ng, w_kv):
    bsz, lc, _ = ctx.shape
    nb = 4 if bsz % 4 == 0 else 1
    return pl.pallas_call(
        _ctx_kv_kernel,
        grid=(bsz // nb,),
        in_specs=[
            pl.BlockSpec((nb, lc, D_MODEL), lambda b: (b, 0, 0)),
            pl.BlockSpec((1, 1, 3 * D_MODEL), lambda b: (mod_row, 0, 0)),
            pl.BlockSpec((1, D_MODEL), lambda b: (0, 0)),
            pl.BlockSpec((D_MODEL, 2 * C_KV_W), lambda b: (0, C_Q_W // (2 * C_KV_W))),
        ],
        out_specs=[pl.BlockSpec((nb, lc, 2 * C_KV_W), lambda b: (b, 0, 0)),
                   pl.BlockSpec((nb, C_KV_W, lc), lambda b: (b, 0, 0))],
        out_shape=[jax.ShapeDtypeStruct((bsz, lc, 2 * C_KV_W), BF16),
                   jax.ShapeDtypeStruct((bsz, C_KV_W, lc), BF16)],
        compiler_params=_cparams(("parallel",)),
        name="ctx_kv",
    )(ctx, mod, ng, w_kv)


def _attn_out_kernel(sink_ref, q_ref, kp_ref, kc_ref, kn_ref, vp_ref, vc_ref, vn_ref, kx_ref, vx_ref,
                     sg_ref, x_ref, mod_ref, w_ref, fg_ref, o_ref):
    n = pl.program_id(1)
    last = pl.num_programs(1) - 1
    gq = C_GROUP * Q_BLOCK
    kj = lax.broadcasted_iota(jnp.int32, (Q_BLOCK, Q_BLOCK), 0)
    qi = lax.broadcasted_iota(jnp.int32, (Q_BLOCK, Q_BLOCK), 1)
    bias_before = jnp.where(kj >= qi, 0.0, NEG_INF).astype(F32)
    bias_after = jnp.where(kj <= qi, 0.0, NEG_INF).astype(F32)
    edge_first = jnp.where(n > 0, 0.0, NEG_INF).astype(F32)
    edge_last = jnp.where(n < last, 0.0, NEG_INF).astype(F32)
    lane = lax.broadcasted_iota(jnp.int32, (Q_BLOCK, 128), 1)
    first = lane < C_HEAD_DIM
    zero = jnp.zeros((Q_BLOCK, 128), BF16)
    row_k = lax.broadcasted_iota(jnp.int32, (128, kx_ref.shape[1] + 3 * Q_BLOCK), 0)
    nq = q_ref.shape[1] // Q_BLOCK

    def window(prev_ref, cur_ref, next_ref, j, lanes):
        if j < 0:
            return prev_ref[0, :, lanes]
        if j >= nq:
            return next_ref[0, :, lanes]
        return cur_ref[0, j * Q_BLOCK:(j + 1) * Q_BLOCK, lanes]

    def scores(qb, kh):
        kl = slice(kh * 128, (kh + 1) * 128)
        kblocks = [kx_ref[0, :, kl]] + [window(kp_ref, kc_ref, kn_ref, j, kl) for j in (qb - 1, qb, qb + 1)]
        biases = (None, (bias_before, edge_first if qb == 0 else None), None,
                  (bias_after, edge_last if qb == nq - 1 else None))
        rows_q = slice(qb * Q_BLOCK, (qb + 1) * Q_BLOCK)
        q4 = []
        for c in range(2):
            qv = q_ref[0, rows_q, kh * 256 + c * 128:kh * 256 + (c + 1) * 128]
            q4 += [jnp.where(first, qv, zero), jnp.where(first, zero, qv)]
        q4 = jnp.concatenate(q4, axis=0)
        k2 = jnp.concatenate(kblocks, axis=0)
        st = lax.dot_general(k2, q4, (((1,), (1,)), ((), ())), preferred_element_type=F32)
        blocks, m8, r = [], None, 0
        for kb, bias in zip(kblocks, biases):
            sb = st[r:r + kb.shape[0]]
            r += kb.shape[0]
            if bias is not None:
                mask_bias, edge = bias
                cols = [sb[:, j * Q_BLOCK:(j + 1) * Q_BLOCK] + mask_bias for j in range(C_GROUP)]
                sb = jnp.concatenate(cols, axis=1)
                if edge is not None:
                    sb = sb + edge
            blocks.append(sb)
            mb = jnp.max(sb.reshape(sb.shape[0] // 8, 8, gq), axis=0)
            m8 = mb if m8 is None else jnp.maximum(m8, mb)
        return blocks, m8

    def softmax(kh, blocks, m8):
        sink2 = sink_ref[kh] * LOG2E
        m = jnp.maximum(jnp.max(m8, axis=0, keepdims=True), sink2)
        pts = []
        for blk in blocks:
            for r in range(0, blk.shape[0], SOFTMAX_ROWS):
                pts.append(jnp.exp2(blk[r:r + SOFTMAX_ROWS] - m).astype(BF16))
        return jnp.concatenate(pts, axis=0), jnp.exp2(sink2 - m)

    def weighted_values(qb, kh, pt, sink_term):
        vrows = slice((kh // 2) * 128, (kh // 2 + 1) * 128)
        vblocks = [vx_ref[0, vrows, :]]
        for j in (qb - 1, qb, qb + 1):
            if j < 0:
                vblocks.append(vp_ref[0, vrows, :])
            elif j >= nq:
                vblocks.append(vn_ref[0, vrows, :])
            else:
                vblocks.append(vc_ref[0, vrows, j * Q_BLOCK:(j + 1) * Q_BLOCK])
        vpair = jnp.concatenate(vblocks, axis=1)
        own = (row_k < C_HEAD_DIM) if kh % 2 == 0 else (row_k >= C_HEAD_DIM)
        vsum = jnp.where(own, vpair, jnp.ones_like(vpair))
        ot = jnp.dot(vsum, pt, preferred_element_type=F32)
        r0 = (kh % 2) * C_HEAD_DIM
        r1 = C_HEAD_DIM - r0
        denom = ot[r1:r1 + C_HEAD_DIM] + sink_term
        ot = ot[r0:r0 + C_HEAD_DIM] / denom
        cols = []
        for c in range(2):
            pair = jnp.concatenate([ot[:, (2 * c) * Q_BLOCK:(2 * c + 1) * Q_BLOCK],
                                    ot[:, (2 * c + 1) * Q_BLOCK:(2 * c + 2) * Q_BLOCK]], axis=0)
            cols.append(pair.T)
        return cols

    chains = [(qb, kh) for qb in range(nq) for kh in range(C_KV_HEADS)]
    o_cols = {qb: [] for qb in range(nq)}
    gate = mod_ref[0][:, 2 * D_MODEL:]

    def finish_tasks(qb0, qb1):
        rows = slice(qb0 * Q_BLOCK, qb1 * Q_BLOCK)
        state = {}

        def gate_values():
            o_all = jnp.concatenate([jnp.concatenate(o_cols[qb], axis=1) for qb in range(qb0, qb1)], axis=0)
            state["o"] = (o_all * sg_ref[0, rows, :].astype(F32)).astype(BF16)
            state["y"] = []

        def project(c0):
            state["y"].append(jnp.dot(state["o"], w_ref[:, c0:c0 + OUT_PROJ_COLS],
                                      preferred_element_type=F32))

        def residual_norm():
            y = jnp.concatenate(state["y"], axis=1)
            x2 = x_ref[0, rows, :] + gate * y
            ms = jnp.mean(x2 * x2, axis=-1, keepdims=True)
            o_ref[0, rows, :] = x2 * lax.rsqrt(ms + NORM_EPS) * fg_ref[...]

        tasks = [gate_values]
        tasks += [functools.partial(project, c0) for c0 in range(0, D_MODEL, OUT_PROJ_COLS)]
        return tasks + [residual_norm]

    pending = [scores(*ch) for ch in chains[:SCORE_LOOKAHEAD]]
    probs = []
    deferred = []

    def values_for(idx):
        qb, kh = chains[idx]
        o_cols[qb] += weighted_values(qb, kh, *probs.pop(0))
        if kh == C_KV_HEADS - 1 and qb % OUT_PROJ_BLOCKS == OUT_PROJ_BLOCKS - 1:
            deferred.extend(finish_tasks(qb + 1 - OUT_PROJ_BLOCKS, qb + 1))

    for idx, (qb, kh) in enumerate(chains):
        if idx + SCORE_LOOKAHEAD < len(chains):
            pending.append(scores(*chains[idx + SCORE_LOOKAHEAD]))
        probs.append(softmax(kh, *pending.pop(0)))
        if idx >= PV_LAG:
            values_for(idx - PV_LAG)
        if deferred:
            deferred.pop(0)()
    for idx in range(len(chains) - PV_LAG, len(chains)):
        values_for(idx)
    for task in deferred:
        task()


def _attn_out_call(sink, q, k, v, kx, vx, sg, xs, mod, w_out, fg):
    bsz, seq_len, _ = xs.shape
    nblk = seq_len // Q_BLOCK
    nq = ATTN_Q_BLOCKS
    nstep = nblk // nq
    lc = kx.shape[1]
    assert lc % SOFTMAX_ROWS == 0 and Q_BLOCK % SOFTMAX_ROWS == 0 and nq % OUT_PROJ_BLOCKS == 0
    cur = lambda w: pl.BlockSpec((1, nq * Q_BLOCK, w), lambda b, n: (b, n, 0))
    prev = lambda w: pl.BlockSpec((1, Q_BLOCK, w), lambda b, n: (b, jnp.maximum(nq * n - 1, 0), 0))
    nxt = lambda w: pl.BlockSpec((1, Q_BLOCK, w), lambda b, n: (b, jnp.minimum(nq * n + nq, nblk - 1), 0))
    ctx_blk = lambda w: pl.BlockSpec((1, lc, w), lambda b, n: (b, 0, 0))
    kw = 2 * C_KV_W
    vcur = pl.BlockSpec((1, C_KV_W, nq * Q_BLOCK), lambda b, n: (b, 0, n))
    vprev = pl.BlockSpec((1, C_KV_W, Q_BLOCK), lambda b, n: (b, 0, jnp.maximum(nq * n - 1, 0)))
    vnxt = pl.BlockSpec((1, C_KV_W, Q_BLOCK), lambda b, n: (b, 0, jnp.minimum(nq * n + nq, nblk - 1)))
    vctx = pl.BlockSpec((1, C_KV_W, lc), lambda b, n: (b, 0, 0))
    sink_t = jnp.repeat(sink.reshape(C_KV_HEADS, 1, C_GROUP), Q_BLOCK, axis=2)
    return pl.pallas_call(
        _attn_out_kernel,
        grid=(bsz, nstep),
        in_specs=[
            pl.BlockSpec((C_KV_HEADS, 1, C_GROUP * Q_BLOCK), lambda b, n: (0, 0, 0)),
            cur(C_Q_W), prev(kw), cur(kw), nxt(kw), vprev, vcur, vnxt, ctx_blk(kw), vctx,
            cur(C_Q_W), cur(D_MODEL),
            pl.BlockSpec((1, 1, 3 * D_MODEL), lambda b, n: (b, 0, 0)),
            pl.BlockSpec((C_Q_W, D_MODEL), lambda b, n: (0, 0)),
            pl.BlockSpec((1, D_MODEL), lambda b, n: (0, 0)),
        ],
        out_specs=cur(D_MODEL),
        out_shape=jax.ShapeDtypeStruct((bsz, seq_len, D_MODEL), F32),
        compiler_params=_cparams(("parallel", "arbitrary")),
        name="attn_out",
    )(sink_t, q, k, k, k, v, v, v, kx, vx, sg, xs, mod, w_out, fg)


def kernel(x, c, ctx, c_ctx, norm_g, ada_w, ada_b, w_in_ab, v_norm_g, spatial_w, spatial_b, w_out_ab,
           w_in_c, sink_logit, w_out_c, final_g):
    bsz, seq_len, _ = x.shape
    depth = ada_w.shape[0]
    assert depth == 2 and bsz + 1 <= MOD_ROWS
    ctx_row = bsz

    cc = jnp.concatenate([c, c_ctx[None, :], jnp.zeros((MOD_ROWS - bsz - 1, D_MODEL), F32)], axis=0)
    mod = _mod_call(cc, ada_w, ada_b)
    mod0 = mod[0].reshape(MOD_ROWS, 1, 3 * D_MODEL)
    mod1 = mod[1].reshape(MOD_ROWS, 1, 3 * D_MODEL)

    cs = jnp.asarray(_channel_dft_matrix())
    ng0 = norm_g[0].reshape(1, D_MODEL)
    ng1 = norm_g[1].reshape(1, D_MODEL)
    w_in0 = w_in_ab[0].astype(BF16)
    w_out0 = w_out_ab[0].astype(BF16)
    vg = v_norm_g[0].reshape(1, A_W)
    sw = spatial_w[0].astype(BF16)
    sb = jnp.broadcast_to(spatial_b[0][:, :, None], (A_HEADS, CHUNK, A_HEAD_DIM))

    x1 = _mixer_ab_layer(x, mod0, None, ng0, w_in0, vg, sw, sb, w_out0, cs, tl2=256, tk=256)
    ctx1 = _mixer_ab_layer(ctx, mod0, ctx_row, ng0, w_in0, vg, sw, sb, w_out0, cs,
                           tl2=ctx.shape[1] // RADIX, tk=ctx.shape[1] // RADIX)

    w_in1 = w_in_c[0].astype(BF16)
    w_out1 = w_out_c[0].astype(BF16)
    cos_t, sin_t = _rope_tables(seq_len)
    q, k, v, sg = _in_c_call(x1, mod1, ng1, w_in1, jnp.asarray(cos_t), jnp.asarray(sin_t), tm=1024)
    kx, vx = _ctx_kv_call(ctx1, mod1, ctx_row, ng1, w_in1)
    return _attn_out_call(sink_logit[0], q, k, v, kx, vx, sg, x1, mod1, w_out1,
                          final_g.reshape(1, D_MODEL))
```

```python
import functools
import math

import numpy as np
import jax
import jax.numpy as jnp
from jax import lax
from jax.experimental import pallas as pl
from jax.experimental.pallas import tpu as pltpu

F32 = jnp.float32
BF16 = jnp.bfloat16

D_MODEL = 1024
GRID_W = 64
CHUNK = 128
A_HEADS = 4
A_HEAD_DIM = 128
A_W = A_HEADS * A_HEAD_DIM
B_GROUPS = 4
B_GROUP_DIM = 128
B_W = B_GROUPS * B_GROUP_DIM
AB_IN = 3 * A_W + 2 * B_W
C_HEADS = 16
C_KV_HEADS = 4
C_GROUP = C_HEADS // C_KV_HEADS
C_HEAD_DIM = 64
C_Q_W = C_HEADS * C_HEAD_DIM
C_KV_W = C_KV_HEADS * C_HEAD_DIM
C_IN = 2 * C_Q_W + 2 * C_KV_W
WINDOW = 128
Q_BLOCK = 128
ROPE_BASE = 10000.0
NORM_EPS = 1e-6
NEG_INF = -1e30
LOG2E = math.log2(math.e)
RADIX = 4
MOD_ROWS = 16
SCORE_LOOKAHEAD = 2
OUT_PROJ_COLS = 256
PV_LAG = 1
ATTN_Q_BLOCKS = 4
OUT_PROJ_BLOCKS = 2
V7X_VMEM_LIMIT = 56 * 1024 * 1024


def _silu(x):
    return x * (1.0 / (1.0 + jnp.exp(-x)))


def _cparams(sem):
    return pltpu.CompilerParams(dimension_semantics=sem, vmem_limit_bytes=V7X_VMEM_LIMIT)


def _channel_dft_matrix():
    n = np.arange(B_GROUP_DIM)
    ang = 2.0 * np.pi * np.outer(n, n) / B_GROUP_DIM
    return np.concatenate([np.cos(ang), np.sin(ang)], axis=1).astype(np.float32)


def _position_dft_matrix(seq_len):
    n2 = seq_len // RADIX
    idx = np.arange(n2)
    ang = 2.0 * np.pi * (np.outer(idx, idx) % n2) / n2
    norm = 1.0 / math.sqrt(seq_len * B_GROUP_DIM)
    return np.concatenate([np.cos(ang) * norm, -np.sin(ang) * norm], axis=1).astype(np.float32)


def _twiddle_tables(seq_len):
    n2 = seq_len // RADIX
    l2 = np.arange(n2)[None, :, None]
    k1 = np.arange(RADIX)[:, None, None]
    ang = 2.0 * np.pi * ((l2 * k1) % seq_len) / seq_len
    ang = np.broadcast_to(ang, (RADIX, n2, 128))
    return np.cos(ang).astype(np.float32), np.sin(ang).astype(np.float32)


def _rope_tables(seq_len):
    t = np.arange(seq_len)
    row = (t // GRID_W).astype(np.float64)
    col = (t % GRID_W).astype(np.float64)
    lane = np.arange(128)
    dd = lane % C_HEAD_DIM
    nf = C_HEAD_DIM // 4
    inv = ROPE_BASE ** (-(dd % nf).astype(np.float64) / nf)
    pos = np.where((dd < C_HEAD_DIM // 2)[None, :], row[:, None], col[:, None])
    ang = pos * inv[None, :]
    sign = np.where((dd % (2 * nf)) < nf, -1.0, 1.0)[None, :]
    return np.cos(ang).astype(np.float32), (np.sin(ang) * sign).astype(np.float32)


def _mod_kernel(c_ref, w_ref, b_ref, o_ref):
    s = _silu(c_ref[...]).astype(BF16)
    o_ref[0] = jnp.dot(s, w_ref[0].astype(BF16), preferred_element_type=F32) + b_ref[0]


def _mod_call(cc, ada_w, ada_b):
    depth = ada_w.shape[0]
    tn = 1024
    return pl.pallas_call(
        _mod_kernel,
        grid=(depth, 3 * D_MODEL // tn),
        in_specs=[
            pl.BlockSpec((MOD_ROWS, D_MODEL), lambda l, j: (0, 0)),
            pl.BlockSpec((1, D_MODEL, tn), lambda l, j: (l, 0, j)),
            pl.BlockSpec((1, 1, tn), lambda l, j: (l, 0, j)),
        ],
        out_specs=pl.BlockSpec((1, MOD_ROWS, tn), lambda l, j: (l, 0, j)),
        out_shape=jax.ShapeDtypeStruct((depth, MOD_ROWS, 3 * D_MODEL), F32),
        compiler_params=_cparams(("arbitrary", "arbitrary")),
        name="adaln_mod",
    )(cc, ada_w, ada_b.reshape(depth, 1, 3 * D_MODEL))


def _modulated_norm(x, mod_row, g):
    shift = mod_row[:, :D_MODEL]
    scale = mod_row[:, D_MODEL:2 * D_MODEL]
    ms = jnp.mean(x * x, axis=-1, keepdims=True)
    h = x * lax.rsqrt(ms + NORM_EPS) * g
    return h * (1.0 + scale) + shift


def _in_ab_kernel(x_ref, mod_ref, ng_ref, w_ref, vg_ref, sw_ref, sb_ref, cs_ref, twc_ref, tws_ref,
                  ya_ref, sgb_ref, uv_ref, *, tl2, ts):
    rows = RADIX * ts
    piece = min(CHUNK, ts)
    nchunk = rows // CHUNK
    assert nchunk % 2 == 0 and tl2 % ts == 0
    cs = cs_ref[...].astype(BF16)

    nb = x_ref.shape[0]

    def project(o):
        x = x_ref[:, :, o:o + ts, :].reshape(nb * rows, D_MODEL)
        h = _modulated_norm(x, mod_ref[0], ng_ref[...])
        return jnp.dot(h.astype(BF16), w_ref[...], preferred_element_type=F32)

    def mix(bb, o, z):
        v = z[:, A_W:2 * A_W]
        mu = jnp.mean(v, axis=-1, keepdims=True)
        vc = v - mu
        var = jnp.mean(vc * vc, axis=-1, keepdims=True)
        vn = (vc * lax.rsqrt(var + NORM_EPS) * vg_ref[...]).astype(BF16)
        for hd in range(A_HEADS):
            c0 = hd * A_HEAD_DIM
            for cp in range(nchunk // 2):
                ra, rb = 2 * cp * CHUNK, (2 * cp + 1) * CHUNK
                vpair = jnp.concatenate([vn[ra:ra + CHUNK, c0:c0 + A_HEAD_DIM],
                                         vn[rb:rb + CHUNK, c0:c0 + A_HEAD_DIM]], axis=1)
                sv2 = jnp.dot(sw_ref[hd], vpair, preferred_element_type=F32)
                for half, r0 in enumerate((ra, rb)):
                    sv = sv2[:, half * A_HEAD_DIM:(half + 1) * A_HEAD_DIM] + sb_ref[hd]
                    u = z[r0:r0 + CHUNK, c0:c0 + A_HEAD_DIM]
                    ga = z[r0:r0 + CHUNK, 2 * A_W + c0:2 * A_W + c0 + A_HEAD_DIM]
                    ya = (u * sv * _silu(ga)).astype(BF16)
                    for p0 in range(0, CHUNK, piece):
                        l1, off = divmod(r0 + p0, ts)
                        ya_ref[bb, l1, o + off:o + off + piece, c0:c0 + A_HEAD_DIM] = ya[p0:p0 + piece]

        gb = z[:, 3 * A_W + B_W:]
        sgb_ref[bb, :, o:o + ts, :] = _silu(gb).astype(BF16).reshape(RADIX, ts, B_W)

        xb = z[:, 3 * A_W:3 * A_W + B_W].astype(BF16)
        for g in range(B_GROUPS):
            c0 = g * B_GROUP_DIM
            ps, qs = [], []
            for j in range(RADIX):
                pq = jnp.dot(xb[j * ts:(j + 1) * ts, c0:c0 + B_GROUP_DIM], cs,
                             preferred_element_type=F32)
                ps.append(pq[:, :B_GROUP_DIM])
                qs.append(pq[:, B_GROUP_DIM:])
            p02, p13 = ps[0] - ps[2], ps[1] - ps[3]
            q02, q13 = qs[0] - qs[2], qs[1] - qs[3]
            pe, po = ps[0] + ps[2], ps[1] + ps[3]
            qe, qo = qs[0] + qs[2], qs[1] + qs[3]
            us = [pe + po, p02 - q13, pe - po, p02 + q13]
            vs = [qe + qo, q02 + p13, qe - qo, q02 - p13]
            for k1 in range(RADIX):
                if k1 == 0:
                    ut, vt = us[0], vs[0]
                else:
                    tc, tsn = twc_ref[k1, o:o + ts, :], tws_ref[k1, o:o + ts, :]
                    ut = us[k1] * tc - vs[k1] * tsn
                    vt = us[k1] * tsn + vs[k1] * tc
                uv_ref[bb, k1, 0, o:o + ts, c0:c0 + B_GROUP_DIM] = ut.astype(BF16)
                uv_ref[bb, k1, 1, o:o + ts, c0:c0 + B_GROUP_DIM] = vt.astype(BF16)

    for o in range(0, tl2, ts):
        z = project(o)
        for bb in range(nb):
            mix(bb, o, z[bb * rows:(bb + 1) * rows])


def _streams_per_step(bsz, mod_row):
    return 4 if (mod_row is not None and bsz % 4 == 0) else 1


def _in_ab_call(xs, mod, mod_row, ng, w_in, vg, sw, sb, cs, twc, tws, tl2):
    bsz, seq_len, _ = xs.shape
    n2 = seq_len // RADIX
    x4 = xs.reshape(bsz, RADIX, n2, D_MODEL)
    row_of = (lambda b: b) if mod_row is None else (lambda b: mod_row)
    const2 = lambda b, i: (0, 0)
    const3 = lambda b, i: (0, 0, 0)
    nb = _streams_per_step(bsz, mod_row)
    return pl.pallas_call(
        functools.partial(_in_ab_kernel, tl2=tl2, ts=tl2),
        grid=(bsz // nb, n2 // tl2),
        in_specs=[
            pl.BlockSpec((nb, RADIX, tl2, D_MODEL), lambda b, i: (b, 0, i, 0)),
            pl.BlockSpec((1, 1, 3 * D_MODEL), lambda b, i: (row_of(b), 0, 0)),
            pl.BlockSpec((1, D_MODEL), const2),
            pl.BlockSpec((D_MODEL, AB_IN), const2),
            pl.BlockSpec((1, A_W), const2),
            pl.BlockSpec((A_HEADS, CHUNK, CHUNK), const3),
            pl.BlockSpec((A_HEADS, CHUNK, A_HEAD_DIM), const3),
            pl.BlockSpec((B_GROUP_DIM, 2 * B_GROUP_DIM), const2),
            pl.BlockSpec((RADIX, tl2, 128), lambda b, i: (0, i, 0)),
            pl.BlockSpec((RADIX, tl2, 128), lambda b, i: (0, i, 0)),
        ],
        out_specs=[
            pl.BlockSpec((nb, RADIX, tl2, A_W), lambda b, i: (b, 0, i, 0)),
            pl.BlockSpec((nb, RADIX, tl2, B_W), lambda b, i: (b, 0, i, 0)),
            pl.BlockSpec((nb, RADIX, 2, tl2, B_W), lambda b, i: (b, 0, 0, i, 0)),
        ],
        out_shape=[
            jax.ShapeDtypeStruct((bsz, RADIX, n2, A_W), BF16),
            jax.ShapeDtypeStruct((bsz, RADIX, n2, B_W), BF16),
            jax.ShapeDtypeStruct((bsz, RADIX, 2, n2, B_W), BF16),
        ],
        compiler_params=_cparams(("parallel", "arbitrary")),
        name="in_ab",
    )(x4, mod, ng, w_in, vg, sw, sb, cs, twc, tws)


def _dft_out_kernel(g_ref, uv_ref, ya_ref, sgb_ref, x_ref, mod_ref, w_ref, o_ref, f_ref, *, tk):
    nb, rows, _ = x_ref.shape
    r0 = pl.multiple_of(pl.program_id(1) * tk, tk)
    gmat = g_ref[pl.ds(r0, tk), :].astype(BF16)
    for bb in range(nb):
        for k1 in range(RADIX):
            f = jnp.dot(gmat, uv_ref[bb, k1], preferred_element_type=F32)
            for g in range(B_GROUPS):
                f_ref[g, pl.ds(bb * rows + k1, tk, stride=RADIX), :] = f[:, g * B_GROUP_DIM:(g + 1) * B_GROUP_DIM]
    fnat = jnp.concatenate([f_ref[g] for g in range(B_GROUPS)], axis=1)
    yb = (fnat * sgb_ref[...].reshape(nb * rows, B_W).astype(F32)).astype(BF16)
    y = jnp.dot(ya_ref[...].reshape(nb * rows, A_W), w_ref[:A_W], preferred_element_type=F32)
    y = y + jnp.dot(yb, w_ref[A_W:], preferred_element_type=F32)
    gate = mod_ref[0][:, 2 * D_MODEL:]
    o_ref[...] = x_ref[...] + (gate * y).reshape(nb, rows, D_MODEL)


def _dft_out_call(gmat, uv, ya, sgb, xs, mod, mod_row, w_out, tk):
    bsz, seq_len, _ = xs.shape
    n2 = seq_len // RADIX
    rows = RADIX * tk
    row_of = (lambda b: b) if mod_row is None else (lambda b: mod_row)
    nb = _streams_per_step(bsz, mod_row)
    nat = lambda w: pl.BlockSpec((nb, rows, w), lambda b, i: (b, i, 0))
    return pl.pallas_call(
        functools.partial(_dft_out_kernel, tk=tk),
        grid=(bsz // nb, n2 // tk),
        in_specs=[
            pl.BlockSpec((n2, 2 * n2), lambda b, i: (0, 0), pipeline_mode=pl.Buffered(1)),
            pl.BlockSpec((nb, RADIX, 2 * n2, B_W), lambda b, i: (b, 0, 0, 0)),
            nat(A_W), nat(B_W), nat(D_MODEL),
            pl.BlockSpec((1, 1, 3 * D_MODEL), lambda b, i: (row_of(b), 0, 0)),
            pl.BlockSpec((A_W + B_W, D_MODEL), lambda b, i: (0, 0), pipeline_mode=pl.Buffered(1)),
        ],
        out_specs=nat(D_MODEL),
        out_shape=jax.ShapeDtypeStruct((bsz, seq_len, D_MODEL), F32),
        scratch_shapes=[pltpu.VMEM((B_GROUPS, nb * rows, B_GROUP_DIM), F32)],
        compiler_params=_cparams(("parallel", "arbitrary")),
        name="dft_out",
    )(gmat, uv.reshape(bsz, RADIX, 2 * n2, B_W), ya.reshape(bsz, seq_len, A_W),
      sgb.reshape(bsz, seq_len, B_W), xs, mod, w_out)


def _mixer_ab_layer(xs, mod, mod_row, ng, w_in, vg, sw, sb, w_out, cs, tl2, tk):
    seq_len = xs.shape[1]
    twc, tws = _twiddle_tables(seq_len)
    gmat = jnp.asarray(_position_dft_matrix(seq_len))
    ya, sgb, uv = _in_ab_call(xs, mod, mod_row, ng, w_in, vg, sw, sb, cs,
                              jnp.asarray(twc), jnp.asarray(tws), tl2)
    return _dft_out_call(gmat, uv, ya, sgb, xs, mod, mod_row, w_out, tk=tk)


def _rope_block(t, cos, sin_signed, lane_lo):
    nf = C_HEAD_DIM // 4
    swapped = jnp.where(lane_lo, pltpu.roll(t, 128 - nf, axis=1), pltpu.roll(t, nf, axis=1))
    return t * cos + swapped * sin_signed


def _store_dup_heads(k_ref, c, t, lane):
    r = pltpu.roll(t, C_HEAD_DIM, axis=1)
    first = lane < C_HEAD_DIM
    k_ref[0, :, (2 * c) * 128:(2 * c + 1) * 128] = jnp.where(first, t, r).astype(BF16)
    k_ref[0, :, (2 * c + 1) * 128:(2 * c + 2) * 128] = jnp.where(first, r, t).astype(BF16)


def _in_c_kernel(x_ref, mod_ref, ng_ref, w_ref, cos_ref, sin_ref, q_ref, k_ref, v_ref, sg_ref):
    h = _modulated_norm(x_ref[0], mod_ref[0], ng_ref[...])
    z = jnp.dot(h.astype(BF16), w_ref[...], preferred_element_type=F32)
    cos = cos_ref[...]
    sin = sin_ref[...]
    lane = lax.broadcasted_iota(jnp.int32, cos.shape, 1)
    lane_lo = (lane % (C_HEAD_DIM // 2)) < (C_HEAD_DIM // 4)
    qscale = C_HEAD_DIM ** -0.5 * LOG2E
    for c in range(C_Q_W // 128):
        t = _rope_block(z[:, c * 128:(c + 1) * 128], cos, sin, lane_lo)
        q_ref[0, :, c * 128:(c + 1) * 128] = (t * qscale).astype(BF16)
    for c in range(C_KV_W // 128):
        t = _rope_block(z[:, C_Q_W + c * 128:C_Q_W + (c + 1) * 128], cos, sin, lane_lo)
        _store_dup_heads(k_ref, c, t, lane)
    v_ref[0] = z[:, C_Q_W + C_KV_W:C_Q_W + 2 * C_KV_W].T.astype(BF16)
    sg_ref[0] = _silu(z[:, C_Q_W + 2 * C_KV_W:]).astype(BF16)


def _in_c_call(xs, mod, ng, w_in, cos_t, sin_t, tm):
    bsz, seq_len, _ = xs.shape
    const2 = lambda b, i: (0, 0)
    row_blk = lambda w: pl.BlockSpec((1, tm, w), lambda b, i: (b, i, 0))
    return pl.pallas_call(
        _in_c_kernel,
        grid=(bsz, seq_len // tm),
        in_specs=[
            row_blk(D_MODEL),
            pl.BlockSpec((1, 1, 3 * D_MODEL), lambda b, i: (b, 0, 0)),
            pl.BlockSpec((1, D_MODEL), const2),
            pl.BlockSpec((D_MODEL, C_IN), const2),
            pl.BlockSpec((tm, 128), lambda b, i: (i, 0)),
            pl.BlockSpec((tm, 128), lambda b, i: (i, 0)),
        ],
        out_specs=[row_blk(C_Q_W), row_blk(2 * C_KV_W),
                   pl.BlockSpec((1, C_KV_W, tm), lambda b, i: (b, 0, i)), row_blk(C_Q_W)],
        out_shape=[
            jax.ShapeDtypeStruct((bsz, seq_len, C_Q_W), BF16),
            jax.ShapeDtypeStruct((bsz, seq_len, 2 * C_KV_W), BF16),
            jax.ShapeDtypeStruct((bsz, C_KV_W, seq_len), BF16),
            jax.ShapeDtypeStruct((bsz, seq_len, C_Q_W), BF16),
        ],
        compiler_params=_cparams(("parallel", "arbitrary")),
        name="in_c",
    )(xs, mod, ng, w_in, cos_t, sin_t)


def _ctx_kv_kernel(x_ref, mod_ref, ng_ref, w_ref, k_ref, v_ref):
    nb, lc, _ = x_ref.shape
    h = _modulated_norm(x_ref[...].reshape(nb * lc, D_MODEL), mod_ref[0], ng_ref[...])
    z = jnp.dot(h.astype(BF16), w_ref[...], preferred_element_type=F32)
    lane = lax.broadcasted_iota(jnp.int32, (lc, 128), 1)
    for bb in range(nb):
        zb = z[bb * lc:(bb + 1) * lc]
        for c in range(C_KV_W // 128):
            _store_dup_heads(k_ref.at[pl.ds(bb, 1)], c, zb[:, c * 128:(c + 1) * 128], lane)
        v_ref[bb] = zb[:, C_KV_W:].T.astype(BF16)


def _ctx_kv_call(ctx, mod, mod_row, ng, w_kv):
    bsz, lc, _ = ctx.shape
    nb = 4 if bsz % 4 == 0 else 1
    assert C_Q_W % (2 * C_KV_W) == 0
    return pl.pallas_call(
        _ctx_kv_kernel,
        grid=(bsz // nb,),
        in_specs=[
            pl.BlockSpec((nb, lc, D_MODEL), lambda b: (b, 0, 0)),
            pl.BlockSpec((1, 1, 3 * D_MODEL), lambda b: (mod_row, 0, 0)),
            pl.BlockSpec((1, D_MODEL), lambda b: (0, 0)),
            pl.BlockSpec((D_MODEL, 2 * C_KV_W), lambda b: (0, C_Q_W // (2 * C_KV_W))),
        ],
        out_specs=[pl.BlockSpec((nb, lc, 2 * C_KV_W), lambda b: (b, 0, 0)),
                   pl.BlockSpec((nb, C_KV_W, lc), lambda b: (b, 0, 0))],
        out_shape=[jax.ShapeDtypeStruct((bsz, lc, 2 * C_KV_W), BF16),
                   jax.ShapeDtypeStruct((bsz, C_KV_W, lc), BF16)],
        compiler_params=_cparams(("parallel",)),
        name="ctx_kv",
    )(ctx, mod, ng, w_kv)


def _attn_out_kernel(sink_ref, q_ref, kp_ref, kc_ref, kn_ref, vp_ref, vc_ref, vn_ref, kx_ref, vx_ref,
                     sg_ref, x_ref, mod_ref, w_ref, fg_ref, o_ref):
    n = pl.program_id(1)
    last = pl.num_programs(1) - 1
    gq = C_GROUP * Q_BLOCK
    kj = lax.broadcasted_iota(jnp.int32, (Q_BLOCK, Q_BLOCK), 0)
    qi = lax.broadcasted_iota(jnp.int32, (Q_BLOCK, Q_BLOCK), 1)
    bias_before = jnp.where(kj >= qi, 0.0, NEG_INF).astype(F32)
    bias_after = jnp.where(kj <= qi, 0.0, NEG_INF).astype(F32)
    edge_first = jnp.where(n > 0, 0.0, NEG_INF).astype(F32)
    edge_last = jnp.where(n < last, 0.0, NEG_INF).astype(F32)
    lane = lax.broadcasted_iota(jnp.int32, (Q_BLOCK, 128), 1)
    first = lane < C_HEAD_DIM
    zero = jnp.zeros((Q_BLOCK, 128), BF16)
    row_k = lax.broadcasted_iota(jnp.int32, (128, kx_ref.shape[1] + 3 * Q_BLOCK), 0)
    nq = q_ref.shape[1] // Q_BLOCK

    def window(prev_ref, cur_ref, next_ref, j, lanes):
        if j < 0:
            return prev_ref[0, :, lanes]
        if j >= nq:
            return next_ref[0, :, lanes]
        return cur_ref[0, j * Q_BLOCK:(j + 1) * Q_BLOCK, lanes]

    def scores(qb, kh):
        kl = slice(kh * 128, (kh + 1) * 128)
        kblocks = [kx_ref[0, :, kl]] + [window(kp_ref, kc_ref, kn_ref, j, kl) for j in (qb - 1, qb, qb + 1)]
        biases = (None, (bias_before, edge_first if qb == 0 else None), None,
                  (bias_after, edge_last if qb == nq - 1 else None))
        rows_q = slice(qb * Q_BLOCK, (qb + 1) * Q_BLOCK)
        q4 = []
        for c in range(2):
            qv = q_ref[0, rows_q, kh * 256 + c * 128:kh * 256 + (c + 1) * 128]
            q4 += [jnp.where(first, qv, zero), jnp.where(first, zero, qv)]
        q4 = jnp.concatenate(q4, axis=0)
        k2 = jnp.concatenate(kblocks, axis=0)
        st = lax.dot_general(k2, q4, (((1,), (1,)), ((), ())), preferred_element_type=F32)
        blocks, m8, r = [], None, 0
        for kb, bias in zip(kblocks, biases):
            sb = st[r:r + kb.shape[0]]
            r += kb.shape[0]
            if bias is not None:
                mask_bias, edge = bias
                cols = [sb[:, j * Q_BLOCK:(j + 1) * Q_BLOCK] + mask_bias for j in range(C_GROUP)]
                sb = jnp.concatenate(cols, axis=1)
                if edge is not None:
                    sb = sb + edge
            blocks.append(sb)
            mb = jnp.max(sb.reshape(sb.shape[0] // 8, 8, gq), axis=0)
            m8 = mb if m8 is None else jnp.maximum(m8, mb)
        return blocks, m8

    def softmax(kh, blocks, m8):
        sink2 = sink_ref[kh] * LOG2E
        m = jnp.maximum(jnp.max(m8, axis=0, keepdims=True), sink2)
        pt = jnp.concatenate([jnp.exp2(blk - m).astype(BF16) for blk in blocks], axis=0)
        return pt, jnp.exp2(sink2 - m)

    def weighted_values(qb, kh, pt, sink_term):
        vrows = slice((kh // 2) * 128, (kh // 2 + 1) * 128)
        vblocks = [vx_ref[0, vrows, :]]
        for j in (qb - 1, qb, qb + 1):
            if j < 0:
                vblocks.append(vp_ref[0, vrows, :])
            elif j >= nq:
                vblocks.append(vn_ref[0, vrows, :])
            else:
                vblocks.append(vc_ref[0, vrows, j * Q_BLOCK:(j + 1) * Q_BLOCK])
        vpair = jnp.concatenate(vblocks, axis=1)
        own = (row_k < C_HEAD_DIM) if kh % 2 == 0 else (row_k >= C_HEAD_DIM)
        vsum = jnp.where(own, vpair, jnp.ones_like(vpair))
        ot = jnp.dot(vsum, pt, preferred_element_type=F32)
        r0 = (kh % 2) * C_HEAD_DIM
        r1 = C_HEAD_DIM - r0
        denom = ot[r1:r1 + C_HEAD_DIM] + sink_term
        ot = ot[r0:r0 + C_HEAD_DIM] / denom
        cols = []
        for c in range(2):
            pair = jnp.concatenate([ot[:, (2 * c) * Q_BLOCK:(2 * c + 1) * Q_BLOCK],
                                    ot[:, (2 * c + 1) * Q_BLOCK:(2 * c + 2) * Q_BLOCK]], axis=0)
            cols.append(pair.T)
        return cols

    chains = [(qb, kh) for qb in range(nq) for kh in range(C_KV_HEADS)]
    o_cols = {qb: [] for qb in range(nq)}
    gate = mod_ref[0][:, 2 * D_MODEL:]

    def finish_tasks(qb0, qb1):
        rows = slice(qb0 * Q_BLOCK, qb1 * Q_BLOCK)
        state = {}

        def gate_values():
            o_all = jnp.concatenate([jnp.concatenate(o_cols[qb], axis=1) for qb in range(qb0, qb1)], axis=0)
            state["o"] = (o_all * sg_ref[0, rows, :].astype(F32)).astype(BF16)
            state["y"] = []

        def project(c0):
            state["y"].append(jnp.dot(state["o"], w_ref[:, c0:c0 + OUT_PROJ_COLS],
                                      preferred_element_type=F32))

        def residual_norm():
            y = jnp.concatenate(state["y"], axis=1)
            x2 = x_ref[0, rows, :] + gate * y
            ms = jnp.mean(x2 * x2, axis=-1, keepdims=True)
            o_ref[0, rows, :] = x2 * lax.rsqrt(ms + NORM_EPS) * fg_ref[...]

        tasks = [gate_values]
        tasks += [functools.partial(project, c0) for c0 in range(0, D_MODEL, OUT_PROJ_COLS)]
        return tasks + [residual_norm]

    pending = [scores(*ch) for ch in chains[:SCORE_LOOKAHEAD]]
    probs = []
    deferred = []

    def values_for(idx):
        qb, kh = chains[idx]
        o_cols[qb] += weighted_values(qb, kh, *probs.pop(0))
        if kh == C_KV_HEADS - 1 and qb % OUT_PROJ_BLOCKS == OUT_PROJ_BLOCKS - 1:
            deferred.extend(finish_tasks(qb + 1 - OUT_PROJ_BLOCKS, qb + 1))

    for idx, (qb, kh) in enumerate(chains):
        if idx + SCORE_LOOKAHEAD < len(chains):
            pending.append(scores(*chains[idx + SCORE_LOOKAHEAD]))
        probs.append(softmax(kh, *pending.pop(0)))
        if idx >= PV_LAG:
            values_for(idx - PV_LAG)
        if deferred:
            deferred.pop(0)()
    for idx in range(len(chains) - PV_LAG, len(chains)):
        values_for(idx)
    for task in deferred:
        task()


def _attn_out_call(sink, q, k, v, kx, vx, sg, xs, mod, w_out, fg):
    bsz, seq_len, _ = xs.shape
    nblk = seq_len // Q_BLOCK
    nq = ATTN_Q_BLOCKS
    nstep = nblk // nq
    lc = kx.shape[1]
    assert nq % OUT_PROJ_BLOCKS == 0
    cur = lambda w: pl.BlockSpec((1, nq * Q_BLOCK, w), lambda b, n: (b, n, 0))
    prev = lambda w: pl.BlockSpec((1, Q_BLOCK, w), lambda b, n: (b, jnp.maximum(nq * n - 1, 0), 0))
    nxt = lambda w: pl.BlockSpec((1, Q_BLOCK, w), lambda b, n: (b, jnp.minimum(nq * n + nq, nblk - 1), 0))
    ctx_blk = lambda w: pl.BlockSpec((1, lc, w), lambda b, n: (b, 0, 0))
    kw = 2 * C_KV_W
    vcur = pl.BlockSpec((1, C_KV_W, nq * Q_BLOCK), lambda b, n: (b, 0, n))
    vprev = pl.BlockSpec((1, C_KV_W, Q_BLOCK), lambda b, n: (b, 0, jnp.maximum(nq * n - 1, 0)))
    vnxt = pl.BlockSpec((1, C_KV_W, Q_BLOCK), lambda b, n: (b, 0, jnp.minimum(nq * n + nq, nblk - 1)))
    vctx = pl.BlockSpec((1, C_KV_W, lc), lambda b, n: (b, 0, 0))
    sink_t = jnp.repeat(sink.reshape(C_KV_HEADS, 1, C_GROUP), Q_BLOCK, axis=2)
    return pl.pallas_call(
        _attn_out_kernel,
        grid=(bsz, nstep),
        in_specs=[
            pl.BlockSpec((C_KV_HEADS, 1, C_GROUP * Q_BLOCK), lambda b, n: (0, 0, 0)),
            cur(C_Q_W), prev(kw), cur(kw), nxt(kw), vprev, vcur, vnxt, ctx_blk(kw), vctx,
            cur(C_Q_W), cur(D_MODEL),
            pl.BlockSpec((1, 1, 3 * D_MODEL), lambda b, n: (b, 0, 0)),
            pl.BlockSpec((C_Q_W, D_MODEL), lambda b, n: (0, 0)),
            pl.BlockSpec((1, D_MODEL), lambda b, n: (0, 0)),
        ],
        out_specs=cur(D_MODEL),
        out_shape=jax.ShapeDtypeStruct((bsz, seq_len, D_MODEL), F32),
        compiler_params=_cparams(("parallel", "arbitrary")),
        name="attn_out",
    )(sink_t, q, k, k, k, v, v, v, kx, vx, sg, xs, mod, w_out, fg)


def kernel(x, c, ctx, c_ctx, norm_g, ada_w, ada_b, w_in_ab, v_norm_g, spatial_w, spatial_b, w_out_ab,
           w_in_c, sink_logit, w_out_c, final_g):
    bsz, seq_len, _ = x.shape
    depth = ada_w.shape[0]
    assert depth == 2 and bsz + 1 <= MOD_ROWS
    ctx_row = bsz

    cc = jnp.concatenate([c, c_ctx[None, :], jnp.zeros((MOD_ROWS - bsz - 1, D_MODEL), F32)], axis=0)
    mod = _mod_call(cc, ada_w, ada_b)
    mod0 = mod[0].reshape(MOD_ROWS, 1, 3 * D_MODEL)
    mod1 = mod[1].reshape(MOD_ROWS, 1, 3 * D_MODEL)

    cs = jnp.asarray(_channel_dft_matrix())
    ng0 = norm_g[0].reshape(1, D_MODEL)
    ng1 = norm_g[1].reshape(1, D_MODEL)
    w_in0 = w_in_ab[0].astype(BF16)
    w_out0 = w_out_ab[0].astype(BF16)
    vg = v_norm_g[0].reshape(1, A_W)
    sw = spatial_w[0].astype(BF16)
    sb = jnp.broadcast_to(spatial_b[0][:, :, None], (A_HEADS, CHUNK, A_HEAD_DIM))

    x1 = _mixer_ab_layer(x, mod0, None, ng0, w_in0, vg, sw, sb, w_out0, cs, tl2=256, tk=256)
    ctx1 = _mixer_ab_layer(ctx, mod0, ctx_row, ng0, w_in0, vg, sw, sb, w_out0, cs,
                           tl2=ctx.shape[1] // RADIX, tk=ctx.shape[1] // RADIX)

    w_in1 = w_in_c[0].astype(BF16)
    w_out1 = w_out_c[0].astype(BF16)
    cos_t, sin_t = _rope_tables(seq_len)
    q, k, v, sg = _in_c_call(x1, mod1, ng1, w_in1, jnp.asarray(cos_t), jnp.asarray(sin_t), tm=1024)
    kx, vx = _ctx_kv_call(ctx1, mod1, ctx_row, ng1, w_in1)
    return _attn_out_call(sink_logit[0], q, k, v, kx, vx, sg, x1, mod1, w_out1,
                          final_g.reshape(1, D_MODEL))
```

```python
import functools
import math

import numpy as np
import jax
import jax.numpy as jnp
from jax import lax
from jax.experimental import pallas as pl
from jax.experimental.pallas import tpu as pltpu

F32 = jnp.float32
BF16 = jnp.bfloat16

D_MODEL = 1024
GRID_W = 64
CHUNK = 128
A_HEADS = 4
A_HEAD_DIM = 128
A_W = A_HEADS * A_HEAD_DIM
B_GROUPS = 4
B_GROUP_DIM = 128
B_W = B_GROUPS * B_GROUP_DIM
AB_IN = 3 * A_W + 2 * B_W
C_HEADS = 16
C_KV_HEADS = 4
C_GROUP = C_HEADS // C_KV_HEADS
C_HEAD_DIM = 64
C_Q_W = C_HEADS * C_HEAD_DIM
C_KV_W = C_KV_HEADS * C_HEAD_DIM
C_IN = 2 * C_Q_W + 2 * C_KV_W
WINDOW = 128
Q_BLOCK = 128
ROPE_BASE = 10000.0
NORM_EPS = 1e-6
NEG_INF = -1e30
LOG2E = math.log2(math.e)
RADIX = 4
MOD_ROWS = 16
SCORE_LOOKAHEAD = 2
OUT_PROJ_COLS = 256
PV_LAG = 1
ATTN_Q_BLOCKS = 4
OUT_PROJ_BLOCKS = 2
V7X_VMEM_LIMIT = 56 * 1024 * 1024


def _silu(x):
    return x * (0.5 + 0.5 * jnp.tanh(0.5 * x))


def _cparams(sem):
    return pltpu.CompilerParams(dimension_semantics=sem, vmem_limit_bytes=V7X_VMEM_LIMIT)


def _channel_dft_matrix():
    n = np.arange(B_GROUP_DIM)
    ang = 2.0 * np.pi * np.outer(n, n) / B_GROUP_DIM
    return np.concatenate([np.cos(ang), np.sin(ang)], axis=1).astype(np.float32)


def _position_dft_matrix(seq_len):
    n2 = seq_len // RADIX
    idx = np.arange(n2)
    ang = 2.0 * np.pi * (np.outer(idx, idx) % n2) / n2
    norm = 1.0 / math.sqrt(seq_len * B_GROUP_DIM)
    return np.concatenate([np.cos(ang) * norm, -np.sin(ang) * norm], axis=1).astype(np.float32)


def _twiddle_tables(seq_len):
    n2 = seq_len // RADIX
    l2 = np.arange(n2)[None, :, None]
    k1 = np.arange(RADIX)[:, None, None]
    ang = 2.0 * np.pi * ((l2 * k1) % seq_len) / seq_len
    ang = np.broadcast_to(ang, (RADIX, n2, 128))
    return np.cos(ang).astype(np.float32), np.sin(ang).astype(np.float32)


def _rope_tables(seq_len):
    t = np.arange(seq_len)
    row = (t // GRID_W).astype(np.float64)
    col = (t % GRID_W).astype(np.float64)
    lane = np.arange(128)
    dd = lane % C_HEAD_DIM
    nf = C_HEAD_DIM // 4
    inv = ROPE_BASE ** (-(dd % nf).astype(np.float64) / nf)
    pos = np.where((dd < C_HEAD_DIM // 2)[None, :], row[:, None], col[:, None])
    ang = pos * inv[None, :]
    sign = np.where((dd % (2 * nf)) < nf, -1.0, 1.0)[None, :]
    return np.cos(ang).astype(np.float32), (np.sin(ang) * sign).astype(np.float32)


def _mod_kernel(c_ref, w_ref, b_ref, o_ref):
    s = _silu(c_ref[...]).astype(BF16)
    o_ref[0] = jnp.dot(s, w_ref[0].astype(BF16), preferred_element_type=F32) + b_ref[0]


def _mod_call(cc, ada_w, ada_b):
    depth = ada_w.shape[0]
    tn = 1024
    return pl.pallas_call(
        _mod_kernel,
        grid=(depth, 3 * D_MODEL // tn),
        in_specs=[
            pl.BlockSpec((MOD_ROWS, D_MODEL), lambda l, j: (0, 0)),
            pl.BlockSpec((1, D_MODEL, tn), lambda l, j: (l, 0, j)),
            pl.BlockSpec((1, 1, tn), lambda l, j: (l, 0, j)),
        ],
        out_specs=pl.BlockSpec((1, MOD_ROWS, tn), lambda l, j: (l, 0, j)),
        out_shape=jax.ShapeDtypeStruct((depth, MOD_ROWS, 3 * D_MODEL), F32),
        compiler_params=_cparams(("arbitrary", "arbitrary")),
        name="adaln_mod",
    )(cc, ada_w, ada_b.reshape(depth, 1, 3 * D_MODEL))


def _modulated_norm(x, mod_row, g):
    shift = mod_row[:, :D_MODEL]
    scale = mod_row[:, D_MODEL:2 * D_MODEL]
    ms = jnp.mean(x * x, axis=-1, keepdims=True)
    h = x * lax.rsqrt(ms + NORM_EPS) * g
    return h * (1.0 + scale) + shift


def _in_ab_kernel(x_ref, mod_ref, ng_ref, w_ref, vg_ref, sw_ref, sb_ref, cs_ref, twc_ref, tws_ref,
                  ya_ref, sgb_ref, uv_ref, *, tl2, ts):
    rows = RADIX * ts
    piece = min(CHUNK, ts)
    nchunk = rows // CHUNK
    assert nchunk % 2 == 0 and tl2 % ts == 0
    cs = cs_ref[...].astype(BF16)

    nb = x_ref.shape[0]

    def project(o):
        x = x_ref[:, :, o:o + ts, :].reshape(nb * rows, D_MODEL)
        h = _modulated_norm(x, mod_ref[0], ng_ref[...])
        return jnp.dot(h.astype(BF16), w_ref[...], preferred_element_type=F32)

    def mix(bb, o, z):
        v = z[:, A_W:2 * A_W]
        mu = jnp.mean(v, axis=-1, keepdims=True)
        vc = v - mu
        var = jnp.mean(vc * vc, axis=-1, keepdims=True)
        vn = (vc * lax.rsqrt(var + NORM_EPS) * vg_ref[...]).astype(BF16)
        for hd in range(A_HEADS):
            c0 = hd * A_HEAD_DIM
            for cp in range(nchunk // 2):
                ra, rb = 2 * cp * CHUNK, (2 * cp + 1) * CHUNK
                vpair = jnp.concatenate([vn[ra:ra + CHUNK, c0:c0 + A_HEAD_DIM],
                                         vn[rb:rb + CHUNK, c0:c0 + A_HEAD_DIM]], axis=1)
                sv2 = jnp.dot(sw_ref[hd], vpair, preferred_element_type=F32)
                for half, r0 in enumerate((ra, rb)):
                    sv = sv2[:, half * A_HEAD_DIM:(half + 1) * A_HEAD_DIM] + sb_ref[hd]
                    u = z[r0:r0 + CHUNK, c0:c0 + A_HEAD_DIM]
                    ga = z[r0:r0 + CHUNK, 2 * A_W + c0:2 * A_W + c0 + A_HEAD_DIM]
                    ya = (u * sv * _silu(ga)).astype(BF16)
                    for p0 in range(0, CHUNK, piece):
                        l1, off = divmod(r0 + p0, ts)
                        ya_ref[bb, l1, o + off:o + off + piece, c0:c0 + A_HEAD_DIM] = ya[p0:p0 + piece]

        gb = z[:, 3 * A_W + B_W:]
        sgb_ref[bb, :, o:o + ts, :] = _silu(gb).astype(BF16).reshape(RADIX, ts, B_W)

        xb = z[:, 3 * A_W:3 * A_W + B_W].astype(BF16)
        for g in range(B_GROUPS):
            c0 = g * B_GROUP_DIM
            ps, qs = [], []
            for j in range(RADIX):
                pq = jnp.dot(xb[j * ts:(j + 1) * ts, c0:c0 + B_GROUP_DIM], cs,
                             preferred_element_type=F32)
                ps.append(pq[:, :B_GROUP_DIM])
                qs.append(pq[:, B_GROUP_DIM:])
            p02, p13 = ps[0] - ps[2], ps[1] - ps[3]
            q02, q13 = qs[0] - qs[2], qs[1] - qs[3]
            pe, po = ps[0] + ps[2], ps[1] + ps[3]
            qe, qo = qs[0] + qs[2], qs[1] + qs[3]
            us = [pe + po, p02 - q13, pe - po, p02 + q13]
            vs = [qe + qo, q02 + p13, qe - qo, q02 - p13]
            for k1 in range(RADIX):
                if k1 == 0:
                    ut, vt = us[0], vs[0]
                else:
                    tc, tsn = twc_ref[k1, o:o + ts, :], tws_ref[k1, o:o + ts, :]
                    ut = us[k1] * tc - vs[k1] * tsn
                    vt = us[k1] * tsn + vs[k1] * tc
                uv_ref[bb, k1, 0, o:o + ts, c0:c0 + B_GROUP_DIM] = ut.astype(BF16)
                uv_ref[bb, k1, 1, o:o + ts, c0:c0 + B_GROUP_DIM] = vt.astype(BF16)

    for o in range(0, tl2, ts):
        z = project(o)
        for bb in range(nb):
            mix(bb, o, z[bb * rows:(bb + 1) * rows])


def _streams_per_step(bsz, mod_row):
    return 4 if (mod_row is not None and bsz % 4 == 0) else 1


def _in_ab_call(xs, mod, mod_row, ng, w_in, vg, sw, sb, cs, twc, tws, tl2):
    bsz, seq_len, _ = xs.shape
    n2 = seq_len // RADIX
    x4 = xs.reshape(bsz, RADIX, n2, D_MODEL)
    row_of = (lambda b: b) if mod_row is None else (lambda b: mod_row)
    const2 = lambda b, i: (0, 0)
    const3 = lambda b, i: (0, 0, 0)
    nb = _streams_per_step(bsz, mod_row)
    return pl.pallas_call(
        functools.partial(_in_ab_kernel, tl2=tl2, ts=tl2),
        grid=(bsz // nb, n2 // tl2),
        in_specs=[
            pl.BlockSpec((nb, RADIX, tl2, D_MODEL), lambda b, i: (b, 0, i, 0)),
            pl.BlockSpec((1, 1, 3 * D_MODEL), lambda b, i: (row_of(b), 0, 0)),
            pl.BlockSpec((1, D_MODEL), const2),
            pl.BlockSpec((D_MODEL, AB_IN), const2),
            pl.BlockSpec((1, A_W), const2),
            pl.BlockSpec((A_HEADS, CHUNK, CHUNK), const3),
            pl.BlockSpec((A_HEADS, CHUNK, A_HEAD_DIM), const3),
            pl.BlockSpec((B_GROUP_DIM, 2 * B_GROUP_DIM), const2),
            pl.BlockSpec((RADIX, tl2, 128), lambda b, i: (0, i, 0)),
            pl.BlockSpec((RADIX, tl2, 128), lambda b, i: (0, i, 0)),
        ],
        out_specs=[
            pl.BlockSpec((nb, RADIX, tl2, A_W), lambda b, i: (b, 0, i, 0)),
            pl.BlockSpec((nb, RADIX, tl2, B_W), lambda b, i: (b, 0, i, 0)),
            pl.BlockSpec((nb, RADIX, 2, tl2, B_W), lambda b, i: (b, 0, 0, i, 0)),
        ],
        out_shape=[
            jax.ShapeDtypeStruct((bsz, RADIX, n2, A_W), BF16),
            jax.ShapeDtypeStruct((bsz, RADIX, n2, B_W), BF16),
            jax.ShapeDtypeStruct((bsz, RADIX, 2, n2, B_W), BF16),
        ],
        compiler_params=_cparams(("parallel", "arbitrary")),
        name="in_ab",
    )(x4, mod, ng, w_in, vg, sw, sb, cs, twc, tws)


def _dft_out_kernel(g_ref, uv_ref, ya_ref, sgb_ref, x_ref, mod_ref, w_ref, o_ref, f_ref, *, tk):
    nb, rows, _ = x_ref.shape
    r0 = pl.multiple_of(pl.program_id(1) * tk, tk)
    gmat = g_ref[pl.ds(r0, tk), :].astype(BF16)
    for bb in range(nb):
        for k1 in range(RADIX):
            f = jnp.dot(gmat, uv_ref[bb, k1], preferred_element_type=F32)
            for g in range(B_GROUPS):
                f_ref[g, pl.ds(bb * rows + k1, tk, stride=RADIX), :] = f[:, g * B_GROUP_DIM:(g + 1) * B_GROUP_DIM]
    fnat = jnp.concatenate([f_ref[g] for g in range(B_GROUPS)], axis=1)
    yb = (fnat * sgb_ref[...].reshape(nb * rows, B_W).astype(F32)).astype(BF16)
    y = jnp.dot(ya_ref[...].reshape(nb * rows, A_W), w_ref[:A_W], preferred_element_type=F32)
    y = y + jnp.dot(yb, w_ref[A_W:], preferred_element_type=F32)
    gate = mod_ref[0][:, 2 * D_MODEL:]
    o_ref[...] = x_ref[...] + (gate * y).reshape(nb, rows, D_MODEL)


def _dft_out_call(gmat, uv, ya, sgb, xs, mod, mod_row, w_out, tk):
    bsz, seq_len, _ = xs.shape
    n2 = seq_len // RADIX
    rows = RADIX * tk
    row_of = (lambda b: b) if mod_row is None else (lambda b: mod_row)
    nb = _streams_per_step(bsz, mod_row)
    nat = lambda w: pl.BlockSpec((nb, rows, w), lambda b, i: (b, i, 0))
    return pl.pallas_call(
        functools.partial(_dft_out_kernel, tk=tk),
        grid=(bsz // nb, n2 // tk),
        in_specs=[
            pl.BlockSpec((n2, 2 * n2), lambda b, i: (0, 0), pipeline_mode=pl.Buffered(1)),
            pl.BlockSpec((nb, RADIX, 2 * n2, B_W), lambda b, i: (b, 0, 0, 0)),
            nat(A_W), nat(B_W), nat(D_MODEL),
            pl.BlockSpec((1, 1, 3 * D_MODEL), lambda b, i: (row_of(b), 0, 0)),
            pl.BlockSpec((A_W + B_W, D_MODEL), lambda b, i: (0, 0), pipeline_mode=pl.Buffered(1)),
        ],
        out_specs=nat(D_MODEL),
        out_shape=jax.ShapeDtypeStruct((bsz, seq_len, D_MODEL), F32),
        scratch_shapes=[pltpu.VMEM((B_GROUPS, nb * rows, B_GROUP_DIM), F32)],
        compiler_params=_cparams(("parallel", "arbitrary")),
        name="dft_out",
    )(gmat, uv.reshape(bsz, RADIX, 2 * n2, B_W), ya.reshape(bsz, seq_len, A_W),
      sgb.reshape(bsz, seq_len, B_W), xs, mod, w_out)


def _mixer_ab_layer(xs, mod, mod_row, ng, w_in, vg, sw, sb, w_out, cs, tl2, tk):
    seq_len = xs.shape[1]
    twc, tws = _twiddle_tables(seq_len)
    gmat = jnp.asarray(_position_dft_matrix(seq_len))
    ya, sgb, uv = _in_ab_call(xs, mod, mod_row, ng, w_in, vg, sw, sb, cs,
                              jnp.asarray(twc), jnp.asarray(tws), tl2)
    return _dft_out_call(gmat, uv, ya, sgb, xs, mod, mod_row, w_out, tk=tk)


def _rope_block(t, cos, sin_signed, lane_lo):
    nf = C_HEAD_DIM // 4
    swapped = jnp.where(lane_lo, pltpu.roll(t, 128 - nf, axis=1), pltpu.roll(t, nf, axis=1))
    return t * cos + swapped * sin_signed


def _store_dup_heads(k_ref, c, t, lane):
    r = pltpu.roll(t, C_HEAD_DIM, axis=1)
    first = lane < C_HEAD_DIM
    k_ref[0, :, (2 * c) * 128:(2 * c + 1) * 128] = jnp.where(first, t, r).astype(BF16)
    k_ref[0, :, (2 * c + 1) * 128:(2 * c + 2) * 128] = jnp.where(first, r, t).astype(BF16)


def _in_c_kernel(x_ref, mod_ref, ng_ref, w_ref, cos_ref, sin_ref, q_ref, k_ref, v_ref, sg_ref):
    h = _modulated_norm(x_ref[0], mod_ref[0], ng_ref[...])
    z = jnp.dot(h.astype(BF16), w_ref[...], preferred_element_type=F32)
    cos = cos_ref[...]
    sin = sin_ref[...]
    lane = lax.broadcasted_iota(jnp.int32, cos.shape, 1)
    lane_lo = (lane % (C_HEAD_DIM // 2)) < (C_HEAD_DIM // 4)
    qscale = C_HEAD_DIM ** -0.5 * LOG2E
    for c in range(C_Q_W // 128):
        t = _rope_block(z[:, c * 128:(c + 1) * 128], cos, sin, lane_lo)
        q_ref[0, :, c * 128:(c + 1) * 128] = (t * qscale).astype(BF16)
    for c in range(C_KV_W // 128):
        t = _rope_block(z[:, C_Q_W + c * 128:C_Q_W + (c + 1) * 128], cos, sin, lane_lo)
        _store_dup_heads(k_ref, c, t, lane)
    v_ref[0] = z[:, C_Q_W + C_KV_W:C_Q_W + 2 * C_KV_W].T.astype(BF16)
    sg_ref[0] = _silu(z[:, C_Q_W + 2 * C_KV_W:]).astype(BF16)


def _in_c_call(xs, mod, ng, w_in, cos_t, sin_t, tm):
    bsz, seq_len, _ = xs.shape
    const2 = lambda b, i: (0, 0)
    row_blk = lambda w: pl.BlockSpec((1, tm, w), lambda b, i: (b, i, 0))
    return pl.pallas_call(
        _in_c_kernel,
        grid=(bsz, seq_len // tm),
        in_specs=[
            row_blk(D_MODEL),
            pl.BlockSpec((1, 1, 3 * D_MODEL), lambda b, i: (b, 0, 0)),
            pl.BlockSpec((1, D_MODEL), const2),
            pl.BlockSpec((D_MODEL, C_IN), const2),
            pl.BlockSpec((tm, 128), lambda b, i: (i, 0)),
            pl.BlockSpec((tm, 128), lambda b, i: (i, 0)),
        ],
        out_specs=[row_blk(C_Q_W), row_blk(2 * C_KV_W),
                   pl.BlockSpec((1, C_KV_W, tm), lambda b, i: (b, 0, i)), row_blk(C_Q_W)],
        out_shape=[
            jax.ShapeDtypeStruct((bsz, seq_len, C_Q_W), BF16),
            jax.ShapeDtypeStruct((bsz, seq_len, 2 * C_KV_W), BF16),
            jax.ShapeDtypeStruct((bsz, C_KV_W, seq_len), BF16),
            jax.ShapeDtypeStruct((bsz, seq_len, C_Q_W), BF16),
        ],
        compiler_params=_cparams(("parallel", "arbitrary")),
        name="in_c",
    )(xs, mod, ng, w_in, cos_t, sin_t)


def _ctx_kv_kernel(x_ref, mod_ref, ng_ref, w_ref, k_ref, v_ref):
    nb, lc, _ = x_ref.shape
    h = _modulated_norm(x_ref[...].reshape(nb * lc, D_MODEL), mod_ref[0], ng_ref[...])
    z = jnp.dot(h.astype(BF16), w_ref[...], preferred_element_type=F32)
    lane = lax.broadcasted_iota(jnp.int32, (lc, 128), 1)
    for bb in range(nb):
        zb = z[bb * lc:(bb + 1) * lc]
        for c in range(C_KV_W // 128):
            _store_dup_heads(k_ref.at[pl.ds(bb, 1)], c, zb[:, c * 128:(c + 1) * 128], lane)
        v_ref[bb] = zb[:, C_KV_W:].T.astype(BF16)


def _ctx_kv_call(ctx, mod, mod_row, ng, w_kv):
    bsz, lc, _ = ctx.shape
    nb = 4 if bsz % 4 == 0 else 1
    assert C_Q_W % (2 * C_KV_W) == 0
    return pl.pallas_call(
        _ctx_kv_kernel,
        grid=(bsz // nb,),
        in_specs=[
            pl.BlockSpec((nb, lc, D_MODEL), lambda b: (b, 0, 0)),
            pl.BlockSpec((1, 1, 3 * D_MODEL), lambda b: (mod_row, 0, 0)),
            pl.BlockSpec((1, D_MODEL), lambda b: (0, 0)),
            pl.BlockSpec((D_MODEL, 2 * C_KV_W), lambda b: (0, C_Q_W // (2 * C_KV_W))),
        ],
        out_specs=[pl.BlockSpec((nb, lc, 2 * C_KV_W), lambda b: (b, 0, 0)),
                   pl.BlockSpec((nb, C_KV_W, lc), lambda b: (b, 0, 0))],
        out_shape=[jax.ShapeDtypeStruct((bsz, lc, 2 * C_KV_W), BF16),
                   jax.ShapeDtypeStruct((bsz, C_KV_W, lc), BF16)],
        compiler_params=_cparams(("parallel",)),
        name="ctx_kv",
    )(ctx, mod, ng, w_kv)


def _attn_out_kernel(sink_ref, q_ref, kp_ref, kc_ref, kn_ref, vp_ref, vc_ref, vn_ref, kx_ref, vx_ref,
                     sg_ref, x_ref, mod_ref, w_ref, fg_ref, o_ref):
    n = pl.program_id(1)
    last = pl.num_programs(1) - 1
    gq = C_GROUP * Q_BLOCK
    kj = lax.broadcasted_iota(jnp.int32, (Q_BLOCK, Q_BLOCK), 0)
    qi = lax.broadcasted_iota(jnp.int32, (Q_BLOCK, Q_BLOCK), 1)
    bias_before = jnp.where(kj >= qi, 0.0, NEG_INF).astype(F32)
    bias_after = jnp.where(kj <= qi, 0.0, NEG_INF).astype(F32)
    edge_first = jnp.where(n > 0, 0.0, NEG_INF).astype(F32)
    edge_last = jnp.where(n < last, 0.0, NEG_INF).astype(F32)
    lane = lax.broadcasted_iota(jnp.int32, (Q_BLOCK, 128), 1)
    first = lane < C_HEAD_DIM
    zero = jnp.zeros((Q_BLOCK, 128), BF16)
    row_k = lax.broadcasted_iota(jnp.int32, (128, kx_ref.shape[1] + 3 * Q_BLOCK), 0)
    nq = q_ref.shape[1] // Q_BLOCK

    def window(prev_ref, cur_ref, next_ref, j, lanes):
        if j < 0:
            return prev_ref[0, :, lanes]
        if j >= nq:
            return next_ref[0, :, lanes]
        return cur_ref[0, j * Q_BLOCK:(j + 1) * Q_BLOCK, lanes]

    def scores(qb, kh):
        kl = slice(kh * 128, (kh + 1) * 128)
        kblocks = [kx_ref[0, :, kl]] + [window(kp_ref, kc_ref, kn_ref, j, kl) for j in (qb - 1, qb, qb + 1)]
        biases = (None, (bias_before, edge_first if qb == 0 else None), None,
                  (bias_after, edge_last if qb == nq - 1 else None))
        rows_q = slice(qb * Q_BLOCK, (qb + 1) * Q_BLOCK)
        q4 = []
        for c in range(2):
            qv = q_ref[0, rows_q, kh * 256 + c * 128:kh * 256 + (c + 1) * 128]
            q4 += [jnp.where(first, qv, zero), jnp.where(first, zero, qv)]
        q4 = jnp.concatenate(q4, axis=0)
        k2 = jnp.concatenate(kblocks, axis=0)
        st = lax.dot_general(k2, q4, (((1,), (1,)), ((), ())), preferred_element_type=F32)
        blocks, m8, r = [], None, 0
        for kb, bias in zip(kblocks, biases):
            sb = st[r:r + kb.shape[0]]
            r += kb.shape[0]
            if bias is not None:
                mask_bias, edge = bias
                cols = [sb[:, j * Q_BLOCK:(j + 1) * Q_BLOCK] + mask_bias for j in range(C_GROUP)]
                sb = jnp.concatenate(cols, axis=1)
                if edge is not None:
                    sb = sb + edge
            blocks.append(sb)
            mb = jnp.max(sb.reshape(sb.shape[0] // 8, 8, gq), axis=0)
            m8 = mb if m8 is None else jnp.maximum(m8, mb)
        return blocks, m8

    def softmax(kh, blocks, m8):
        sink2 = sink_ref[kh] * LOG2E
        m = jnp.maximum(jnp.max(m8, axis=0, keepdims=True), sink2)
        pt = jnp.concatenate([jnp.exp2(blk - m).astype(BF16) for blk in blocks], axis=0)
        return pt, jnp.exp2(sink2 - m)

    def weighted_values(qb, kh, pt, sink_term):
        vrows = slice((kh // 2) * 128, (kh // 2 + 1) * 128)
        vblocks = [vx_ref[0, vrows, :]]
        for j in (qb - 1, qb, qb + 1):
            if j < 0:
                vblocks.append(vp_ref[0, vrows, :])
            elif j >= nq:
                vblocks.append(vn_ref[0, vrows, :])
            else:
                vblocks.append(vc_ref[0, vrows, j * Q_BLOCK:(j + 1) * Q_BLOCK])
        vpair = jnp.concatenate(vblocks, axis=1)
        own = (row_k < C_HEAD_DIM) if kh % 2 == 0 else (row_k >= C_HEAD_DIM)
        vsum = jnp.where(own, vpair, jnp.ones_like(vpair))
        ot = jnp.dot(vsum, pt, preferred_element_type=F32)
        r0 = (kh % 2) * C_HEAD_DIM
        r1 = C_HEAD_DIM - r0
        denom = ot[r1:r1 + C_HEAD_DIM] + sink_term
        ot = ot[r0:r0 + C_HEAD_DIM] / denom
        cols = []
        for c in range(2):
            pair = jnp.concatenate([ot[:, (2 * c) * Q_BLOCK:(2 * c + 1) * Q_BLOCK],
                                    ot[:, (2 * c + 1) * Q_BLOCK:(2 * c + 2) * Q_BLOCK]], axis=0)
            cols.append(pair.T)
        return cols

    chains = [(qb, kh) for qb in range(nq) for kh in range(C_KV_HEADS)]
    o_cols = {qb: [] for qb in range(nq)}
    gate = mod_ref[0][:, 2 * D_MODEL:]

    def finish_tasks(qb0, qb1):
        rows = slice(qb0 * Q_BLOCK, qb1 * Q_BLOCK)
        state = {}

        def gate_values():
            o_all = jnp.concatenate([jnp.concatenate(o_cols[qb], axis=1) for qb in range(qb0, qb1)], axis=0)
            state["o"] = (o_all * sg_ref[0, rows, :].astype(F32)).astype(BF16)
            state["y"] = []

        def project(c0):
            state["y"].append(jnp.dot(state["o"], w_ref[:, c0:c0 + OUT_PROJ_COLS],
                                      preferred_element_type=F32))

        def residual_norm():
            y = jnp.concatenate(state["y"], axis=1)
            x2 = x_ref[0, rows, :] + gate * y
            ms = jnp.mean(x2 * x2, axis=-1, keepdims=True)
            o_ref[0, rows, :] = x2 * lax.rsqrt(ms + NORM_EPS) * fg_ref[...]

        tasks = [gate_values]
        tasks += [functools.partial(project, c0) for c0 in range(0, D_MODEL, OUT_PROJ_COLS)]
        return tasks + [residual_norm]

    pending = [scores(*ch) for ch in chains[:SCORE_LOOKAHEAD]]
    probs = []
    deferred = []

    def values_for(idx):
        qb, kh = chains[idx]
        o_cols[qb] += weighted_values(qb, kh, *probs.pop(0))
        if kh == C_KV_HEADS - 1 and qb % OUT_PROJ_BLOCKS == OUT_PROJ_BLOCKS - 1:
            deferred.extend(finish_tasks(qb + 1 - OUT_PROJ_BLOCKS, qb + 1))

    for idx, (qb, kh) in enumerate(chains):
        if idx + SCORE_LOOKAHEAD < len(chains):
            pending.append(scores(*chains[idx + SCORE_LOOKAHEAD]))
        probs.append(softmax(kh, *pending.pop(0)))
        if idx >= PV_LAG:
            values_for(idx - PV_LAG)
        if deferred:
            deferred.pop(0)()
    for idx in range(len(chains) - PV_LAG, len(chains)):
        values_for(idx)
    for task in deferred:
        task()


def _attn_out_call(sink, q, k, v, kx, vx, sg, xs, mod, w_out, fg):
    bsz, seq_len, _ = xs.shape
    nblk = seq_len // Q_BLOCK
    nq = ATTN_Q_BLOCKS
    nstep = nblk // nq
    lc = kx.shape[1]
    assert nq % OUT_PROJ_BLOCKS == 0
    cur = lambda w: pl.BlockSpec((1, nq * Q_BLOCK, w), lambda b, n: (b, n, 0))
    prev = lambda w: pl.BlockSpec((1, Q_BLOCK, w), lambda b, n: (b, jnp.maximum(nq * n - 1, 0), 0))
    nxt = lambda w: pl.BlockSpec((1, Q_BLOCK, w), lambda b, n: (b, jnp.minimum(nq * n + nq, nblk - 1), 0))
    ctx_blk = lambda w: pl.BlockSpec((1, lc, w), lambda b, n: (b, 0, 0))
    kw = 2 * C_KV_W
    vcur = pl.BlockSpec((1, C_KV_W, nq * Q_BLOCK), lambda b, n: (b, 0, n))
    vprev = pl.BlockSpec((1, C_KV_W, Q_BLOCK), lambda b, n: (b, 0, jnp.maximum(nq * n - 1, 0)))
    vnxt = pl.BlockSpec((1, C_KV_W, Q_BLOCK), lambda b, n: (b, 0, jnp.minimum(nq * n + nq, nblk - 1)))
    vctx = pl.BlockSpec((1, C_KV_W, lc), lambda b, n: (b, 0, 0))
    sink_t = jnp.repeat(sink.reshape(C_KV_HEADS, 1, C_GROUP), Q_BLOCK, axis=2)
    return pl.pallas_call(
        _attn_out_kernel,
        grid=(bsz, nstep),
        in_specs=[
            pl.BlockSpec((C_KV_HEADS, 1, C_GROUP * Q_BLOCK), lambda b, n: (0, 0, 0)),
            cur(C_Q_W), prev(kw), cur(kw), nxt(kw), vprev, vcur, vnxt, ctx_blk(kw), vctx,
            cur(C_Q_W), cur(D_MODEL),
            pl.BlockSpec((1, 1, 3 * D_MODEL), lambda b, n: (b, 0, 0)),
            pl.BlockSpec((C_Q_W, D_MODEL), lambda b, n: (0, 0)),
            pl.BlockSpec((1, D_MODEL), lambda b, n: (0, 0)),
        ],
        out_specs=cur(D_MODEL),
        out_shape=jax.ShapeDtypeStruct((bsz, seq_len, D_MODEL), F32),
        compiler_params=_cparams(("parallel", "arbitrary")),
        name="attn_out",
    )(sink_t, q, k, k, k, v, v, v, kx, vx, sg, xs, mod, w_out, fg)


def kernel(x, c, ctx, c_ctx, norm_g, ada_w, ada_b, w_in_ab, v_norm_g, spatial_w, spatial_b, w_out_ab,
           w_in_c, sink_logit, w_out_c, final_g):
    bsz, seq_len, _ = x.shape
    depth = ada_w.shape[0]
    assert depth == 2 and bsz + 1 <= MOD_ROWS
    ctx_row = bsz

    cc = jnp.concatenate([c, c_ctx[None, :], jnp.zeros((MOD_ROWS - bsz - 1, D_MODEL), F32)], axis=0)
    mod = _mod_call(cc, ada_w, ada_b)
    mod0 = mod[0].reshape(MOD_ROWS, 1, 3 * D_MODEL)
    mod1 = mod[1].reshape(MOD_ROWS, 1, 3 * D_MODEL)

    cs = jnp.asarray(_channel_dft_matrix())
    ng0 = norm_g[0].reshape(1, D_MODEL)
    ng1 = norm_g[1].reshape(1, D_MODEL)
    w_in0 = w_in_ab[0].astype(BF16)
    w_out0 = w_out_ab[0].astype(BF16)
    vg = v_norm_g[0].reshape(1, A_W)
    sw = spatial_w[0].astype(BF16)
    sb = jnp.broadcast_to(spatial_b[0][:, :, None], (A_HEADS, CHUNK, A_HEAD_DIM))

    x1 = _mixer_ab_layer(x, mod0, None, ng0, w_in0, vg, sw, sb, w_out0, cs, tl2=256, tk=256)
    ctx1 = _mixer_ab_layer(ctx, mod0, ctx_row, ng0, w_in0, vg, sw, sb, w_out0, cs,
                           tl2=ctx.shape[1] // RADIX, tk=ctx.shape[1] // RADIX)

    w_in1 = w_in_c[0].astype(BF16)
    w_out1 = w_out_c[0].astype(BF16)
    cos_t, sin_t = _rope_tables(seq_len)
    q, k, v, sg = _in_c_call(x1, mod1, ng1, w_in1, jnp.asarray(cos_t), jnp.asarray(sin_t), tm=1024)
    kx, vx = _ctx_kv_call(ctx1, mod1, ctx_row, ng1, w_in1)
    return _attn_out_call(sink_logit[0], q, k, v, kx, vx, sg, x1, mod1, w_out1,
                          final_g.reshape(1, D_MODEL))
```

```python
import functools
import math

import numpy as np
import jax
import jax.numpy as jnp
from jax import lax
from jax.experimental import pallas as pl
from jax.experimental.pallas import tpu as pltpu

F32 = jnp.float32
BF16 = jnp.bfloat16

D_MODEL = 1024
GRID_W = 64
CHUNK = 128
A_HEADS = 4
A_HEAD_DIM = 128
A_W = A_HEADS * A_HEAD_DIM
B_GROUPS = 4
B_GROUP_DIM = 128
B_W = B_GROUPS * B_GROUP_DIM
AB_IN = 3 * A_W + 2 * B_W
C_HEADS = 16
C_KV_HEADS = 4
C_GROUP = C_HEADS // C_KV_HEADS
C_HEAD_DIM = 64
C_Q_W = C_HEADS * C_HEAD_DIM
C_KV_W = C_KV_HEADS * C_HEAD_DIM
C_IN = 2 * C_Q_W + 2 * C_KV_W
WINDOW = 128
Q_BLOCK = 128
ROPE_BASE = 10000.0
NORM_EPS = 1e-6
NEG_INF = -1e30
LOG2E = math.log2(math.e)
RADIX = 4
MOD_ROWS = 16
SCORE_LOOKAHEAD = 2
OUT_PROJ_COLS = 256
PV_LAG = 1
ATTN_Q_BLOCKS = 4
OUT_PROJ_BLOCKS = 2
V7X_VMEM_LIMIT = 56 * 1024 * 1024


def _silu(x):
    return x * (0.5 + 0.5 * jnp.tanh(0.5 * x))


def _cparams(sem, n_operands=None, weight_operand=None):
    fusion = None
    if weight_operand is not None:
        fusion = [i == weight_operand for i in range(n_operands)]
    return pltpu.CompilerParams(dimension_semantics=sem, vmem_limit_bytes=V7X_VMEM_LIMIT,
                                allow_input_fusion=fusion)


def _channel_dft_matrix():
    n = np.arange(B_GROUP_DIM)
    ang = 2.0 * np.pi * np.outer(n, n) / B_GROUP_DIM
    return np.concatenate([np.cos(ang), np.sin(ang)], axis=1).astype(np.float32)


def _position_dft_matrix(seq_len):
    n2 = seq_len // RADIX
    idx = np.arange(n2)
    ang = 2.0 * np.pi * (np.outer(idx, idx) % n2) / n2
    norm = 1.0 / math.sqrt(seq_len * B_GROUP_DIM)
    return np.concatenate([np.cos(ang) * norm, -np.sin(ang) * norm], axis=1).astype(np.float32)


def _twiddle_tables(seq_len):
    n2 = seq_len // RADIX
    l2 = np.arange(n2)[None, :, None]
    k1 = np.arange(RADIX)[:, None, None]
    ang = 2.0 * np.pi * ((l2 * k1) % seq_len) / seq_len
    ang = np.broadcast_to(ang, (RADIX, n2, 128))
    return np.cos(ang).astype(np.float32), np.sin(ang).astype(np.float32)


def _rope_tables(seq_len):
    t = np.arange(seq_len)
    row = (t // GRID_W).astype(np.float64)
    col = (t % GRID_W).astype(np.float64)
    lane = np.arange(128)
    dd = lane % C_HEAD_DIM
    nf = C_HEAD_DIM // 4
    inv = ROPE_BASE ** (-(dd % nf).astype(np.float64) / nf)
    pos = np.where((dd < C_HEAD_DIM // 2)[None, :], row[:, None], col[:, None])
    ang = pos * inv[None, :]
    sign = np.where((dd % (2 * nf)) < nf, -1.0, 1.0)[None, :]
    return np.cos(ang).astype(np.float32), (np.sin(ang) * sign).astype(np.float32)


def _mod_kernel(c_ref, w_ref, b_ref, o_ref):
    s = _silu(c_ref[...]).astype(BF16)
    o_ref[0] = jnp.dot(s, w_ref[0].astype(BF16), preferred_element_type=F32) + b_ref[0]


def _mod_call(cc, ada_w, ada_b):
    depth = ada_w.shape[0]
    tn = 1024
    return pl.pallas_call(
        _mod_kernel,
        grid=(depth, 3 * D_MODEL // tn),
        in_specs=[
            pl.BlockSpec((MOD_ROWS, D_MODEL), lambda l, j: (0, 0)),
            pl.BlockSpec((1, D_MODEL, tn), lambda l, j: (l, 0, j)),
            pl.BlockSpec((1, 1, tn), lambda l, j: (l, 0, j)),
        ],
        out_specs=pl.BlockSpec((1, MOD_ROWS, tn), lambda l, j: (l, 0, j)),
        out_shape=jax.ShapeDtypeStruct((depth, MOD_ROWS, 3 * D_MODEL), F32),
        compiler_params=_cparams(("arbitrary", "arbitrary")),
        name="adaln_mod",
    )(cc, ada_w, ada_b.reshape(depth, 1, 3 * D_MODEL))


def _modulated_norm(x, mod_row, g):
    shift = mod_row[:, :D_MODEL]
    scale = mod_row[:, D_MODEL:2 * D_MODEL]
    ms = jnp.mean(x * x, axis=-1, keepdims=True)
    h = x * lax.rsqrt(ms + NORM_EPS) * g
    return h * (1.0 + scale) + shift


def _in_ab_kernel(x_ref, mod_ref, ng_ref, w_ref, vg_ref, sw_ref, sb_ref, cs_ref, twc_ref, tws_ref,
                  ya_ref, sgb_ref, uv_ref, *, tl2, ts):
    rows = RADIX * ts
    piece = min(CHUNK, ts)
    nchunk = rows // CHUNK
    assert nchunk % 2 == 0 and tl2 % ts == 0
    cs = cs_ref[...].astype(BF16)

    nb = x_ref.shape[0]

    def project(o):
        x = x_ref[:, :, o:o + ts, :].reshape(nb * rows, D_MODEL)
        h = _modulated_norm(x, mod_ref[0], ng_ref[...])
        return jnp.dot(h.astype(BF16), w_ref[...], preferred_element_type=F32)

    def mix(bb, o, z):
        v = z[:, A_W:2 * A_W]
        mu = jnp.mean(v, axis=-1, keepdims=True)
        vc = v - mu
        var = jnp.mean(vc * vc, axis=-1, keepdims=True)
        vn = (vc * lax.rsqrt(var + NORM_EPS) * vg_ref[...]).astype(BF16)
        for hd in range(A_HEADS):
            c0 = hd * A_HEAD_DIM
            for cp in range(nchunk // 2):
                ra, rb = 2 * cp * CHUNK, (2 * cp + 1) * CHUNK
                vpair = jnp.concatenate([vn[ra:ra + CHUNK, c0:c0 + A_HEAD_DIM],
                                         vn[rb:rb + CHUNK, c0:c0 + A_HEAD_DIM]], axis=1)
                sv2 = jnp.dot(sw_ref[hd], vpair, preferred_element_type=F32)
                for half, r0 in enumerate((ra, rb)):
                    sv = sv2[:, half * A_HEAD_DIM:(half + 1) * A_HEAD_DIM] + sb_ref[hd]
                    u = z[r0:r0 + CHUNK, c0:c0 + A_HEAD_DIM]
                    ga = z[r0:r0 + CHUNK, 2 * A_W + c0:2 * A_W + c0 + A_HEAD_DIM]
                    ya = (u * sv * _silu(ga)).astype(BF16)
                    for p0 in range(0, CHUNK, piece):
                        l1, off = divmod(r0 + p0, ts)
                        ya_ref[bb, l1, o + off:o + off + piece, c0:c0 + A_HEAD_DIM] = ya[p0:p0 + piece]

        gb = z[:, 3 * A_W + B_W:]
        sgb_ref[bb, :, o:o + ts, :] = _silu(gb).astype(BF16).reshape(RADIX, ts, B_W)

        xb = z[:, 3 * A_W:3 * A_W + B_W].astype(BF16)
        for g in range(B_GROUPS):
            c0 = g * B_GROUP_DIM
            ps, qs = [], []
            for j in range(RADIX):
                pq = jnp.dot(xb[j * ts:(j + 1) * ts, c0:c0 + B_GROUP_DIM], cs,
                             preferred_element_type=F32)
                ps.append(pq[:, :B_GROUP_DIM])
                qs.append(pq[:, B_GROUP_DIM:])
            p02, p13 = ps[0] - ps[2], ps[1] - ps[3]
            q02, q13 = qs[0] - qs[2], qs[1] - qs[3]
            pe, po = ps[0] + ps[2], ps[1] + ps[3]
            qe, qo = qs[0] + qs[2], qs[1] + qs[3]
            us = [pe + po, p02 - q13, pe - po, p02 + q13]
            vs = [qe + qo, q02 + p13, qe - qo, q02 - p13]
            for k1 in range(RADIX):
                if k1 == 0:
                    ut, vt = us[0], vs[0]
                else:
                    tc, tsn = twc_ref[k1, o:o + ts, :], tws_ref[k1, o:o + ts, :]
                    ut = us[k1] * tc - vs[k1] * tsn
                    vt = us[k1] * tsn + vs[k1] * tc
                uv_ref[bb, k1, 0, o:o + ts, c0:c0 + B_GROUP_DIM] = ut.astype(BF16)
                uv_ref[bb, k1, 1, o:o + ts, c0:c0 + B_GROUP_DIM] = vt.astype(BF16)

    for o in range(0, tl2, ts):
        z = project(o)
        for bb in range(nb):
            mix(bb, o, z[bb * rows:(bb + 1) * rows])


def _streams_per_step(bsz, mod_row):
    return 4 if (mod_row is not None and bsz % 4 == 0) else 1


def _in_ab_call(xs, mod, mod_row, ng, w_in, vg, sw, sb, cs, twc, tws, tl2):
    bsz, seq_len, _ = xs.shape
    n2 = seq_len // RADIX
    x4 = xs.reshape(bsz, RADIX, n2, D_MODEL)
    row_of = (lambda b: b) if mod_row is None else (lambda b: mod_row)
    const2 = lambda b, i: (0, 0)
    const3 = lambda b, i: (0, 0, 0)
    nb = _streams_per_step(bsz, mod_row)
    return pl.pallas_call(
        functools.partial(_in_ab_kernel, tl2=tl2, ts=tl2),
        grid=(bsz // nb, n2 // tl2),
        in_specs=[
            pl.BlockSpec((nb, RADIX, tl2, D_MODEL), lambda b, i: (b, 0, i, 0)),
            pl.BlockSpec((1, 1, 3 * D_MODEL), lambda b, i: (row_of(b), 0, 0)),
            pl.BlockSpec((1, D_MODEL), const2),
            pl.BlockSpec((D_MODEL, AB_IN), const2),
            pl.BlockSpec((1, A_W), const2),
            pl.BlockSpec((A_HEADS, CHUNK, CHUNK), const3),
            pl.BlockSpec((A_HEADS, CHUNK, A_HEAD_DIM), const3),
            pl.BlockSpec((B_GROUP_DIM, 2 * B_GROUP_DIM), const2),
            pl.BlockSpec((RADIX, tl2, 128), lambda b, i: (0, i, 0)),
            pl.BlockSpec((RADIX, tl2, 128), lambda b, i: (0, i, 0)),
        ],
        out_specs=[
            pl.BlockSpec((nb, RADIX, tl2, A_W), lambda b, i: (b, 0, i, 0)),
            pl.BlockSpec((nb, RADIX, tl2, B_W), lambda b, i: (b, 0, i, 0)),
            pl.BlockSpec((nb, RADIX, 2, tl2, B_W), lambda b, i: (b, 0, 0, i, 0)),
        ],
        out_shape=[
            jax.ShapeDtypeStruct((bsz, RADIX, n2, A_W), BF16),
            jax.ShapeDtypeStruct((bsz, RADIX, n2, B_W), BF16),
            jax.ShapeDtypeStruct((bsz, RADIX, 2, n2, B_W), BF16),
        ],
        compiler_params=_cparams(("parallel", "arbitrary"), 10, 3),
        name="in_ab",
    )(x4, mod, ng, w_in, vg, sw, sb, cs, twc, tws)


def _dft_out_kernel(g_ref, uv_ref, ya_ref, sgb_ref, x_ref, mod_ref, w_ref, o_ref, f_ref, *, tk):
    nb, rows, _ = x_ref.shape
    r0 = pl.multiple_of(pl.program_id(1) * tk, tk)
    gmat = g_ref[pl.ds(r0, tk), :].astype(BF16)
    for bb in range(nb):
        for k1 in range(RADIX):
            f = jnp.dot(gmat, uv_ref[bb, k1], preferred_element_type=F32)
            for g in range(B_GROUPS):
                f_ref[g, pl.ds(bb * rows + k1, tk, stride=RADIX), :] = f[:, g * B_GROUP_DIM:(g + 1) * B_GROUP_DIM]
    fnat = jnp.concatenate([f_ref[g] for g in range(B_GROUPS)], axis=1)
    yb = (fnat * sgb_ref[...].reshape(nb * rows, B_W).astype(F32)).astype(BF16)
    y = jnp.dot(ya_ref[...].reshape(nb * rows, A_W), w_ref[:A_W], preferred_element_type=F32)
    y = y + jnp.dot(yb, w_ref[A_W:], preferred_element_type=F32)
    gate = mod_ref[0][:, 2 * D_MODEL:]
    o_ref[...] = x_ref[...] + (gate * y).reshape(nb, rows, D_MODEL)


def _dft_out_call(gmat, uv, ya, sgb, xs, mod, mod_row, w_out, tk):
    bsz, seq_len, _ = xs.shape
    n2 = seq_len // RADIX
    rows = RADIX * tk
    row_of = (lambda b: b) if mod_row is None else (lambda b: mod_row)
    nb = _streams_per_step(bsz, mod_row)
    nat = lambda w: pl.BlockSpec((nb, rows, w), lambda b, i: (b, i, 0))
    return pl.pallas_call(
        functools.partial(_dft_out_kernel, tk=tk),
        grid=(bsz // nb, n2 // tk),
        in_specs=[
            pl.BlockSpec((n2, 2 * n2), lambda b, i: (0, 0), pipeline_mode=pl.Buffered(1)),
            pl.BlockSpec((nb, RADIX, 2 * n2, B_W), lambda b, i: (b, 0, 0, 0)),
            nat(A_W), nat(B_W), nat(D_MODEL),
            pl.BlockSpec((1, 1, 3 * D_MODEL), lambda b, i: (row_of(b), 0, 0)),
            pl.BlockSpec((A_W + B_W, D_MODEL), lambda b, i: (0, 0), pipeline_mode=pl.Buffered(1)),
        ],
        out_specs=nat(D_MODEL),
        out_shape=jax.ShapeDtypeStruct((bsz, seq_len, D_MODEL), F32),
        scratch_shapes=[pltpu.VMEM((B_GROUPS, nb * rows, B_GROUP_DIM), F32)],
        compiler_params=_cparams(("parallel", "arbitrary"), 7, 6),
        name="dft_out",
    )(gmat, uv.reshape(bsz, RADIX, 2 * n2, B_W), ya.reshape(bsz, seq_len, A_W),
      sgb.reshape(bsz, seq_len, B_W), xs, mod, w_out)


def _mixer_ab_layer(xs, mod, mod_row, ng, w_in, vg, sw, sb, w_out, cs, tl2, tk):
    seq_len = xs.shape[1]
    twc, tws = _twiddle_tables(seq_len)
    gmat = jnp.asarray(_position_dft_matrix(seq_len))
    ya, sgb, uv = _in_ab_call(xs, mod, mod_row, ng, w_in, vg, sw, sb, cs,
                              jnp.asarray(twc), jnp.asarray(tws), tl2)
    return _dft_out_call(gmat, uv, ya, sgb, xs, mod, mod_row, w_out, tk=tk)


def _rope_block(t, cos, sin_signed, lane_lo):
    nf = C_HEAD_DIM // 4
    swapped = jnp.where(lane_lo, pltpu.roll(t, 128 - nf, axis=1), pltpu.roll(t, nf, axis=1))
    return t * cos + swapped * sin_signed


def _store_dup_heads(k_ref, c, t, lane):
    r = pltpu.roll(t, C_HEAD_DIM, axis=1)
    first = lane < C_HEAD_DIM
    k_ref[0, :, (2 * c) * 128:(2 * c + 1) * 128] = jnp.where(first, t, r).astype(BF16)
    k_ref[0, :, (2 * c + 1) * 128:(2 * c + 2) * 128] = jnp.where(first, r, t).astype(BF16)


def _in_c_kernel(x_ref, mod_ref, ng_ref, w_ref, cos_ref, sin_ref, q_ref, k_ref, v_ref, sg_ref):
    h = _modulated_norm(x_ref[0], mod_ref[0], ng_ref[...])
    z = jnp.dot(h.astype(BF16), w_ref[...], preferred_element_type=F32)
    cos = cos_ref[...]
    sin = sin_ref[...]
    lane = lax.broadcasted_iota(jnp.int32, cos.shape, 1)
    lane_lo = (lane % (C_HEAD_DIM // 2)) < (C_HEAD_DIM // 4)
    qscale = C_HEAD_DIM ** -0.5 * LOG2E
    for c in range(C_Q_W // 128):
        t = _rope_block(z[:, c * 128:(c + 1) * 128], cos, sin, lane_lo)
        q_ref[0, :, c * 128:(c + 1) * 128] = (t * qscale).astype(BF16)
    for c in range(C_KV_W // 128):
        t = _rope_block(z[:, C_Q_W + c * 128:C_Q_W + (c + 1) * 128], cos, sin, lane_lo)
        _store_dup_heads(k_ref, c, t, lane)
    v_ref[0] = z[:, C_Q_W + C_KV_W:C_Q_W + 2 * C_KV_W].T.astype(BF16)
    sg_ref[0] = _silu(z[:, C_Q_W + 2 * C_KV_W:]).astype(BF16)


def _in_c_call(xs, mod, ng, w_in, cos_t, sin_t, tm):
    bsz, seq_len, _ = xs.shape
    const2 = lambda b, i: (0, 0)
    row_blk = lambda w: pl.BlockSpec((1, tm, w), lambda b, i: (b, i, 0))
    return pl.pallas_call(
        _in_c_kernel,
        grid=(bsz, seq_len // tm),
        in_specs=[
            row_blk(D_MODEL),
            pl.BlockSpec((1, 1, 3 * D_MODEL), lambda b, i: (b, 0, 0)),
            pl.BlockSpec((1, D_MODEL), const2),
            pl.BlockSpec((D_MODEL, C_IN), const2),
            pl.BlockSpec((tm, 128), lambda b, i: (i, 0)),
            pl.BlockSpec((tm, 128), lambda b, i: (i, 0)),
        ],
        out_specs=[row_blk(C_Q_W), row_blk(2 * C_KV_W),
                   pl.BlockSpec((1, C_KV_W, tm), lambda b, i: (b, 0, i)), row_blk(C_Q_W)],
        out_shape=[
            jax.ShapeDtypeStruct((bsz, seq_len, C_Q_W), BF16),
            jax.ShapeDtypeStruct((bsz, seq_len, 2 * C_KV_W), BF16),
            jax.ShapeDtypeStruct((bsz, C_KV_W, seq_len), BF16),
            jax.ShapeDtypeStruct((bsz, seq_len, C_Q_W), BF16),
        ],
        compiler_params=_cparams(("parallel", "arbitrary"), 6, 3),
        name="in_c",
    )(xs, mod, ng, w_in, cos_t, sin_t)


def _ctx_kv_kernel(x_ref, mod_ref, ng_ref, w_ref, k_ref, v_ref):
    nb, lc, _ = x_ref.shape
    h = _modulated_norm(x_ref[...].reshape(nb * lc, D_MODEL), mod_ref[0], ng_ref[...])
    z = jnp.dot(h.astype(BF16), w_ref[...], preferred_element_type=F32)
    lane = lax.broadcasted_iota(jnp.int32, (lc, 128), 1)
    for bb in range(nb):
        zb = z[bb * lc:(bb + 1) * lc]
        for c in range(C_KV_W // 128):
            _store_dup_heads(k_ref.at[pl.ds(bb, 1)], c, zb[:, c * 128:(c + 1) * 128], lane)
        v_ref[bb] = zb[:, C_KV_W:].T.astype(BF16)


def _ctx_kv_call(ctx, mod, mod_row, ng, w_kv):
    bsz, lc, _ = ctx.shape
    nb = 4 if bsz % 4 == 0 else 1
    assert C_Q_W % (2 * C_KV_W) == 0
    return pl.pallas_call(
        _ctx_kv_kernel,
        grid=(bsz // nb,),
        in_specs=[
            pl.BlockSpec((nb, lc, D_MODEL), lambda b: (b, 0, 0)),
            pl.BlockSpec((1, 1, 3 * D_MODEL), lambda b: (mod_row, 0, 0)),
            pl.BlockSpec((1, D_MODEL), lambda b: (0, 0)),
            pl.BlockSpec((D_MODEL, 2 * C_KV_W), lambda b: (0, C_Q_W // (2 * C_KV_W))),
        ],
        out_specs=[pl.BlockSpec((nb, lc, 2 * C_KV_W), lambda b: (b, 0, 0)),
                   pl.BlockSpec((nb, C_KV_W, lc), lambda b: (b, 0, 0))],
        out_shape=[jax.ShapeDtypeStruct((bsz, lc, 2 * C_KV_W), BF16),
                   jax.ShapeDtypeStruct((bsz, C_KV_W, lc), BF16)],
        compiler_params=_cparams(("parallel",), 4, 3),
        name="ctx_kv",
    )(ctx, mod, ng, w_kv)


def _attn_out_kernel(sink_ref, q_ref, kp_ref, kc_ref, kn_ref, vp_ref, vc_ref, vn_ref, kx_ref, vx_ref,
                     sg_ref, x_ref, mod_ref, w_ref, fg_ref, o_ref):
    n = pl.program_id(1)
    last = pl.num_programs(1) - 1
    gq = C_GROUP * Q_BLOCK
    kj = lax.broadcasted_iota(jnp.int32, (Q_BLOCK, Q_BLOCK), 0)
    qi = lax.broadcasted_iota(jnp.int32, (Q_BLOCK, Q_BLOCK), 1)
    bias_before = jnp.where(kj >= qi, 0.0, NEG_INF).astype(F32)
    bias_after = jnp.where(kj <= qi, 0.0, NEG_INF).astype(F32)
    edge_first = jnp.where(n > 0, 0.0, NEG_INF).astype(F32)
    edge_last = jnp.where(n < last, 0.0, NEG_INF).astype(F32)
    lane = lax.broadcasted_iota(jnp.int32, (Q_BLOCK, 128), 1)
    first = lane < C_HEAD_DIM
    zero = jnp.zeros((Q_BLOCK, 128), BF16)
    row_k = lax.broadcasted_iota(jnp.int32, (128, kx_ref.shape[1] + 3 * Q_BLOCK), 0)
    nq = q_ref.shape[1] // Q_BLOCK

    def window(prev_ref, cur_ref, next_ref, j, lanes):
        if j < 0:
            return prev_ref[0, :, lanes]
        if j >= nq:
            return next_ref[0, :, lanes]
        return cur_ref[0, j * Q_BLOCK:(j + 1) * Q_BLOCK, lanes]

    def scores(qb, kh):
        kl = slice(kh * 128, (kh + 1) * 128)
        kblocks = [kx_ref[0, :, kl]] + [window(kp_ref, kc_ref, kn_ref, j, kl) for j in (qb - 1, qb, qb + 1)]
        biases = (None, (bias_before, edge_first if qb == 0 else None), None,
                  (bias_after, edge_last if qb == nq - 1 else None))
        rows_q = slice(qb * Q_BLOCK, (qb + 1) * Q_BLOCK)
        q4 = []
        for c in range(2):
            qv = q_ref[0, rows_q, kh * 256 + c * 128:kh * 256 + (c + 1) * 128]
            q4 += [jnp.where(first, qv, zero), jnp.where(first, zero, qv)]
        q4 = jnp.concatenate(q4, axis=0)
        k2 = jnp.concatenate(kblocks, axis=0)
        st = lax.dot_general(k2, q4, (((1,), (1,)), ((), ())), preferred_element_type=F32)
        blocks, m8, r = [], None, 0
        for kb, bias in zip(kblocks, biases):
            sb = st[r:r + kb.shape[0]]
            r += kb.shape[0]
            if bias is not None:
                mask_bias, edge = bias
                cols = [sb[:, j * Q_BLOCK:(j + 1) * Q_BLOCK] + mask_bias for j in range(C_GROUP)]
                sb = jnp.concatenate(cols, axis=1)
                if edge is not None:
                    sb = sb + edge
            blocks.append(sb)
            mb = jnp.max(sb.reshape(sb.shape[0] // 8, 8, gq), axis=0)
            m8 = mb if m8 is None else jnp.maximum(m8, mb)
        return blocks, m8

    def softmax(kh, blocks, m8):
        sink2 = sink_ref[kh] * LOG2E
        m = jnp.maximum(jnp.max(m8, axis=0, keepdims=True), sink2)
        pt = jnp.concatenate([jnp.exp2(blk - m).astype(BF16) for blk in blocks], axis=0)
        return pt, jnp.exp2(sink2 - m)

    def weighted_values(qb, kh, pt, sink_term):
        vrows = slice((kh // 2) * 128, (kh // 2 + 1) * 128)
        vblocks = [vx_ref[0, vrows, :]]
        for j in (qb - 1, qb, qb + 1):
            if j < 0:
                vblocks.append(vp_ref[0, vrows, :])
            elif j >= nq:
                vblocks.append(vn_ref[0, vrows, :])
            else:
                vblocks.append(vc_ref[0, vrows, j * Q_BLOCK:(j + 1) * Q_BLOCK])
        vpair = jnp.concatenate(vblocks, axis=1)
        own = (row_k < C_HEAD_DIM) if kh % 2 == 0 else (row_k >= C_HEAD_DIM)
        vsum = jnp.where(own, vpair, jnp.ones_like(vpair))
        ot = jnp.dot(vsum, pt, preferred_element_type=F32)
        r0 = (kh % 2) * C_HEAD_DIM
        r1 = C_HEAD_DIM - r0
        denom = ot[r1:r1 + C_HEAD_DIM] + sink_term
        ot = ot[r0:r0 + C_HEAD_DIM] / denom
        cols = []
        for c in range(2):
            pair = jnp.concatenate([ot[:, (2 * c) * Q_BLOCK:(2 * c + 1) * Q_BLOCK],
                                    ot[:, (2 * c + 1) * Q_BLOCK:(2 * c + 2) * Q_BLOCK]], axis=0)
            cols.append(pair.T)
        return cols

    chains = [(qb, kh) for qb in range(nq) for kh in range(C_KV_HEADS)]
    o_cols = {qb: [] for qb in range(nq)}
    gate = mod_ref[0][:, 2 * D_MODEL:]

    def finish_tasks(qb0, qb1):
        rows = slice(qb0 * Q_BLOCK, qb1 * Q_BLOCK)
        state = {}

        def gate_values():
            o_all = jnp.concatenate([jnp.concatenate(o_cols[qb], axis=1) for qb in range(qb0, qb1)], axis=0)
            state["o"] = (o_all * sg_ref[0, rows, :].astype(F32)).astype(BF16)
            state["y"] = []

        def project(c0):
            state["y"].append(jnp.dot(state["o"], w_ref[:, c0:c0 + OUT_PROJ_COLS],
                                      preferred_element_type=F32))

        def residual_norm():
            y = jnp.concatenate(state["y"], axis=1)
            x2 = x_ref[0, rows, :] + gate * y
            ms = jnp.mean(x2 * x2, axis=-1, keepdims=True)
            o_ref[0, rows, :] = x2 * lax.rsqrt(ms + NORM_EPS) * fg_ref[...]

        tasks = [gate_values]
        tasks += [functools.partial(project, c0) for c0 in range(0, D_MODEL, OUT_PROJ_COLS)]
        return tasks + [residual_norm]

    pending = [scores(*ch) for ch in chains[:SCORE_LOOKAHEAD]]
    probs = []
    deferred = []

    def values_for(idx):
        qb, kh = chains[idx]
        o_cols[qb] += weighted_values(qb, kh, *probs.pop(0))
        if kh == C_KV_HEADS - 1 and qb % OUT_PROJ_BLOCKS == OUT_PROJ_BLOCKS - 1:
            deferred.extend(finish_tasks(qb + 1 - OUT_PROJ_BLOCKS, qb + 1))

    for idx, (qb, kh) in enumerate(chains):
        if idx + SCORE_LOOKAHEAD < len(chains):
            pending.append(scores(*chains[idx + SCORE_LOOKAHEAD]))
        probs.append(softmax(kh, *pending.pop(0)))
        if idx >= PV_LAG:
            values_for(idx - PV_LAG)
        if deferred:
            deferred.pop(0)()
    for idx in range(len(chains) - PV_LAG, len(chains)):
        values_for(idx)
    for task in deferred:
        task()


def _attn_out_call(sink, q, k, v, kx, vx, sg, xs, mod, w_out, fg):
    bsz, seq_len, _ = xs.shape
    nblk = seq_len // Q_BLOCK
    nq = ATTN_Q_BLOCKS
    nstep = nblk // nq
    lc = kx.shape[1]
    assert nq % OUT_PROJ_BLOCKS == 0
    cur = lambda w: pl.BlockSpec((1, nq * Q_BLOCK, w), lambda b, n: (b, n, 0))
    prev = lambda w: pl.BlockSpec((1, Q_BLOCK, w), lambda b, n: (b, jnp.maximum(nq * n - 1, 0), 0))
    nxt = lambda w: pl.BlockSpec((1, Q_BLOCK, w), lambda b, n: (b, jnp.minimum(nq * n + nq, nblk - 1), 0))
    ctx_blk = lambda w: pl.BlockSpec((1, lc, w), lambda b, n: (b, 0, 0))
    kw = 2 * C_KV_W
    vcur = pl.BlockSpec((1, C_KV_W, nq * Q_BLOCK), lambda b, n: (b, 0, n))
    vprev = pl.BlockSpec((1, C_KV_W, Q_BLOCK), lambda b, n: (b, 0, jnp.maximum(nq * n - 1, 0)))
    vnxt = pl.BlockSpec((1, C_KV_W, Q_BLOCK), lambda b, n: (b, 0, jnp.minimum(nq * n + nq, nblk - 1)))
    vctx = pl.BlockSpec((1, C_KV_W, lc), lambda b, n: (b, 0, 0))
    sink_t = jnp.repeat(sink.reshape(C_KV_HEADS, 1, C_GROUP), Q_BLOCK, axis=2)
    return pl.pallas_call(
        _attn_out_kernel,
        grid=(bsz, nstep),
        in_specs=[
            pl.BlockSpec((C_KV_HEADS, 1, C_GROUP * Q_BLOCK), lambda b, n: (0, 0, 0)),
            cur(C_Q_W), prev(kw), cur(kw), nxt(kw), vprev, vcur, vnxt, ctx_blk(kw), vctx,
            cur(C_Q_W), cur(D_MODEL),
            pl.BlockSpec((1, 1, 3 * D_MODEL), lambda b, n: (b, 0, 0)),
            pl.BlockSpec((C_Q_W, D_MODEL), lambda b, n: (0, 0)),
            pl.BlockSpec((1, D_MODEL), lambda b, n: (0, 0)),
        ],
        out_specs=cur(D_MODEL),
        out_shape=jax.ShapeDtypeStruct((bsz, seq_len, D_MODEL), F32),
        compiler_params=_cparams(("parallel", "arbitrary"), 15, 13),
        name="attn_out",
    )(sink_t, q, k, k, k, v, v, v, kx, vx, sg, xs, mod, w_out, fg)


def kernel(x, c, ctx, c_ctx, norm_g, ada_w, ada_b, w_in_ab, v_norm_g, spatial_w, spatial_b, w_out_ab,
           w_in_c, sink_logit, w_out_c, final_g):
    bsz, seq_len, _ = x.shape
    depth = ada_w.shape[0]
    assert depth == 2 and bsz + 1 <= MOD_ROWS
    ctx_row = bsz

    cc = jnp.concatenate([c, c_ctx[None, :], jnp.zeros((MOD_ROWS - bsz - 1, D_MODEL), F32)], axis=0)
    mod = _mod_call(cc, ada_w, ada_b)
    mod0 = mod[0].reshape(MOD_ROWS, 1, 3 * D_MODEL)
    mod1 = mod[1].reshape(MOD_ROWS, 1, 3 * D_MODEL)

    cs = jnp.asarray(_channel_dft_matrix())
    ng0 = norm_g[0].reshape(1, D_MODEL)
    ng1 = norm_g[1].reshape(1, D_MODEL)
    w_in0 = w_in_ab[0].astype(BF16)
    w_out0 = w_out_ab[0].astype(BF16)
    vg = v_norm_g[0].reshape(1, A_W)
    sw = spatial_w[0].astype(BF16)
    sb = jnp.broadcast_to(spatial_b[0][:, :, None], (A_HEADS, CHUNK, A_HEAD_DIM))

    x1 = _mixer_ab_layer(x, mod0, None, ng0, w_in0, vg, sw, sb, w_out0, cs, tl2=256, tk=256)
    ctx1 = _mixer_ab_layer(ctx, mod0, ctx_row, ng0, w_in0, vg, sw, sb, w_out0, cs,
                           tl2=ctx.shape[1] // RADIX, tk=ctx.shape[1] // RADIX)

    w_in1 = w_in_c[0].astype(BF16)
    w_out1 = w_out_c[0].astype(BF16)
    cos_t, sin_t = _rope_tables(seq_len)
    q, k, v, sg = _in_c_call(x1, mod1, ng1, w_in1, jnp.asarray(cos_t), jnp.asarray(sin_t), tm=1024)
    kx, vx = _ctx_kv_call(ctx1, mod1, ctx_row, ng1, w_in1)
    return _attn_out_call(sink_logit[0], q, k, v, kx, vx, sg, x1, mod1, w_out1,
                          final_g.reshape(1, D_MODEL))
```

```python
import functools
import math

import numpy as np
import jax
import jax.numpy as jnp
from jax import lax
from jax.experimental import pallas as pl
from jax.experimental.pallas import tpu as pltpu

F32 = jnp.float32
BF16 = jnp.bfloat16

D_MODEL = 1024
GRID_W = 64
CHUNK = 128
A_HEADS = 4
A_HEAD_DIM = 128
A_W = A_HEADS * A_HEAD_DIM
B_GROUPS = 4
B_GROUP_DIM = 128
B_W = B_GROUPS * B_GROUP_DIM
AB_IN = 3 * A_W + 2 * B_W
C_HEADS = 16
C_KV_HEADS = 4
C_GROUP = C_HEADS // C_KV_HEADS
C_HEAD_DIM = 64
C_Q_W = C_HEADS * C_HEAD_DIM
C_KV_W = C_KV_HEADS * C_HEAD_DIM
C_IN = 2 * C_Q_W + 2 * C_KV_W
WINDOW = 128
Q_BLOCK = 128
ROPE_BASE = 10000.0
NORM_EPS = 1e-6
NEG_INF = -1e30
LOG2E = math.log2(math.e)
RADIX = 4
MOD_ROWS = 16
SCORE_LOOKAHEAD = 2
OUT_PROJ_COLS = 256
PV_LAG = 1
ATTN_Q_BLOCKS = 4
OUT_PROJ_BLOCKS = 2
V7X_VMEM_LIMIT = 56 * 1024 * 1024


def _silu(x):
    return x * (0.5 + 0.5 * jnp.tanh(0.5 * x))


def _cparams(sem, n_operands=None, weight_operand=None):
    fusion = None
    if weight_operand is not None:
        fusion = [i == weight_operand for i in range(n_operands)]
    return pltpu.CompilerParams(dimension_semantics=sem, vmem_limit_bytes=V7X_VMEM_LIMIT,
                                allow_input_fusion=fusion)


def _channel_dft_matrix():
    n = np.arange(B_GROUP_DIM)
    ang = 2.0 * np.pi * np.outer(n, n) / B_GROUP_DIM
    return np.concatenate([np.cos(ang), np.sin(ang)], axis=1).astype(np.float32)


def _position_dft_matrix(seq_len):
    n2 = seq_len // RADIX
    idx = np.arange(n2)
    ang = 2.0 * np.pi * (np.outer(idx, idx) % n2) / n2
    norm = 1.0 / math.sqrt(seq_len * B_GROUP_DIM)
    return np.concatenate([np.cos(ang) * norm, -np.sin(ang) * norm], axis=1).astype(np.float32)


def _twiddle_tables(seq_len):
    n2 = seq_len // RADIX
    l2 = np.arange(n2)[None, :, None]
    k1 = np.arange(RADIX)[:, None, None]
    ang = 2.0 * np.pi * ((l2 * k1) % seq_len) / seq_len
    ang = np.broadcast_to(ang, (RADIX, n2, 128))
    return np.cos(ang).astype(np.float32), np.sin(ang).astype(np.float32)


def _rope_tables(seq_len):
    t = np.arange(seq_len)
    row = (t // GRID_W).astype(np.float64)
    col = (t % GRID_W).astype(np.float64)
    lane = np.arange(128)
    dd = lane % C_HEAD_DIM
    nf = C_HEAD_DIM // 4
    inv = ROPE_BASE ** (-(dd % nf).astype(np.float64) / nf)
    pos = np.where((dd < C_HEAD_DIM // 2)[None, :], row[:, None], col[:, None])
    ang = pos * inv[None, :]
    sign = np.where((dd % (2 * nf)) < nf, -1.0, 1.0)[None, :]
    return np.cos(ang).astype(np.float32), (np.sin(ang) * sign).astype(np.float32)


def _mod_kernel(c_ref, w_ref, b_ref, o_ref):
    s = _silu(c_ref[...]).astype(BF16)
    o_ref[0] = jnp.dot(s, w_ref[0].astype(BF16), preferred_element_type=F32) + b_ref[0]


def _mod_call(cc, ada_w, ada_b):
    depth = ada_w.shape[0]
    tn = 1024
    return pl.pallas_call(
        _mod_kernel,
        grid=(depth, 3 * D_MODEL // tn),
        in_specs=[
            pl.BlockSpec((MOD_ROWS, D_MODEL), lambda l, j: (0, 0)),
            pl.BlockSpec((1, D_MODEL, tn), lambda l, j: (l, 0, j)),
            pl.BlockSpec((1, 1, tn), lambda l, j: (l, 0, j)),
        ],
        out_specs=pl.BlockSpec((1, MOD_ROWS, tn), lambda l, j: (l, 0, j)),
        out_shape=jax.ShapeDtypeStruct((depth, MOD_ROWS, 3 * D_MODEL), F32),
        compiler_params=_cparams(("arbitrary", "arbitrary")),
        name="adaln_mod",
    )(cc, ada_w, ada_b.reshape(depth, 1, 3 * D_MODEL))


def _modulated_norm(x, mod_row, g):
    shift = mod_row[:, :D_MODEL]
    scale = mod_row[:, D_MODEL:2 * D_MODEL]
    ms = jnp.mean(x * x, axis=-1, keepdims=True)
    h = x * lax.rsqrt(ms + NORM_EPS) * g
    return h * (1.0 + scale) + shift


def _in_ab_kernel(x_ref, mod_ref, ng_ref, w_ref, vg_ref, sw_ref, sb_ref, cs_ref, twc_ref, tws_ref,
                  ya_ref, sgb_ref, uv_ref, *, tl2, ts):
    rows = RADIX * ts
    piece = min(CHUNK, ts)
    nchunk = rows // CHUNK
    assert nchunk % 2 == 0 and tl2 % ts == 0
    cs = cs_ref[...].astype(BF16)

    nb = x_ref.shape[0]

    def project(o):
        x = x_ref[:, :, o:o + ts, :].reshape(nb * rows, D_MODEL)
        h = _modulated_norm(x, mod_ref[0], ng_ref[...])
        return jnp.dot(h.astype(BF16), w_ref[...], preferred_element_type=F32)

    def mix(bb, o, z):
        v = z[:, A_W:2 * A_W]
        mu = jnp.mean(v, axis=-1, keepdims=True)
        vc = v - mu
        var = jnp.mean(vc * vc, axis=-1, keepdims=True)
        vn = (vc * lax.rsqrt(var + NORM_EPS) * vg_ref[...]).astype(BF16)
        for hd in range(A_HEADS):
            c0 = hd * A_HEAD_DIM
            for cp in range(nchunk // 2):
                ra, rb = 2 * cp * CHUNK, (2 * cp + 1) * CHUNK
                vpair = jnp.concatenate([vn[ra:ra + CHUNK, c0:c0 + A_HEAD_DIM],
                                         vn[rb:rb + CHUNK, c0:c0 + A_HEAD_DIM]], axis=1)
                sv2 = jnp.dot(sw_ref[hd], vpair, preferred_element_type=F32)
                for half, r0 in enumerate((ra, rb)):
                    sv = sv2[:, half * A_HEAD_DIM:(half + 1) * A_HEAD_DIM] + sb_ref[hd]
                    u = z[r0:r0 + CHUNK, c0:c0 + A_HEAD_DIM]
                    ga = z[r0:r0 + CHUNK, 2 * A_W + c0:2 * A_W + c0 + A_HEAD_DIM]
                    ya = (u * sv * _silu(ga)).astype(BF16)
                    for p0 in range(0, CHUNK, piece):
                        l1, off = divmod(r0 + p0, ts)
                        ya_ref[bb, l1, o + off:o + off + piece, c0:c0 + A_HEAD_DIM] = ya[p0:p0 + piece]

        gb = z[:, 3 * A_W + B_W:]
        sgb_ref[bb, :, o:o + ts, :] = _silu(gb).astype(BF16).reshape(RADIX, ts, B_W)

        xb = z[:, 3 * A_W:3 * A_W + B_W].astype(BF16)
        for g in range(B_GROUPS):
            c0 = g * B_GROUP_DIM
            ps, qs = [], []
            for j in range(RADIX):
                pq = jnp.dot(xb[j * ts:(j + 1) * ts, c0:c0 + B_GROUP_DIM], cs,
                             preferred_element_type=F32)
                ps.append(pq[:, :B_GROUP_DIM])
                qs.append(pq[:, B_GROUP_DIM:])
            p02, p13 = ps[0] - ps[2], ps[1] - ps[3]
            q02, q13 = qs[0] - qs[2], qs[1] - qs[3]
            pe, po = ps[0] + ps[2], ps[1] + ps[3]
            qe, qo = qs[0] + qs[2], qs[1] + qs[3]
            us = [pe + po, p02 - q13, pe - po, p02 + q13]
            vs = [qe + qo, q02 + p13, qe - qo, q02 - p13]
            for k1 in range(RADIX):
                if k1 == 0:
                    ut, vt = us[0], vs[0]
                else:
                    tc, tsn = twc_ref[k1, o:o + ts, :], tws_ref[k1, o:o + ts, :]
                    ut = us[k1] * tc - vs[k1] * tsn
                    vt = us[k1] * tsn + vs[k1] * tc
                uv_ref[bb, k1, 0, o:o + ts, c0:c0 + B_GROUP_DIM] = ut.astype(BF16)
                uv_ref[bb, k1, 1, o:o + ts, c0:c0 + B_GROUP_DIM] = vt.astype(BF16)

    for o in range(0, tl2, ts):
        z = project(o)
        for bb in range(nb):
            mix(bb, o, z[bb * rows:(bb + 1) * rows])


def _streams_per_step(bsz, mod_row):
    return 4 if (mod_row is not None and bsz % 4 == 0) else 1


def _in_ab_call(xs, mod, mod_row, ng, w_in, vg, sw, sb, cs, twc, tws, tl2):
    bsz, seq_len, _ = xs.shape
    n2 = seq_len // RADIX
    x4 = xs.reshape(bsz, RADIX, n2, D_MODEL)
    row_of = (lambda b: b) if mod_row is None else (lambda b: mod_row)
    const2 = lambda b, i: (0, 0)
    const3 = lambda b, i: (0, 0, 0)
    nb = _streams_per_step(bsz, mod_row)
    return pl.pallas_call(
        functools.partial(_in_ab_kernel, tl2=tl2, ts=tl2),
        grid=(bsz // nb, n2 // tl2),
        in_specs=[
            pl.BlockSpec((nb, RADIX, tl2, D_MODEL), lambda b, i: (b, 0, i, 0)),
            pl.BlockSpec((1, 1, 3 * D_MODEL), lambda b, i: (row_of(b), 0, 0)),
            pl.BlockSpec((1, D_MODEL), const2),
            pl.BlockSpec((D_MODEL, AB_IN), const2),
            pl.BlockSpec((1, A_W), const2),
            pl.BlockSpec((A_HEADS, CHUNK, CHUNK), const3),
            pl.BlockSpec((A_HEADS, CHUNK, A_HEAD_DIM), const3),
            pl.BlockSpec((B_GROUP_DIM, 2 * B_GROUP_DIM), const2),
            pl.BlockSpec((RADIX, tl2, 128), lambda b, i: (0, i, 0)),
            pl.BlockSpec((RADIX, tl2, 128), lambda b, i: (0, i, 0)),
        ],
        out_specs=[
            pl.BlockSpec((nb, RADIX, tl2, A_W), lambda b, i: (b, 0, i, 0)),
            pl.BlockSpec((nb, RADIX, tl2, B_W), lambda b, i: (b, 0, i, 0)),
            pl.BlockSpec((nb, RADIX, 2, tl2, B_W), lambda b, i: (b, 0, 0, i, 0)),
        ],
        out_shape=[
            jax.ShapeDtypeStruct((bsz, RADIX, n2, A_W), BF16),
            jax.ShapeDtypeStruct((bsz, RADIX, n2, B_W), BF16),
            jax.ShapeDtypeStruct((bsz, RADIX, 2, n2, B_W), BF16),
        ],
        compiler_params=_cparams(("parallel", "arbitrary"), 10, 3),
        name="in_ab",
    )(x4, mod, ng, w_in, vg, sw, sb, cs, twc, tws)


def _dft_out_kernel(g_ref, uv_ref, ya_ref, sgb_ref, x_ref, mod_ref, w_ref, o_ref, f_ref, *, tk):
    nb, rows, _ = x_ref.shape
    r0 = pl.multiple_of(pl.program_id(1) * tk, tk)
    gmat = g_ref[pl.ds(r0, tk), :].astype(BF16)
    for bb in range(nb):
        for k1 in range(RADIX):
            f = jnp.dot(gmat, uv_ref[bb, k1], preferred_element_type=F32)
            for g in range(B_GROUPS):
                f_ref[g, pl.ds(bb * rows + k1, tk, stride=RADIX), :] = f[:, g * B_GROUP_DIM:(g + 1) * B_GROUP_DIM]
    fnat = jnp.concatenate([f_ref[g] for g in range(B_GROUPS)], axis=1)
    yb = (fnat * sgb_ref[...].reshape(nb * rows, B_W).astype(F32)).astype(BF16)
    y = jnp.dot(ya_ref[...].reshape(nb * rows, A_W), w_ref[:A_W], preferred_element_type=F32)
    y = y + jnp.dot(yb, w_ref[A_W:], preferred_element_type=F32)
    gate = mod_ref[0][:, 2 * D_MODEL:]
    o_ref[...] = x_ref[...] + (gate * y).reshape(nb, rows, D_MODEL)


def _dft_out_call(gmat, uv, ya, sgb, xs, mod, mod_row, w_out, tk):
    bsz, seq_len, _ = xs.shape
    n2 = seq_len // RADIX
    rows = RADIX * tk
    row_of = (lambda b: b) if mod_row is None else (lambda b: mod_row)
    nb = _streams_per_step(bsz, mod_row)
    nat = lambda w: pl.BlockSpec((nb, rows, w), lambda b, i: (b, i, 0))
    return pl.pallas_call(
        functools.partial(_dft_out_kernel, tk=tk),
        grid=(bsz // nb, n2 // tk),
        in_specs=[
            pl.BlockSpec((n2, 2 * n2), lambda b, i: (0, 0), pipeline_mode=pl.Buffered(1)),
            pl.BlockSpec((nb, RADIX, 2 * n2, B_W), lambda b, i: (b, 0, 0, 0)),
            nat(A_W), nat(B_W), nat(D_MODEL),
            pl.BlockSpec((1, 1, 3 * D_MODEL), lambda b, i: (row_of(b), 0, 0)),
            pl.BlockSpec((A_W + B_W, D_MODEL), lambda b, i: (0, 0), pipeline_mode=pl.Buffered(1)),
        ],
        out_specs=nat(D_MODEL),
        out_shape=jax.ShapeDtypeStruct((bsz, seq_len, D_MODEL), F32),
        scratch_shapes=[pltpu.VMEM((B_GROUPS, nb * rows, B_GROUP_DIM), F32)],
        compiler_params=_cparams(("parallel", "arbitrary"), 7, 6),
        name="dft_out",
    )(gmat, uv.reshape(bsz, RADIX, 2 * n2, B_W), ya.reshape(bsz, seq_len, A_W),
      sgb.reshape(bsz, seq_len, B_W), xs, mod, w_out)


def _mixer_ab_layer(xs, mod, mod_row, ng, w_in, vg, sw, sb, w_out, cs, tl2, tk):
    seq_len = xs.shape[1]
    twc, tws = _twiddle_tables(seq_len)
    gmat = jnp.asarray(_position_dft_matrix(seq_len))
    ya, sgb, uv = _in_ab_call(xs, mod, mod_row, ng, w_in, vg, sw, sb, cs,
                              jnp.asarray(twc), jnp.asarray(tws), tl2)
    return _dft_out_call(gmat, uv, ya, sgb, xs, mod, mod_row, w_out, tk=tk)


def _rope_block(t, cos, sin_signed, lane_lo):
    nf = C_HEAD_DIM // 4
    swapped = jnp.where(lane_lo, pltpu.roll(t, 128 - nf, axis=1), pltpu.roll(t, nf, axis=1))
    return t * cos + swapped * sin_signed


def _store_dup_heads(k_ref, c, t, lane):
    r = pltpu.roll(t, C_HEAD_DIM, axis=1)
    first = lane < C_HEAD_DIM
    k_ref[0, :, (2 * c) * 128:(2 * c + 1) * 128] = jnp.where(first, t, r).astype(BF16)
    k_ref[0, :, (2 * c + 1) * 128:(2 * c + 2) * 128] = jnp.where(first, r, t).astype(BF16)


def _in_c_kernel(x_ref, mod_ref, ng_ref, w_ref, cos_ref, sin_ref, q_ref, k_ref, v_ref, sg_ref):
    h = _modulated_norm(x_ref[0], mod_ref[0], ng_ref[...])
    z = jnp.dot(h.astype(BF16), w_ref[...], preferred_element_type=F32)
    cos = cos_ref[...]
    sin = sin_ref[...]
    lane = lax.broadcasted_iota(jnp.int32, cos.shape, 1)
    lane_lo = (lane % (C_HEAD_DIM // 2)) < (C_HEAD_DIM // 4)
    qscale = C_HEAD_DIM ** -0.5 * LOG2E
    for c in range(C_Q_W // 128):
        t = _rope_block(z[:, c * 128:(c + 1) * 128], cos, sin, lane_lo)
        q_ref[0, :, c * 128:(c + 1) * 128] = (t * qscale).astype(BF16)
    for c in range(C_KV_W // 128):
        t = _rope_block(z[:, C_Q_W + c * 128:C_Q_W + (c + 1) * 128], cos, sin, lane_lo)
        _store_dup_heads(k_ref, c, t, lane)
    v_ref[0] = z[:, C_Q_W + C_KV_W:C_Q_W + 2 * C_KV_W].T.astype(BF16)
    sg_ref[0] = z[:, C_Q_W + 2 * C_KV_W:].astype(BF16)


def _in_c_call(xs, mod, ng, w_in, cos_t, sin_t, tm):
    bsz, seq_len, _ = xs.shape
    const2 = lambda b, i: (0, 0)
    row_blk = lambda w: pl.BlockSpec((1, tm, w), lambda b, i: (b, i, 0))
    return pl.pallas_call(
        _in_c_kernel,
        grid=(bsz, seq_len // tm),
        in_specs=[
            row_blk(D_MODEL),
            pl.BlockSpec((1, 1, 3 * D_MODEL), lambda b, i: (b, 0, 0)),
            pl.BlockSpec((1, D_MODEL), const2),
            pl.BlockSpec((D_MODEL, C_IN), const2),
            pl.BlockSpec((tm, 128), lambda b, i: (i, 0)),
            pl.BlockSpec((tm, 128), lambda b, i: (i, 0)),
        ],
        out_specs=[row_blk(C_Q_W), row_blk(2 * C_KV_W),
                   pl.BlockSpec((1, C_KV_W, tm), lambda b, i: (b, 0, i)), row_blk(C_Q_W)],
        out_shape=[
            jax.ShapeDtypeStruct((bsz, seq_len, C_Q_W), BF16),
            jax.ShapeDtypeStruct((bsz, seq_len, 2 * C_KV_W), BF16),
            jax.ShapeDtypeStruct((bsz, C_KV_W, seq_len), BF16),
            jax.ShapeDtypeStruct((bsz, seq_len, C_Q_W), BF16),
        ],
        compiler_params=_cparams(("parallel", "arbitrary"), 6, 3),
        name="in_c",
    )(xs, mod, ng, w_in, cos_t, sin_t)


def _ctx_kv_kernel(x_ref, mod_ref, ng_ref, w_ref, k_ref, v_ref):
    nb, lc, _ = x_ref.shape
    h = _modulated_norm(x_ref[...].reshape(nb * lc, D_MODEL), mod_ref[0], ng_ref[...])
    z = jnp.dot(h.astype(BF16), w_ref[...], preferred_element_type=F32)
    lane = lax.broadcasted_iota(jnp.int32, (lc, 128), 1)
    for bb in range(nb):
        zb = z[bb * lc:(bb + 1) * lc]
        for c in range(C_KV_W // 128):
            _store_dup_heads(k_ref.at[pl.ds(bb, 1)], c, zb[:, c * 128:(c + 1) * 128], lane)
        v_ref[bb] = zb[:, C_KV_W:].T.astype(BF16)


def _ctx_kv_call(ctx, mod, mod_row, ng, w_kv):
    bsz, lc, _ = ctx.shape
    nb = 4 if bsz % 4 == 0 else 1
    assert C_Q_W % (2 * C_KV_W) == 0
    return pl.pallas_call(
        _ctx_kv_kernel,
        grid=(bsz // nb,),
        in_specs=[
            pl.BlockSpec((nb, lc, D_MODEL), lambda b: (b, 0, 0)),
            pl.BlockSpec((1, 1, 3 * D_MODEL), lambda b: (mod_row, 0, 0)),
            pl.BlockSpec((1, D_MODEL), lambda b: (0, 0)),
            pl.BlockSpec((D_MODEL, 2 * C_KV_W), lambda b: (0, C_Q_W // (2 * C_KV_W))),
        ],
        out_specs=[pl.BlockSpec((nb, lc, 2 * C_KV_W), lambda b: (b, 0, 0)),
                   pl.BlockSpec((nb, C_KV_W, lc), lambda b: (b, 0, 0))],
        out_shape=[jax.ShapeDtypeStruct((bsz, lc, 2 * C_KV_W), BF16),
                   jax.ShapeDtypeStruct((bsz, C_KV_W, lc), BF16)],
        compiler_params=_cparams(("parallel",), 4, 3),
        name="ctx_kv",
    )(ctx, mod, ng, w_kv)


def _attn_out_kernel(sink_ref, q_ref, kp_ref, kc_ref, kn_ref, vp_ref, vc_ref, vn_ref, kx_ref, vx_ref,
                     sg_ref, x_ref, mod_ref, w_ref, fg_ref, o_ref):
    n = pl.program_id(1)
    last = pl.num_programs(1) - 1
    gq = C_GROUP * Q_BLOCK
    kj = lax.broadcasted_iota(jnp.int32, (Q_BLOCK, Q_BLOCK), 0)
    qi = lax.broadcasted_iota(jnp.int32, (Q_BLOCK, Q_BLOCK), 1)
    bias_before = jnp.where(kj >= qi, 0.0, NEG_INF).astype(F32)
    bias_after = jnp.where(kj <= qi, 0.0, NEG_INF).astype(F32)
    edge_first = jnp.where(n > 0, 0.0, NEG_INF).astype(F32)
    edge_last = jnp.where(n < last, 0.0, NEG_INF).astype(F32)
    lane = lax.broadcasted_iota(jnp.int32, (Q_BLOCK, 128), 1)
    first = lane < C_HEAD_DIM
    zero = jnp.zeros((Q_BLOCK, 128), BF16)
    row_k = lax.broadcasted_iota(jnp.int32, (128, kx_ref.shape[1] + 3 * Q_BLOCK), 0)
    nq = q_ref.shape[1] // Q_BLOCK

    def window(prev_ref, cur_ref, next_ref, j, lanes):
        if j < 0:
            return prev_ref[0, :, lanes]
        if j >= nq:
            return next_ref[0, :, lanes]
        return cur_ref[0, j * Q_BLOCK:(j + 1) * Q_BLOCK, lanes]

    def scores(qb, kh):
        kl = slice(kh * 128, (kh + 1) * 128)
        kblocks = [kx_ref[0, :, kl]] + [window(kp_ref, kc_ref, kn_ref, j, kl) for j in (qb - 1, qb, qb + 1)]
        biases = (None, (bias_before, edge_first if qb == 0 else None), None,
                  (bias_after, edge_last if qb == nq - 1 else None))
        rows_q = slice(qb * Q_BLOCK, (qb + 1) * Q_BLOCK)
        q4 = []
        for c in range(2):
            qv = q_ref[0, rows_q, kh * 256 + c * 128:kh * 256 + (c + 1) * 128]
            q4 += [jnp.where(first, qv, zero), jnp.where(first, zero, qv)]
        q4 = jnp.concatenate(q4, axis=0)
        k2 = jnp.concatenate(kblocks, axis=0)
        st = lax.dot_general(k2, q4, (((1,), (1,)), ((), ())), preferred_element_type=F32)
        blocks, m8, r = [], None, 0
        for kb, bias in zip(kblocks, biases):
            sb = st[r:r + kb.shape[0]]
            r += kb.shape[0]
            if bias is not None:
                mask_bias, edge = bias
                cols = [sb[:, j * Q_BLOCK:(j + 1) * Q_BLOCK] + mask_bias for j in range(C_GROUP)]
                sb = jnp.concatenate(cols, axis=1)
                if edge is not None:
                    sb = sb + edge
            blocks.append(sb)
            mb = jnp.max(sb.reshape(sb.shape[0] // 8, 8, gq), axis=0)
            m8 = mb if m8 is None else jnp.maximum(m8, mb)
        return blocks, m8

    def softmax(kh, blocks, m8):
        sink2 = sink_ref[kh] * LOG2E
        m = jnp.maximum(jnp.max(m8, axis=0, keepdims=True), sink2)
        pt = jnp.concatenate([jnp.exp2(blk - m).astype(BF16) for blk in blocks], axis=0)
        return pt, jnp.exp2(sink2 - m)

    def weighted_values(qb, kh, pt, sink_term):
        vrows = slice((kh // 2) * 128, (kh // 2 + 1) * 128)
        vblocks = [vx_ref[0, vrows, :]]
        for j in (qb - 1, qb, qb + 1):
            if j < 0:
                vblocks.append(vp_ref[0, vrows, :])
            elif j >= nq:
                vblocks.append(vn_ref[0, vrows, :])
            else:
                vblocks.append(vc_ref[0, vrows, j * Q_BLOCK:(j + 1) * Q_BLOCK])
        vpair = jnp.concatenate(vblocks, axis=1)
        own = (row_k < C_HEAD_DIM) if kh % 2 == 0 else (row_k >= C_HEAD_DIM)
        vsum = jnp.where(own, vpair, jnp.ones_like(vpair))
        ot = jnp.dot(vsum, pt, preferred_element_type=F32)
        r0 = (kh % 2) * C_HEAD_DIM
        r1 = C_HEAD_DIM - r0
        denom = ot[r1:r1 + C_HEAD_DIM] + sink_term
        ot = ot[r0:r0 + C_HEAD_DIM] / denom
        cols = []
        for c in range(2):
            pair = jnp.concatenate([ot[:, (2 * c) * Q_BLOCK:(2 * c + 1) * Q_BLOCK],
                                    ot[:, (2 * c + 1) * Q_BLOCK:(2 * c + 2) * Q_BLOCK]], axis=0)
            cols.append(pair.T)
        return cols

    chains = [(qb, kh) for qb in range(nq) for kh in range(C_KV_HEADS)]
    o_cols = {qb: [] for qb in range(nq)}
    gate = mod_ref[0][:, 2 * D_MODEL:]

    def finish_tasks(qb0, qb1):
        rows = slice(qb0 * Q_BLOCK, qb1 * Q_BLOCK)
        state = {}

        def gate_values():
            o_all = jnp.concatenate([jnp.concatenate(o_cols[qb], axis=1) for qb in range(qb0, qb1)], axis=0)
            state["o"] = (o_all * _silu(sg_ref[0, rows, :].astype(F32))).astype(BF16)
            state["y"] = []

        def project(c0):
            state["y"].append(jnp.dot(state["o"], w_ref[:, c0:c0 + OUT_PROJ_COLS],
                                      preferred_element_type=F32))

        def residual_norm():
            y = jnp.concatenate(state["y"], axis=1)
            x2 = x_ref[0, rows, :] + gate * y
            ms = jnp.mean(x2 * x2, axis=-1, keepdims=True)
            o_ref[0, rows, :] = x2 * lax.rsqrt(ms + NORM_EPS) * fg_ref[...]

        tasks = [gate_values]
        tasks += [functools.partial(project, c0) for c0 in range(0, D_MODEL, OUT_PROJ_COLS)]
        return tasks + [residual_norm]

    pending = [scores(*ch) for ch in chains[:SCORE_LOOKAHEAD]]
    probs = []
    deferred = []

    def values_for(idx):
        qb, kh = chains[idx]
        o_cols[qb] += weighted_values(qb, kh, *probs.pop(0))
        if kh == C_KV_HEADS - 1 and qb % OUT_PROJ_BLOCKS == OUT_PROJ_BLOCKS - 1:
            deferred.extend(finish_tasks(qb + 1 - OUT_PROJ_BLOCKS, qb + 1))

    for idx, (qb, kh) in enumerate(chains):
        if idx + SCORE_LOOKAHEAD < len(chains):
            pending.append(scores(*chains[idx + SCORE_LOOKAHEAD]))
        probs.append(softmax(kh, *pending.pop(0)))
        if idx >= PV_LAG:
            values_for(idx - PV_LAG)
        if deferred:
            deferred.pop(0)()
    for idx in range(len(chains) - PV_LAG, len(chains)):
        values_for(idx)
    for task in deferred:
        task()


def _attn_out_call(sink, q, k, v, kx, vx, sg, xs, mod, w_out, fg):
    bsz, seq_len, _ = xs.shape
    nblk = seq_len // Q_BLOCK
    nq = ATTN_Q_BLOCKS
    nstep = nblk // nq
    lc = kx.shape[1]
    assert nq % OUT_PROJ_BLOCKS == 0
    cur = lambda w: pl.BlockSpec((1, nq * Q_BLOCK, w), lambda b, n: (b, n, 0))
    prev = lambda w: pl.BlockSpec((1, Q_BLOCK, w), lambda b, n: (b, jnp.maximum(nq * n - 1, 0), 0))
    nxt = lambda w: pl.BlockSpec((1, Q_BLOCK, w), lambda b, n: (b, jnp.minimum(nq * n + nq, nblk - 1), 0))
    ctx_blk = lambda w: pl.BlockSpec((1, lc, w), lambda b, n: (b, 0, 0))
    kw = 2 * C_KV_W
    vcur = pl.BlockSpec((1, C_KV_W, nq * Q_BLOCK), lambda b, n: (b, 0, n))
    vprev = pl.BlockSpec((1, C_KV_W, Q_BLOCK), lambda b, n: (b, 0, jnp.maximum(nq * n - 1, 0)))
    vnxt = pl.BlockSpec((1, C_KV_W, Q_BLOCK), lambda b, n: (b, 0, jnp.minimum(nq * n + nq, nblk - 1)))
    vctx = pl.BlockSpec((1, C_KV_W, lc), lambda b, n: (b, 0, 0))
    sink_t = jnp.repeat(sink.reshape(C_KV_HEADS, 1, C_GROUP), Q_BLOCK, axis=2)
    return pl.pallas_call(
        _attn_out_kernel,
        grid=(bsz, nstep),
        in_specs=[
            pl.BlockSpec((C_KV_HEADS, 1, C_GROUP * Q_BLOCK), lambda b, n: (0, 0, 0)),
            cur(C_Q_W), prev(kw), cur(kw), nxt(kw), vprev, vcur, vnxt, ctx_blk(kw), vctx,
            cur(C_Q_W), cur(D_MODEL),
            pl.BlockSpec((1, 1, 3 * D_MODEL), lambda b, n: (b, 0, 0)),
            pl.BlockSpec((C_Q_W, D_MODEL), lambda b, n: (0, 0)),
            pl.BlockSpec((1, D_MODEL), lambda b, n: (0, 0)),
        ],
        out_specs=cur(D_MODEL),
        out_shape=jax.ShapeDtypeStruct((bsz, seq_len, D_MODEL), F32),
        compiler_params=_cparams(("parallel", "arbitrary"), 15, 13),
        name="attn_out",
    )(sink_t, q, k, k, k, v, v, v, kx, vx, sg, xs, mod, w_out, fg)


def kernel(x, c, ctx, c_ctx, norm_g, ada_w, ada_b, w_in_ab, v_norm_g, spatial_w, spatial_b, w_out_ab,
           w_in_c, sink_logit, w_out_c, final_g):
    bsz, seq_len, _ = x.shape
    depth = ada_w.shape[0]
    assert depth == 2 and bsz + 1 <= MOD_ROWS
    ctx_row = bsz

    cc = jnp.concatenate([c, c_ctx[None, :], jnp.zeros((MOD_ROWS - bsz - 1, D_MODEL), F32)], axis=0)
    mod = _mod_call(cc, ada_w, ada_b)
    mod0 = mod[0].reshape(MOD_ROWS, 1, 3 * D_MODEL)
    mod1 = mod[1].reshape(MOD_ROWS, 1, 3 * D_MODEL)

    cs = jnp.asarray(_channel_dft_matrix())
    ng0 = norm_g[0].reshape(1, D_MODEL)
    ng1 = norm_g[1].reshape(1, D_MODEL)
    w_in0 = w_in_ab[0].astype(BF16)
    w_out0 = w_out_ab[0].astype(BF16)
    vg = v_norm_g[0].reshape(1, A_W)
    sw = spatial_w[0].astype(BF16)
    sb = jnp.broadcast_to(spatial_b[0][:, :, None], (A_HEADS, CHUNK, A_HEAD_DIM))

    x1 = _mixer_ab_layer(x, mod0, None, ng0, w_in0, vg, sw, sb, w_out0, cs, tl2=256, tk=256)
    ctx1 = _mixer_ab_layer(ctx, mod0, ctx_row, ng0, w_in0, vg, sw, sb, w_out0, cs,
                           tl2=ctx.shape[1] // RADIX, tk=ctx.shape[1] // RADIX)

    w_in1 = w_in_c[0].astype(BF16)
    w_out1 = w_out_c[0].astype(BF16)
    cos_t, sin_t = _rope_tables(seq_len)
    q, k, v, sg = _in_c_call(x1, mod1, ng1, w_in1, jnp.asarray(cos_t), jnp.asarray(sin_t), tm=1024)
    kx, vx = _ctx_kv_call(ctx1, mod1, ctx_row, ng1, w_in1)
    return _attn_out_call(sink_logit[0], q, k, v, kx, vx, sg, x1, mod1, w_out1,
                          final_g.reshape(1, D_MODEL))
```
